```python
import math
import jax, jax.numpy as jnp
from jax import lax
import numpy as np

D_MODEL = 1024
BATCH = 8
SEQ = 4096
DEPTH = 4

D_A = D_MODEL
CONV_A_WIDTH = 3
N_POOL_GROUPS = 4
POOL_GROUP_DIM = D_MODEL // N_POOL_GROUPS
POOL_WINDOWS = (2, 4, 8, 16)
D_C = D_MODEL
CONV_C_WIDTH = 31
N_BRANCHES = 3
IN_SPLITS = (D_A, D_A, D_A, D_MODEL, D_C, D_C, N_BRANCHES * D_MODEL)
D_IN = sum(IN_SPLITS)
N_EXPERTS = 32
TOP_K = 4
D_FF = D_MODEL
SWIGLU_LIMIT = 7.0
SWIGLU_ALPHA = 1.702
EXPERT_BLOCK = 128
LN_EPS = 1e-5
DEEPNORM_ALPHA = (2.0 * DEPTH) ** 0.25
DEEPNORM_BETA = (8.0 * DEPTH) ** -0.25

kernel_name = "hybrid_gated_conv_pool_conformer_moe_deepnorm"


def layer_norm(x, g, b):
    xf = x.astype(jnp.float32)
    mu = jnp.mean(xf, axis=-1, keepdims=True)
    var = jnp.mean(jnp.square(xf - mu), axis=-1, keepdims=True)
    y = (xf - mu) * lax.rsqrt(var + LN_EPS) * g.astype(jnp.float32) + b.astype(jnp.float32)
    return y.astype(x.dtype)


def causal_dwconv(u, w):
    k, c = w.shape
    return lax.conv_general_dilated(
        u, w[:, None, :].astype(u.dtype), window_strides=(1,), padding=[(k - 1, 0)],
        dimension_numbers=("NWC", "WIO", "NWC"), feature_group_count=c)


def causal_multiscale_pool(p):
    bsz, s, _ = p.shape
    pg = p.reshape(bsz, s, N_POOL_GROUPS, POOL_GROUP_DIM).astype(jnp.float32)
    csp = jnp.pad(jnp.cumsum(pg, axis=1), ((0, 0), (1, 0), (0, 0), (0, 0)))
    t = jnp.arange(s)
    outs = []
    for g, w in enumerate(POOL_WINDOWS):
        c = csp[:, :, g]
        upper = c[:, 1:]
        lower = jnp.pad(c, ((0, 0), (w - 1, 0), (0, 0)))[:, :s]
        cnt = jnp.minimum(t + 1, w).astype(jnp.float32)[None, :, None]
        outs.append((upper - lower) / cnt)
    pooled = jnp.stack(outs, axis=2)
    return pooled - pg


def token_mixer(x, w_in, b_in, conv_a, w_out_a, w_pool, scale_pool,
                conv_c, conv_c_b, ln_c_g, ln_c_b, w_out_c, b_out_c, w_o):
    bsz, s, d = x.shape
    z = jnp.einsum("bsd,de->bse", x, w_in) + b_in
    idx = np.cumsum(IN_SPLITS)[:-1].tolist()
    gb_a, gc_a, h_a, p_in, glu_a, glu_b, gate_logits = jnp.split(z, idx, axis=-1)

    u_a = causal_dwconv(gc_a * h_a, conv_a)
    y_a = jnp.einsum("bsc,cd->bsd", gb_a * u_a, w_out_a)

    pooled = causal_multiscale_pool(p_in).astype(x.dtype)
    y_b = jnp.einsum("bsgc,gce->bsge", pooled, w_pool).reshape(bsz, s, d) * scale_pool

    v = glu_a * jax.nn.sigmoid(glu_b)
    v = causal_dwconv(v, conv_c) + conv_c_b
    v = jax.nn.silu(layer_norm(v, ln_c_g, ln_c_b))
    y_c = jnp.einsum("bsc,cd->bsd", v, w_out_c) + b_out_c

    g = jax.nn.sigmoid(gate_logits).reshape(bsz, s, N_BRANCHES, d)
    merged = g[:, :, 0] * y_a + g[:, :, 1] * y_b + g[:, :, 2] * y_c
    return jnp.einsum("bsd,de->bse", merged, w_o)


def moe_ffn(x2, w_router, b_router, w_gu, b_gu, w_down, b_down):
    t_tok, d = x2.shape
    logits = jnp.einsum("td,de->te", x2, w_router).astype(jnp.float32) + b_router.astype(jnp.float32)
    top_vals, top_idx = lax.top_k(logits, TOP_K)
    gates = jax.nn.softmax(top_vals, axis=-1)

    n_assign = t_tok * TOP_K
    flat_e = top_idx.reshape(n_assign)
    order = jnp.argsort(flat_e)
    sorted_e = flat_e[order]
    sorted_tok = (order // TOP_K).astype(jnp.int32)
    sorted_gate = gates.reshape(n_assign)[order]

    counts = jnp.bincount(flat_e, length=N_EXPERTS)
    blocks_per_e = (counts + EXPERT_BLOCK - 1) // EXPERT_BLOCK
    blk_end = jnp.cumsum(blocks_per_e)
    blk_start = blk_end - blocks_per_e
    grp_start = jnp.cumsum(counts) - counts
    rank = jnp.arange(n_assign) - grp_start[sorted_e]
    dest = blk_start[sorted_e] * EXPERT_BLOCK + rank

    n_blocks = -(-n_assign // EXPERT_BLOCK) + N_EXPERTS
    n_slots = n_blocks * EXPERT_BLOCK
    slot_tok = jnp.full((n_slots,), t_tok, jnp.int32).at[dest].set(sorted_tok)
    slot_gate = jnp.zeros((n_slots,), jnp.float32).at[dest].set(sorted_gate)
    block_e = jnp.minimum(jnp.searchsorted(blk_end, jnp.arange(n_blocks), side="right"),
                          N_EXPERTS - 1).astype(jnp.int32)

    x_pad = jnp.concatenate([x2, jnp.zeros((1, d), x2.dtype)], axis=0)

    def expert_block(args):
        tok, e = args
        xb = x_pad[tok]
        h = xb @ w_gu[e] + b_gu[e]
        gate = jnp.minimum(h[:, 0::2], SWIGLU_LIMIT)
        up = jnp.clip(h[:, 1::2], -SWIGLU_LIMIT, SWIGLU_LIMIT)
        act = (up + 1.0) * (gate * jax.nn.sigmoid(SWIGLU_ALPHA * gate))
        return act @ w_down[e] + b_down[e]

    y_blocks = lax.map(expert_block, (slot_tok.reshape(n_blocks, EXPERT_BLOCK), block_e))
    y = y_blocks.reshape(n_slots, d).astype(jnp.float32) * slot_gate[:, None]
    out = jax.ops.segment_sum(y, slot_tok, num_segments=t_tok + 1)[:t_tok]
    return out.astype(x2.dtype)


def setup_inputs(seed: int = 0) -> dict:
    key = jax.random.key(seed)
    ks = jax.random.split(key, 24)
    f32 = jnp.float32
    nrm = lambda k, shape, scale: jax.random.normal(k, shape, f32) * scale
    L = DEPTH
    return {
        "x": jax.random.normal(ks[0], (BATCH, SEQ, D_MODEL), f32),
        "w_in": nrm(ks[1], (L, D_MODEL, D_IN), D_MODEL ** -0.5),
        "b_in": nrm(ks[2], (L, D_IN), 0.02),
        "conv_a": nrm(ks[3], (L, CONV_A_WIDTH, D_A), CONV_A_WIDTH ** -0.5),
        "w_out_a": nrm(ks[4], (L, D_A, D_MODEL), D_A ** -0.5),
        "w_pool": nrm(ks[5], (L, N_POOL_GROUPS, POOL_GROUP_DIM, POOL_GROUP_DIM), POOL_GROUP_DIM ** -0.5),
        "scale_pool": 1.0 + nrm(ks[6], (L, D_MODEL), 0.1),
        "conv_c": nrm(ks[7], (L, CONV_C_WIDTH, D_C), CONV_C_WIDTH ** -0.5),
        "conv_c_b": nrm(ks[8], (L, D_C), 0.02),
        "ln_c_g": 1.0 + nrm(ks[9], (L, D_C), 0.05),
        "ln_c_b": nrm(ks[10], (L, D_C), 0.02),
        "w_out_c": nrm(ks[11], (L, D_C, D_MODEL), D_C ** -0.5),
        "b_out_c": nrm(ks[12], (L, D_MODEL), 0.02),
        "w_o": nrm(ks[13], (L, D_MODEL, D_MODEL), D_MODEL ** -0.5 * DEEPNORM_BETA),
        "ln1_g": 1.0 + nrm(ks[14], (L, D_MODEL), 0.05),
        "ln1_b": nrm(ks[15], (L, D_MODEL), 0.02),
        "w_router": nrm(ks[16], (L, D_MODEL, N_EXPERTS), D_MODEL ** -0.5),
        "b_router": nrm(ks[17], (L, N_EXPERTS), 0.01),
        "w_gu": nrm(ks[18], (L, N_EXPERTS, D_MODEL, 2 * D_FF), D_MODEL ** -0.5),
        "b_gu": nrm(ks[19], (L, N_EXPERTS, 2 * D_FF), 0.02),
        "w_down": nrm(ks[20], (L, N_EXPERTS, D_FF, D_MODEL), D_FF ** -0.5 * DEEPNORM_BETA),
        "b_down": nrm(ks[21], (L, N_EXPERTS, D_MODEL), 0.02),
        "ln2_g": 1.0 + nrm(ks[22], (L, D_MODEL), 0.05),
        "ln2_b": nrm(ks[23], (L, D_MODEL), 0.02),
    }


def reference(x, w_in, b_in, conv_a, w_out_a, w_pool, scale_pool, conv_c, conv_c_b,
              ln_c_g, ln_c_b, w_out_c, b_out_c, w_o, ln1_g, ln1_b, w_router, b_router,
              w_gu, b_gu, w_down, b_down, ln2_g, ln2_b):
    bsz, s, d = x.shape
    for l in range(DEPTH):
        h = token_mixer(x, w_in[l], b_in[l], conv_a[l], w_out_a[l], w_pool[l], scale_pool[l],
                        conv_c[l], conv_c_b[l], ln_c_g[l], ln_c_b[l], w_out_c[l], b_out_c[l], w_o[l])
        x = layer_norm(DEEPNORM_ALPHA * x + h, ln1_g[l], ln1_b[l])
        m = moe_ffn(x.reshape(bsz * s, d), w_router[l], b_router[l], w_gu[l], b_gu[l],
                    w_down[l], b_down[l]).reshape(bsz, s, d)
        x = layer_norm(DEEPNORM_ALPHA * x + m, ln2_g[l], ln2_b[l])
    return x
```

```python
import functools

import jax
import jax.numpy as jnp
from jax import lax
from jax.experimental import pallas as pl
from jax.experimental.pallas import tpu as pltpu

D_MODEL = 1024
DEPTH = 4
N_COLS = 9
CONV_A_WIDTH = 3
POOL_WINDOWS = (2, 4, 8, 16)
POOL_GROUP_DIM = D_MODEL // len(POOL_WINDOWS)
CONV_C_WIDTH = 31
N_EXPERTS = 32
TOP_K = 4
D_FF = D_MODEL
SWIGLU_LIMIT = 7.0
SWIGLU_ALPHA = 1.702
LN_EPS = 1e-5
DEEPNORM_ALPHA = (2.0 * DEPTH) ** 0.25

V7X_LANES = 128
V7X_SUBLANES = 8
V7X_VMEM_LIMIT_BYTES = 56 * 1024 * 1024

TM_MIX = 256
HALO_A = 8
HALO_P = 16
HALO_C = 32
CONV_LANE_CHUNK = 128
TR_ROUTE = 512
E_PAD = V7X_LANES
BM_EXPERT = 256
TM_COMB = 256
DMA_CHUNK = 2048
NEG_BIG = -3.0e38

F32 = jnp.float32
BF16 = jnp.bfloat16


def _layer_norm(x, g, b):
    mu = jnp.mean(x, axis=-1, keepdims=True)
    xc = x - mu
    var = jnp.mean(xc * xc, axis=-1, keepdims=True)
    return xc * lax.rsqrt(var + LN_EPS) * g + b


def _store_token_major(ref, val):
    rows = val.shape[0]
    for g in range(D_MODEL // V7X_LANES):
        ref[pl.ds(g, rows, stride=V7X_SUBLANES), :] = val[:, g * V7X_LANES:(g + 1) * V7X_LANES]


def _load_token_major(ref, rows):
    parts = [ref[pl.ds(g, rows, stride=V7X_SUBLANES), :] for g in range(D_MODEL // V7X_LANES)]
    return jnp.concatenate(parts, axis=1)


def _causal_taps(ext_ref, halo, rows, lane0, lanes, weights):
    e = ext_ref[:, lane0:lane0 + lanes]
    realigned = {}
    acc = None
    for j, w in enumerate(weights):
        a, b = divmod(j, V7X_SUBLANES)
        if b == 0:
            start = halo - V7X_SUBLANES * a
            term = e[start:start + rows]
        else:
            if b not in realigned:
                realigned[b] = e[V7X_SUBLANES - b:halo + rows - b]
            start = halo - V7X_SUBLANES * (a + 1)
            term = realigned[b][start:start + rows]
        if w is not None:
            term = term * w
        acc = term if acc is None else acc + term
    return acc


def _mixer_kernel(x_ref, w_in_ref, b_in_ref, conv_a_ref, w_out_a_ref, w_pool_ref, scale_pool_ref,
                  conv_c_ref, conv_c_b_ref, ln_c_g_ref, ln_c_b_ref, w_out_c_ref, b_out_c_ref,
                  w_o_ref, ln1_g_ref, ln1_b_ref,
                  y_ref, y_tm_ref,
                  ext_a, ext_p, ext_c, *, tiles_per_seq):
    tm = x_ref.shape[0]
    tile_in_seq = pl.program_id(0) % tiles_per_seq

    @pl.when(tile_in_seq == 0)
    def _():
        ext_a[0:HALO_A, :] = jnp.zeros((HALO_A, D_MODEL), F32)
        ext_p[0:HALO_P, :] = jnp.zeros((HALO_P, D_MODEL), F32)
        ext_c[0:HALO_C, :] = jnp.zeros((HALO_C, D_MODEL), F32)

    x = x_ref[...]
    xb = x.astype(BF16)

    def proj(col):
        lo, hi = col * D_MODEL, (col + 1) * D_MODEL
        return jnp.dot(xb, w_in_ref[:, lo:hi], preferred_element_type=F32) + b_in_ref[:, lo:hi]

    def gate(col):
        return jax.nn.sigmoid(proj(col))

    ext_a[HALO_A:HALO_A + tm, :] = proj(1) * proj(2)
    parts = []
    for c0 in range(0, D_MODEL, CONV_LANE_CHUNK):
        w = [conv_a_ref[CONV_A_WIDTH - 1 - j:CONV_A_WIDTH - j, c0:c0 + CONV_LANE_CHUNK]
             for j in range(CONV_A_WIDTH)]
        parts.append(_causal_taps(ext_a, HALO_A, tm, c0, CONV_LANE_CHUNK, w))
    u_a = jnp.concatenate(parts, axis=1)
    ext_a[0:HALO_A, :] = ext_a[tm:tm + HALO_A, :]
    y_a = jnp.dot((proj(0) * u_a).astype(BF16), w_out_a_ref[...], preferred_element_type=F32)
    merged = gate(6) * y_a

    p_in = proj(3)
    ext_p[HALO_P:HALO_P + tm, :] = p_in
    pos = tile_in_seq * tm + lax.broadcasted_iota(jnp.int32, (tm, POOL_GROUP_DIM), 0)
    parts = []
    for g, win in enumerate(POOL_WINDOWS):
        lo = g * POOL_GROUP_DIM
        sub = []
        for c0 in range(lo, lo + POOL_GROUP_DIM, CONV_LANE_CHUNK):
            sub.append(_causal_taps(ext_p, HALO_P, tm, c0, CONV_LANE_CHUNK, [None] * win))
        wsum = jnp.concatenate(sub, axis=1)
        cnt = jnp.minimum(pos + 1, win).astype(F32)
        pooled = wsum / cnt - p_in[:, lo:lo + POOL_GROUP_DIM]
        parts.append(jnp.dot(pooled.astype(BF16), w_pool_ref[g], preferred_element_type=F32))
    ext_p[0:HALO_P, :] = ext_p[tm:tm + HALO_P, :]
    y_b = jnp.concatenate(parts, axis=1) * scale_pool_ref[...]
    merged = merged + gate(7) * y_b

    ext_c[HALO_C:HALO_C + tm, :] = proj(4) * gate(5)
    parts = []
    for c0 in range(0, D_MODEL, CONV_LANE_CHUNK):
        w = [conv_c_ref[CONV_C_WIDTH - 1 - j:CONV_C_WIDTH - j, c0:c0 + CONV_LANE_CHUNK]
             for j in range(CONV_C_WIDTH)]
        parts.append(_causal_taps(ext_c, HALO_C, tm, c0, CONV_LANE_CHUNK, w))
    v = jnp.concatenate(parts, axis=1) + conv_c_b_ref[...]
    ext_c[0:HALO_C, :] = ext_c[tm:tm + HALO_C, :]
    v = _layer_norm(v, ln_c_g_ref[...], ln_c_b_ref[...])
    v = v * jax.nn.sigmoid(v)
    y_c = jnp.dot(v.astype(BF16), w_out_c_ref[...], preferred_element_type=F32) + b_out_c_ref[...]
    merged = merged + gate(8) * y_c

    h = jnp.dot(merged.astype(BF16), w_o_ref[...], preferred_element_type=F32)
    y = _layer_norm(DEEPNORM_ALPHA * x + h, ln1_g_ref[...], ln1_b_ref[...])
    y_ref[...] = y
    _store_token_major(y_tm_ref, y)


def _resident(shape):
    zeros = (0,) * len(shape)
    return pl.BlockSpec(shape, lambda i: zeros, pipeline_mode=pl.Buffered(1))


def _mixer(x2, w_in, b_in, conv_a, w_out_a, w_pool, scale_pool, conv_c, conv_c_b,
           ln_c_g, ln_c_b, w_out_c, b_out_c, w_o, ln1_g, ln1_b, *, seq_len):
    t_tok = x2.shape[0]
    tm = TM_MIX
    assert seq_len % tm == 0 and t_tok % seq_len == 0
    consts = (w_in, b_in, conv_a, w_out_a, w_pool, scale_pool, conv_c, conv_c_b,
              ln_c_g, ln_c_b, w_out_c, b_out_c, w_o, ln1_g, ln1_b)
    return pl.pallas_call(
        functools.partial(_mixer_kernel, tiles_per_seq=seq_len // tm),
        out_shape=(jax.ShapeDtypeStruct((t_tok, D_MODEL), F32),
                   jax.ShapeDtypeStruct((t_tok * V7X_SUBLANES, V7X_LANES), F32)),
        grid=(t_tok // tm,),
        in_specs=[pl.BlockSpec((tm, D_MODEL), lambda i: (i, 0))] + [_resident(c.shape) for c in consts],
        out_specs=(pl.BlockSpec((tm, D_MODEL), lambda i: (i, 0)),
                   pl.BlockSpec((tm * V7X_SUBLANES, V7X_LANES), lambda i: (i, 0))),
        scratch_shapes=[pltpu.VMEM((HALO_A + tm, D_MODEL), F32),
                        pltpu.VMEM((HALO_P + tm, D_MODEL), F32),
                        pltpu.VMEM((HALO_C + tm, D_MODEL), F32)],
        compiler_params=pltpu.CompilerParams(dimension_semantics=("arbitrary",),
                                             vmem_limit_bytes=V7X_VMEM_LIMIT_BYTES),
        name="mixer",
    )(x2, *consts)


def _router_kernel(x_ref, wr_ref, br_ref, idx_ref, gate_ref, rank_ref, counts_ref, run_ref):
    tr = x_ref.shape[0]

    @pl.when(pl.program_id(0) == 0)
    def _():
        run_ref[...] = jnp.zeros(run_ref.shape, F32)

    logits = jnp.dot(x_ref[...].astype(BF16), wr_ref[...], preferred_element_type=F32) + br_ref[...]
    lane = lax.broadcasted_iota(jnp.int32, (tr, E_PAD), 1)
    work = logits
    vals, idxs = [], []
    for _ in range(TOP_K):
        m = jnp.max(work, axis=-1, keepdims=True)
        idx = jnp.min(jnp.where(work == m, lane, E_PAD), axis=-1, keepdims=True)
        vals.append(m)
        idxs.append(idx)
        work = jnp.where(lane == idx, NEG_BIG, work)
    exps = [jnp.exp(v - vals[0]) for v in vals]
    denom = exps[0] + exps[1] + exps[2] + exps[3]

    sel = jnp.zeros((tr, E_PAD), F32)
    for k in range(TOP_K):
        sel = sel + jnp.where(lane == idxs[k], 1.0, 0.0)
    r_i = lax.broadcasted_iota(jnp.int32, (tr, tr), 0)
    c_i = lax.broadcasted_iota(jnp.int32, (tr, tr), 1)
    before = jnp.where(c_i < r_i, 1.0, 0.0).astype(BF16)
    prior = jnp.dot(before, sel.astype(BF16), preferred_element_type=F32) + run_ref[0:1, :]

    idx_out = jnp.zeros((tr, E_PAD), jnp.int32)
    gate_out = jnp.zeros((tr, E_PAD), F32)
    rank_out = jnp.zeros((tr, E_PAD), F32)
    for k in range(TOP_K):
        rank_k = jnp.sum(jnp.where(lane == idxs[k], prior, 0.0), axis=-1, keepdims=True)
        idx_out = jnp.where(lane == k, idxs[k], idx_out)
        gate_out = jnp.where(lane == k, exps[k] / denom, gate_out)
        rank_out = jnp.where(lane == k, rank_k, rank_out)
    idx_ref[...] = idx_out
    gate_ref[...] = gate_out
    rank_ref[...] = rank_out.astype(jnp.int32)

    run_ref[0:1, :] = run_ref[0:1, :] + jnp.sum(sel, axis=0, keepdims=True)
    counts_ref[...] = run_ref[...]


def _router(x2, wr_pad, br_pad):
    t_tok = x2.shape[0]
    tr = TR_ROUTE
    tok_spec = pl.BlockSpec((tr, E_PAD), lambda i: (i, 0))
    return pl.pallas_call(
        _router_kernel,
        out_shape=(jax.ShapeDtypeStruct((t_tok, E_PAD), jnp.int32),
                   jax.ShapeDtypeStruct((t_tok, E_PAD), F32),
                   jax.ShapeDtypeStruct((t_tok, E_PAD), jnp.int32),
                   jax.ShapeDtypeStruct((V7X_SUBLANES, E_PAD), F32)),
        grid=(t_tok // tr,),
        in_specs=[pl.BlockSpec((tr, D_MODEL), lambda i: (i, 0)),
                  pl.BlockSpec((D_MODEL, E_PAD), lambda i: (0, 0)),
                  pl.BlockSpec((1, E_PAD), lambda i: (0, 0))],
        out_specs=(tok_spec, tok_spec, tok_spec,
                   pl.BlockSpec((V7X_SUBLANES, E_PAD), lambda i: (0, 0))),
        scratch_shapes=[pltpu.VMEM((V7X_SUBLANES, E_PAD), F32)],
        compiler_params=pltpu.CompilerParams(dimension_semantics=("arbitrary",)),
        name="router",
    )(x2, wr_pad, br_pad)


def _row_permute_kernel(src_idx_ref, dst_idx_ref, src_hbm, dst_init_hbm, dst_hbm, sem):
    del dst_init_hbm
    n = src_idx_ref.shape[-1]

    def row_copy(s, d):
        return pltpu.make_async_copy(src_hbm.at[s], dst_hbm.at[d], sem)

    def issue(j, carry):
        row_copy(src_idx_ref[0, 0, j], dst_idx_ref[0, 0, j]).start()
        return carry

    def drain(j, carry):
        row_copy(0, 0).wait()
        return carry

    lax.fori_loop(0, n, issue, 0)
    lax.fori_loop(0, n, drain, 0)


def _row_permute(src3, src_idx, dst_idx, dst_init3):
    n = src_idx.shape[0]
    assert n % DMA_CHUNK == 0
    idx_spec = pl.BlockSpec((1, 1, DMA_CHUNK), lambda i: (i, 0, 0), memory_space=pltpu.SMEM)
    any_spec = pl.BlockSpec(memory_space=pl.ANY)
    return pl.pallas_call(
        _row_permute_kernel,
        out_shape=jax.ShapeDtypeStruct(dst_init3.shape, dst_init3.dtype),
        grid=(n // DMA_CHUNK,),
        in_specs=[idx_spec, idx_spec, any_spec, any_spec],
        out_specs=any_spec,
        scratch_shapes=[pltpu.SemaphoreType.DMA(())],
        input_output_aliases={3: 0},
        compiler_params=pltpu.CompilerParams(dimension_semantics=("arbitrary",),
                                             has_side_effects=True),
        name="row_permute",
    )(src_idx.reshape(n // DMA_CHUNK, 1, DMA_CHUNK), dst_idx.reshape(n // DMA_CHUNK, 1, DMA_CHUNK),
      src3, dst_init3)


def _expert_kernel(block_e_ref, nb_used_ref, xs_ref, w_gu_ref, b_gu_ref, w_down_ref, b_down_ref,
                   ys_ref):
    del block_e_ref
    bm = BM_EXPERT
    b = pl.program_id(0)

    @pl.when(b < nb_used_ref[0])
    def _():
        xb = _load_token_major(xs_ref, bm).astype(BF16)
        h = jnp.dot(xb, w_gu_ref[0], preferred_element_type=F32) + b_gu_ref[0]
        gate = jnp.minimum(h[:, :D_FF], SWIGLU_LIMIT)
        up = jnp.clip(h[:, D_FF:], -SWIGLU_LIMIT, SWIGLU_LIMIT)
        act = (up + 1.0) * (gate * jax.nn.sigmoid(SWIGLU_ALPHA * gate))
        y = jnp.dot(act.astype(BF16), w_down_ref[0], preferred_element_type=F32) + b_down_ref[0]
        _store_token_major(ys_ref, y)

    @pl.when(b >= nb_used_ref[0])
    def _():
        ys_ref[...] = jnp.zeros(ys_ref.shape, F32)


def _experts(block_e, nb_used, xs_tm, w_gu, b_gu, w_down, b_down):
    bm = BM_EXPERT
    n_blocks = xs_tm.shape[0] // (bm * V7X_SUBLANES)

    def x_map(b, be, nb):
        return (jnp.minimum(b, nb[0] - 1), 0)

    def e_map(b, be, nb):
        return (be[b], 0, 0)

    grid_spec = pltpu.PrefetchScalarGridSpec(
        num_scalar_prefetch=2,
        grid=(n_blocks,),
        in_specs=[pl.BlockSpec((bm * V7X_SUBLANES, V7X_LANES), x_map),
                  pl.BlockSpec((1, D_MODEL, 2 * D_FF), e_map),
                  pl.BlockSpec((1, 1, 2 * D_FF), e_map),
                  pl.BlockSpec((1, D_FF, D_MODEL), e_map),
                  pl.BlockSpec((1, 1, D_MODEL), e_map)],
        out_specs=pl.BlockSpec((bm * V7X_SUBLANES, V7X_LANES), lambda b, be, nb: (b, 0)),
    )
    return pl.pallas_call(
        _expert_kernel,
        out_shape=jax.ShapeDtypeStruct(xs_tm.shape, F32),
        grid_spec=grid_spec,
        compiler_params=pltpu.CompilerParams(dimension_semantics=("arbitrary",),
                                             vmem_limit_bytes=V7X_VMEM_LIMIT_BYTES),
        name="experts",
    )(block_e, nb_used, xs_tm, w_gu, b_gu, w_down, b_down)


def _combine_kernel(x_ref, yk_ref, gate_ref, g_ref, b_ref, o_ref):
    tm = x_ref.shape[0]
    gates = gate_ref[...]
    m = jnp.zeros((tm, D_MODEL), F32)
    for k in range(TOP_K):
        m = m + gates[:, k:k + 1] * _load_token_major(yk_ref.at[k], tm)
    o_ref[...] = _layer_norm(DEEPNORM_ALPHA * x_ref[...] + m, g_ref[...], b_ref[...])


def _combine(x2, yk4, gates_pad, ln_g, ln_b):
    t_tok = x2.shape[0]
    tm = TM_COMB
    return pl.pallas_call(
        _combine_kernel,
        out_shape=jax.ShapeDtypeStruct((t_tok, D_MODEL), F32),
        grid=(t_tok // tm,),
        in_specs=[pl.BlockSpec((tm, D_MODEL), lambda i: (i, 0)),
                  pl.BlockSpec((TOP_K, tm * V7X_SUBLANES, V7X_LANES), lambda i: (0, i, 0)),
                  pl.BlockSpec((tm, E_PAD), lambda i: (i, 0)),
                  pl.BlockSpec((1, D_MODEL), lambda i: (0, 0)),
                  pl.BlockSpec((1, D_MODEL), lambda i: (0, 0))],
        out_specs=pl.BlockSpec((tm, D_MODEL), lambda i: (i, 0)),
        compiler_params=pltpu.CompilerParams(dimension_semantics=("arbitrary",)),
        name="combine",
    )(x2, yk4, gates_pad, ln_g, ln_b)


def _moe(x1, x1_tm, wr_pad, br_pad, w_gu, b_gu, w_down, b_down, ln_g, ln_b):
    t_tok = x1.shape[0]
    n_assign = t_tok * TOP_K
    bm = BM_EXPERT
    n_blocks = n_assign // bm + N_EXPERTS
    n_slots = n_blocks * bm

    idx_pad, gates_pad, rank_pad, counts_f = _router(x1, wr_pad, br_pad)
    top_idx = idx_pad[:, :TOP_K]
    rank = rank_pad[:, :TOP_K]

    counts = counts_f[0, :N_EXPERTS].astype(jnp.int32)
    blocks_per_e = (counts + bm - 1) // bm
    blk_end = jnp.cumsum(blocks_per_e)
    row_start = (blk_end - blocks_per_e) * bm
    dest = jnp.take(row_start, top_idx) + rank
    block_e = jnp.minimum(jnp.searchsorted(blk_end, jnp.arange(n_blocks), side="right"),
                          N_EXPERTS - 1).astype(jnp.int32)
    nb_used = blk_end[-1:].astype(jnp.int32)

    tile = (V7X_SUBLANES, V7X_LANES)
    tok_of_assign = jnp.arange(n_assign, dtype=jnp.int32) // TOP_K
    xs3 = _row_permute(x1_tm.reshape(t_tok, *tile), tok_of_assign, dest.reshape(n_assign),
                       jnp.zeros((n_slots, *tile), F32))
    ys_tm = _experts(block_e, nb_used, xs3.reshape(n_slots * V7X_SUBLANES, V7X_LANES),
                     w_gu, b_gu, w_down, b_down)
    k_major = (jnp.arange(TOP_K, dtype=jnp.int32)[None, :] * t_tok
               + jnp.arange(t_tok, dtype=jnp.int32)[:, None])
    yk3 = _row_permute(ys_tm.reshape(n_slots, *tile), dest.reshape(n_assign),
                       k_major.reshape(n_assign), jnp.zeros((n_assign, *tile), F32))
    yk4 = yk3.reshape(TOP_K, t_tok * V7X_SUBLANES, V7X_LANES)
    return _combine(x1, yk4, gates_pad, ln_g, ln_b)


def kernel(x, w_in, b_in, conv_a, w_out_a, w_pool, scale_pool, conv_c, conv_c_b, ln_c_g, ln_c_b,
           w_out_c, b_out_c, w_o, ln1_g, ln1_b, w_router, b_router, w_gu, b_gu, w_down, b_down,
           ln2_g, ln2_b):
    bsz, seq_len, d = x.shape
    assert d == D_MODEL
    t_tok = bsz * seq_len
    depth = w_in.shape[0]

    def row(v):
        return v[:, None, :]

    w_in_b = w_in.astype(BF16)
    w_out_a_b = w_out_a.astype(BF16)
    w_pool_b = w_pool.astype(BF16)
    w_out_c_b = w_out_c.astype(BF16)
    w_o_b = w_o.astype(BF16)
    wr_pad = jnp.pad(w_router, ((0, 0), (0, 0), (0, E_PAD - N_EXPERTS))).astype(BF16)
    br_pad = jnp.pad(b_router, ((0, 0), (0, E_PAD - N_EXPERTS)), constant_values=NEG_BIG)
    w_gu_b = jnp.concatenate([w_gu[..., 0::2], w_gu[..., 1::2]], axis=-1).astype(BF16)
    b_gu_d = jnp.concatenate([b_gu[..., 0::2], b_gu[..., 1::2]], axis=-1)[:, :, None, :]
    w_down_b = w_down.astype(BF16)
    b_down_d = b_down[:, :, None, :]

    x2 = x.reshape(t_tok, d)
    for l in range(depth):
        x1, x1_tm = _mixer(x2, w_in_b[l], row(b_in)[l], conv_a[l], w_out_a_b[l], w_pool_b[l],
                           row(scale_pool)[l], conv_c[l], row(conv_c_b)[l], row(ln_c_g)[l],
                           row(ln_c_b)[l], w_out_c_b[l], row(b_out_c)[l], w_o_b[l],
                           row(ln1_g)[l], row(ln1_b)[l], seq_len=seq_len)
        x2 = _moe(x1, x1_tm, wr_pad[l], row(br_pad)[l], w_gu_b[l], b_gu_d[l], w_down_b[l],
                  b_down_d[l], row(ln2_g)[l], row(ln2_b)[l])
    return x2.reshape(bsz, seq_len, d)
```

```python
import functools

import jax
import jax.numpy as jnp
from jax import lax
from jax.experimental import pallas as pl
from jax.experimental.pallas import tpu as pltpu
from jax.experimental.pallas import tpu_sc as plsc

D_MODEL = 1024
DEPTH = 4
N_COLS = 9
CONV_A_WIDTH = 3
POOL_WINDOWS = (2, 4, 8, 16)
POOL_GROUP_DIM = D_MODEL // len(POOL_WINDOWS)
CONV_C_WIDTH = 31
N_EXPERTS = 32
TOP_K = 4
D_FF = D_MODEL
SWIGLU_LIMIT = 7.0
SWIGLU_ALPHA = 1.702
LN_EPS = 1e-5
DEEPNORM_ALPHA = (2.0 * DEPTH) ** 0.25

V7X_LANES = 128
V7X_SUBLANES = 8
V7X_VMEM_LIMIT_BYTES = 56 * 1024 * 1024
V7X_MXU_DIM = 256
V7X_SC_CORES = 2
V7X_SC_SUBCORES = 16
V7X_SC_WORKERS = V7X_SC_CORES * V7X_SC_SUBCORES

TM_MIX = 256
HALO_A = 8
HALO_P = 16
HALO_C = 32
CONV_LANE_CHUNK = 128
TR_ROUTE = 512
E_PAD = V7X_LANES
BM_EXPERT = 256
TM_COMB = 256
DEINTERLEAVE_COLS = V7X_MXU_DIM
SC_ROWS = 32
NEG_BIG = -3.0e38

F32 = jnp.float32
BF16 = jnp.bfloat16


def _layer_norm(x, g, b):
    mu = jnp.mean(x, axis=-1, keepdims=True)
    xc = x - mu
    var = jnp.mean(xc * xc, axis=-1, keepdims=True)
    return xc * lax.rsqrt(var + LN_EPS) * g + b


def _store_token_major(ref, val):
    rows = val.shape[0]
    for g in range(D_MODEL // V7X_LANES):
        ref[pl.ds(g, rows, stride=V7X_SUBLANES), :] = val[:, g * V7X_LANES:(g + 1) * V7X_LANES]


def _load_token_major(ref, rows):
    parts = [ref[pl.ds(g, rows, stride=V7X_SUBLANES), :] for g in range(D_MODEL // V7X_LANES)]
    return jnp.concatenate(parts, axis=1)


def _realign(ext_ref, shift_ref, halo, rows, lane0, lanes, max_shift):
    n = halo + rows - V7X_SUBLANES
    for b in range(1, min(max_shift, V7X_SUBLANES - 1) + 1):
        shift_ref[b - 1, 0:n, lane0:lane0 + lanes] = (
            ext_ref[V7X_SUBLANES - b:V7X_SUBLANES - b + n, lane0:lane0 + lanes])


def _causal_taps(ext_ref, shift_ref, halo, rows, lane0, lanes, weights):
    acc = None
    for j, w in enumerate(weights):
        a, b = divmod(j, V7X_SUBLANES)
        if b == 0:
            start = halo - V7X_SUBLANES * a
            term = ext_ref[start:start + rows, lane0:lane0 + lanes]
        else:
            start = halo - V7X_SUBLANES * (a + 1)
            term = shift_ref[b - 1, start:start + rows, lane0:lane0 + lanes]
        if w is not None:
            term = term * w
        acc = term if acc is None else acc + term
    return acc


def _mixer_kernel(x_ref, w_in_ref, b_in_ref, conv_a_ref, w_out_a_ref, w_pool_ref, scale_pool_ref,
                  conv_c_ref, conv_c_b_ref, ln_c_g_ref, ln_c_b_ref, w_out_c_ref, b_out_c_ref,
                  w_o_ref, ln1_g_ref, ln1_b_ref,
                  y_ref, y_tm_ref,
                  ext_a, ext_p, ext_c, shift_ref, *, tiles_per_seq):
    tm = x_ref.shape[0]
    tile_in_seq = pl.program_id(0) % tiles_per_seq

    @pl.when(tile_in_seq == 0)
    def _():
        ext_a[0:HALO_A, :] = jnp.zeros((HALO_A, D_MODEL), F32)
        ext_p[0:HALO_P, :] = jnp.zeros((HALO_P, D_MODEL), F32)
        ext_c[0:HALO_C, :] = jnp.zeros((HALO_C, D_MODEL), F32)

    x = x_ref[...]
    xb = x.astype(BF16)

    def proj(col):
        lo, hi = col * D_MODEL, (col + 1) * D_MODEL
        return jnp.dot(xb, w_in_ref[:, lo:hi], preferred_element_type=F32) + b_in_ref[:, lo:hi]

    def gate(col):
        return jax.nn.sigmoid(proj(col))

    ext_a[HALO_A:HALO_A + tm, :] = proj(1) * proj(2)
    parts = []
    for c0 in range(0, D_MODEL, CONV_LANE_CHUNK):
        w = [conv_a_ref[CONV_A_WIDTH - 1 - j:CONV_A_WIDTH - j, c0:c0 + CONV_LANE_CHUNK]
             for j in range(CONV_A_WIDTH)]
        _realign(ext_a, shift_ref, HALO_A, tm, c0, CONV_LANE_CHUNK, CONV_A_WIDTH - 1)
        parts.append(_causal_taps(ext_a, shift_ref, HALO_A, tm, c0, CONV_LANE_CHUNK, w))
    u_a = jnp.concatenate(parts, axis=1)
    ext_a[0:HALO_A, :] = ext_a[tm:tm + HALO_A, :]
    y_a = jnp.dot((proj(0) * u_a).astype(BF16), w_out_a_ref[...], preferred_element_type=F32)
    merged = gate(6) * y_a

    p_in = proj(3)
    ext_p[HALO_P:HALO_P + tm, :] = p_in
    pos = tile_in_seq * tm + lax.broadcasted_iota(jnp.int32, (tm, POOL_GROUP_DIM), 0)
    parts = []
    for g, win in enumerate(POOL_WINDOWS):
        lo = g * POOL_GROUP_DIM
        sub = []
        for c0 in range(lo, lo + POOL_GROUP_DIM, CONV_LANE_CHUNK):
            _realign(ext_p, shift_ref, HALO_P, tm, c0, CONV_LANE_CHUNK, win - 1)
            sub.append(_causal_taps(ext_p, shift_ref, HALO_P, tm, c0, CONV_LANE_CHUNK,
                                    [None] * win))
        wsum = jnp.concatenate(sub, axis=1)
        cnt = jnp.minimum(pos + 1, win).astype(F32)
        pooled = wsum / cnt - p_in[:, lo:lo + POOL_GROUP_DIM]
        parts.append(jnp.dot(pooled.astype(BF16), w_pool_ref[g], preferred_element_type=F32))
    ext_p[0:HALO_P, :] = ext_p[tm:tm + HALO_P, :]
    y_b = jnp.concatenate(parts, axis=1) * scale_pool_ref[...]
    merged = merged + gate(7) * y_b

    ext_c[HALO_C:HALO_C + tm, :] = proj(4) * gate(5)
    parts = []
    for c0 in range(0, D_MODEL, CONV_LANE_CHUNK):
        w = [conv_c_ref[CONV_C_WIDTH - 1 - j:CONV_C_WIDTH - j, c0:c0 + CONV_LANE_CHUNK]
             for j in range(CONV_C_WIDTH)]
        _realign(ext_c, shift_ref, HALO_C, tm, c0, CONV_LANE_CHUNK, CONV_C_WIDTH - 1)
        parts.append(_causal_taps(ext_c, shift_ref, HALO_C, tm, c0, CONV_LANE_CHUNK, w))
    v = jnp.concatenate(parts, axis=1) + conv_c_b_ref[...]
    ext_c[0:HALO_C, :] = ext_c[tm:tm + HALO_C, :]
    v = _layer_norm(v, ln_c_g_ref[...], ln_c_b_ref[...])
    v = v * jax.nn.sigmoid(v)
    y_c = jnp.dot(v.astype(BF16), w_out_c_ref[...], preferred_element_type=F32) + b_out_c_ref[...]
    merged = merged + gate(8) * y_c

    h = jnp.dot(merged.astype(BF16), w_o_ref[...], preferred_element_type=F32)
    y = _layer_norm(DEEPNORM_ALPHA * x + h, ln1_g_ref[...], ln1_b_ref[...])
    y_ref[...] = y
    _store_token_major(y_tm_ref, y)


def _resident(shape):
    zeros = (0,) * len(shape)
    return pl.BlockSpec(shape, lambda i: zeros, pipeline_mode=pl.Buffered(1))


def _mixer(x2, w_in, b_in, conv_a, w_out_a, w_pool, scale_pool, conv_c, conv_c_b,
           ln_c_g, ln_c_b, w_out_c, b_out_c, w_o, ln1_g, ln1_b, *, seq_len):
    t_tok = x2.shape[0]
    tm = TM_MIX
    assert seq_len % tm == 0 and t_tok % seq_len == 0
    consts = (w_in, b_in, conv_a, w_out_a, w_pool, scale_pool, conv_c, conv_c_b,
              ln_c_g, ln_c_b, w_out_c, b_out_c, w_o, ln1_g, ln1_b)
    return pl.pallas_call(
        functools.partial(_mixer_kernel, tiles_per_seq=seq_len // tm),
        out_shape=(jax.ShapeDtypeStruct((t_tok, D_MODEL), F32),
                   jax.ShapeDtypeStruct((t_tok * V7X_SUBLANES, V7X_LANES), F32)),
        grid=(t_tok // tm,),
        in_specs=[pl.BlockSpec((tm, D_MODEL), lambda i: (i, 0))] + [_resident(c.shape) for c in consts],
        out_specs=(pl.BlockSpec((tm, D_MODEL), lambda i: (i, 0)),
                   pl.BlockSpec((tm * V7X_SUBLANES, V7X_LANES), lambda i: (i, 0))),
        scratch_shapes=[pltpu.VMEM((HALO_A + tm, D_MODEL), F32),
                        pltpu.VMEM((HALO_P + tm, D_MODEL), F32),
                        pltpu.VMEM((HALO_C + tm, D_MODEL), F32),
                        pltpu.VMEM((V7X_SUBLANES - 1, HALO_C + tm - V7X_SUBLANES, D_MODEL), F32)],
        compiler_params=pltpu.CompilerParams(dimension_semantics=("arbitrary",),
                                             vmem_limit_bytes=V7X_VMEM_LIMIT_BYTES),
        name="mixer",
    )(x2, *consts)


def _router_kernel(x_ref, wr_ref, br_ref, idx_ref, gate_ref, rank_ref, counts_ref, run_ref):
    tr = x_ref.shape[0]

    @pl.when(pl.program_id(0) == 0)
    def _():
        run_ref[...] = jnp.zeros(run_ref.shape, F32)

    logits = jnp.dot(x_ref[...].astype(BF16), wr_ref[...], preferred_element_type=F32) + br_ref[...]
    lane = lax.broadcasted_iota(jnp.int32, (tr, E_PAD), 1)
    work = logits
    vals, idxs = [], []
    for _ in range(TOP_K):
        m = jnp.max(work, axis=-1, keepdims=True)
        idx = jnp.min(jnp.where(work == m, lane, E_PAD), axis=-1, keepdims=True)
        vals.append(m)
        idxs.append(idx)
        work = jnp.where(lane == idx, NEG_BIG, work)
    exps = [jnp.exp(v - vals[0]) for v in vals]
    denom = exps[0] + exps[1] + exps[2] + exps[3]

    sel = jnp.zeros((tr, E_PAD), F32)
    for k in range(TOP_K):
        sel = sel + jnp.where(lane == idxs[k], 1.0, 0.0)
    r_i = lax.broadcasted_iota(jnp.int32, (tr, tr), 0)
    c_i = lax.broadcasted_iota(jnp.int32, (tr, tr), 1)
    before = jnp.where(c_i < r_i, 1.0, 0.0).astype(BF16)
    prior = jnp.dot(before, sel.astype(BF16), preferred_element_type=F32) + run_ref[0:1, :]

    idx_out = jnp.zeros((tr, E_PAD), jnp.int32)
    gate_out = jnp.zeros((tr, E_PAD), F32)
    rank_out = jnp.zeros((tr, E_PAD), F32)
    for k in range(TOP_K):
        rank_k = jnp.sum(jnp.where(lane == idxs[k], prior, 0.0), axis=-1, keepdims=True)
        idx_out = jnp.where(lane == k, idxs[k], idx_out)
        gate_out = jnp.where(lane == k, exps[k] / denom, gate_out)
        rank_out = jnp.where(lane == k, rank_k, rank_out)
    idx_ref[...] = idx_out
    gate_ref[...] = gate_out
    rank_ref[...] = rank_out.astype(jnp.int32)

    run_ref[0:1, :] = run_ref[0:1, :] + jnp.sum(sel, axis=0, keepdims=True)
    counts_ref[...] = run_ref[...]


def _router(x2, wr_pad, br_pad):
    t_tok = x2.shape[0]
    tr = TR_ROUTE
    tok_spec = pl.BlockSpec((tr, E_PAD), lambda i: (i, 0))
    return pl.pallas_call(
        _router_kernel,
        out_shape=(jax.ShapeDtypeStruct((t_tok, E_PAD), jnp.int32),
                   jax.ShapeDtypeStruct((t_tok, E_PAD), F32),
                   jax.ShapeDtypeStruct((t_tok, E_PAD), jnp.int32),
                   jax.ShapeDtypeStruct((V7X_SUBLANES, E_PAD), F32)),
        grid=(t_tok // tr,),
        in_specs=[pl.BlockSpec((tr, D_MODEL), lambda i: (i, 0)),
                  pl.BlockSpec((D_MODEL, E_PAD), lambda i: (0, 0)),
                  pl.BlockSpec((1, E_PAD), lambda i: (0, 0))],
        out_specs=(tok_spec, tok_spec, tok_spec,
                   pl.BlockSpec((V7X_SUBLANES, E_PAD), lambda i: (0, 0))),
        scratch_shapes=[pltpu.VMEM((V7X_SUBLANES, E_PAD), F32)],
        compiler_params=pltpu.CompilerParams(dimension_semantics=("arbitrary",)),
        name="router",
    )(x2, wr_pad, br_pad)


def _sc_mesh():
    return plsc.VectorSubcoreMesh(core_axis_name="c", subcore_axis_name="s")


def _sc_worker_id():
    return lax.axis_index("s") * V7X_SC_CORES + lax.axis_index("c")


def _dispatch_rows(x3, dest_km, n_slots):
    t_tok = x3.shape[0]
    tok_per_worker = t_tok // V7X_SC_WORKERS
    assert tok_per_worker % SC_ROWS == 0

    def body(x_hbm, dest_hbm, out_hbm, rows_v, idx_v, sem):
        base = _sc_worker_id() * tok_per_worker

        @pl.loop(0, tok_per_worker // SC_ROWS)
        def _(step):
            t0 = pl.multiple_of(base + step * SC_ROWS, SC_ROWS)
            pltpu.sync_copy(x_hbm.at[pl.ds(t0, SC_ROWS)], rows_v)
            for k in range(TOP_K):
                pltpu.sync_copy(dest_hbm.at[k, pl.ds(t0, SC_ROWS)], idx_v.at[k])
            for k in range(TOP_K):
                pltpu.async_copy(rows_v, out_hbm.at[idx_v.at[k]], sem).wait()

    return pl.kernel(
        body, mesh=_sc_mesh(),
        out_type=jax.ShapeDtypeStruct((n_slots,) + x3.shape[1:], x3.dtype),
        scratch_types=[pltpu.VMEM((SC_ROWS,) + x3.shape[1:], x3.dtype),
                       pltpu.VMEM((TOP_K, SC_ROWS), jnp.int32),
                       pltpu.SemaphoreType.DMA],
    )(x3, dest_km)


def _gather_rows(ys3, src_rows):
    n = src_rows.shape[0]
    rows_per_worker = n // V7X_SC_WORKERS
    assert rows_per_worker % SC_ROWS == 0

    def body(ys_hbm, src_hbm, out_hbm, rows_v, idx_v, sem):
        base = _sc_worker_id() * rows_per_worker

        @pl.loop(0, rows_per_worker // SC_ROWS)
        def _(step):
            a0 = pl.multiple_of(base + step * SC_ROWS, SC_ROWS)
            pltpu.sync_copy(src_hbm.at[pl.ds(a0, SC_ROWS)], idx_v)
            pltpu.async_copy(ys_hbm.at[idx_v], rows_v, sem).wait()
            pltpu.sync_copy(rows_v, out_hbm.at[pl.ds(a0, SC_ROWS)])

    return pl.kernel(
        body, mesh=_sc_mesh(),
        out_type=jax.ShapeDtypeStruct((n,) + ys3.shape[1:], ys3.dtype),
        scratch_types=[pltpu.VMEM((SC_ROWS,) + ys3.shape[1:], ys3.dtype),
                       pltpu.VMEM((SC_ROWS,), jnp.int32),
                       pltpu.SemaphoreType.DMA],
    )(ys3, src_rows)


def _expert_kernel(block_e_ref, nb_used_ref, xs_ref, w_gu_ref, b_gu_ref, w_down_ref, b_down_ref,
                   ys_ref, w_gu_b, w_down_b):
    bm = BM_EXPERT
    b = pl.program_id(0)
    new_expert = jnp.logical_or(b == 0, block_e_ref[b] != block_e_ref[jnp.maximum(b - 1, 0)])

    @pl.when(jnp.logical_and(new_expert, b < nb_used_ref[0]))
    def _():
        cb = DEINTERLEAVE_COLS
        src = lax.broadcasted_iota(jnp.int32, (cb, cb), 0)
        dst = lax.broadcasted_iota(jnp.int32, (cb, cb), 1)
        pick = jnp.where(dst < cb // 2, 2 * dst, 2 * (dst - cb // 2) + 1)
        sel = jnp.where(src == pick, 1.0, 0.0).astype(BF16)
        for c in range(2 * D_FF // cb):
            blk = jnp.dot(w_gu_ref[0, :, c * cb:(c + 1) * cb].astype(BF16), sel,
                          preferred_element_type=F32).astype(BF16)
            lo = c * (cb // 2)
            w_gu_b[:, lo:lo + cb // 2] = blk[:, :cb // 2]
            w_gu_b[:, D_FF + lo:D_FF + lo + cb // 2] = blk[:, cb // 2:]
        w_down_b[...] = w_down_ref[0].astype(BF16)

    @pl.when(b < nb_used_ref[0])
    def _():
        xb = _load_token_major(xs_ref, bm).astype(BF16)
        h = jnp.dot(xb, w_gu_b[...], preferred_element_type=F32) + b_gu_ref[0]
        gate = jnp.minimum(h[:, :D_FF], SWIGLU_LIMIT)
        up = jnp.clip(h[:, D_FF:], -SWIGLU_LIMIT, SWIGLU_LIMIT)
        act = (up + 1.0) * (gate * jax.nn.sigmoid(SWIGLU_ALPHA * gate))
        y = jnp.dot(act.astype(BF16), w_down_b[...], preferred_element_type=F32) + b_down_ref[0]
        _store_token_major(ys_ref, y)

    @pl.when(b >= nb_used_ref[0])
    def _():
        ys_ref[...] = jnp.zeros(ys_ref.shape, F32)


def _experts(block_e, nb_used, xs_tm, w_gu, b_gu, w_down, b_down):
    bm = BM_EXPERT
    n_blocks = xs_tm.shape[0] // (bm * V7X_SUBLANES)

    def x_map(b, be, nb):
        return (jnp.minimum(b, nb[0] - 1), 0)

    def e_map(b, be, nb):
        return (be[b], 0, 0)

    grid_spec = pltpu.PrefetchScalarGridSpec(
        num_scalar_prefetch=2,
        grid=(n_blocks,),
        in_specs=[pl.BlockSpec((bm * V7X_SUBLANES, V7X_LANES), x_map),
                  pl.BlockSpec((1, D_MODEL, 2 * D_FF), e_map),
                  pl.BlockSpec((1, 1, 2 * D_FF), e_map),
                  pl.BlockSpec((1, D_FF, D_MODEL), e_map),
                  pl.BlockSpec((1, 1, D_MODEL), e_map)],
        out_specs=pl.BlockSpec((bm * V7X_SUBLANES, V7X_LANES), lambda b, be, nb: (b, 0)),
        scratch_shapes=[pltpu.VMEM((D_MODEL, 2 * D_FF), BF16),
                        pltpu.VMEM((D_FF, D_MODEL), BF16)],
    )
    return pl.pallas_call(
        _expert_kernel,
        out_shape=jax.ShapeDtypeStruct(xs_tm.shape, F32),
        grid_spec=grid_spec,
        compiler_params=pltpu.CompilerParams(dimension_semantics=("arbitrary",),
                                             vmem_limit_bytes=V7X_VMEM_LIMIT_BYTES),
        name="experts",
    )(block_e, nb_used, xs_tm, w_gu, b_gu, w_down, b_down)


def _combine_kernel(x_ref, yk_ref, gate_ref, g_ref, b_ref, o_ref):
    tm = x_ref.shape[0]
    gates = gate_ref[...]
    m = jnp.zeros((tm, D_MODEL), F32)
    for k in range(TOP_K):
        m = m + gates[:, k:k + 1] * _load_token_major(yk_ref.at[k], tm)
    o_ref[...] = _layer_norm(DEEPNORM_ALPHA * x_ref[...] + m, g_ref[...], b_ref[...])


def _combine(x2, yk4, gates_pad, ln_g, ln_b):
    t_tok = x2.shape[0]
    tm = TM_COMB
    return pl.pallas_call(
        _combine_kernel,
        out_shape=jax.ShapeDtypeStruct((t_tok, D_MODEL), F32),
        grid=(t_tok // tm,),
        in_specs=[pl.BlockSpec((tm, D_MODEL), lambda i: (i, 0)),
                  pl.BlockSpec((TOP_K, tm * V7X_SUBLANES, V7X_LANES), lambda i: (0, i, 0)),
                  pl.BlockSpec((tm, E_PAD), lambda i: (i, 0)),
                  pl.BlockSpec((1, D_MODEL), lambda i: (0, 0)),
                  pl.BlockSpec((1, D_MODEL), lambda i: (0, 0))],
        out_specs=pl.BlockSpec((tm, D_MODEL), lambda i: (i, 0)),
        compiler_params=pltpu.CompilerParams(dimension_semantics=("arbitrary",)),
        name="combine",
    )(x2, yk4, gates_pad, ln_g, ln_b)


def _moe(x1, x1_tm, wr_pad, br_pad, w_gu, b_gu, w_down, b_down, ln_g, ln_b):
    t_tok = x1.shape[0]
    n_assign = t_tok * TOP_K
    bm = BM_EXPERT
    n_blocks = n_assign // bm + N_EXPERTS
    n_slots = n_blocks * bm

    idx_pad, gates_pad, rank_pad, counts_f = _router(x1, wr_pad, br_pad)
    top_idx = idx_pad[:, :TOP_K]
    rank = rank_pad[:, :TOP_K]

    counts = counts_f[0, :N_EXPERTS].astype(jnp.int32)
    blocks_per_e = (counts + bm - 1) // bm
    blk_end = jnp.cumsum(blocks_per_e)
    row_start = (blk_end - blocks_per_e) * bm
    experts = jnp.arange(N_EXPERTS, dtype=jnp.int32)
    start_of = jnp.sum(jnp.where(top_idx[:, :, None] == experts, row_start, 0), axis=-1)
    dest_km = (start_of + rank).T
    block_e = jnp.minimum(
        jnp.sum(blk_end[None, :] <= jnp.arange(n_blocks, dtype=jnp.int32)[:, None], axis=1),
        N_EXPERTS - 1).astype(jnp.int32)
    nb_used = blk_end[-1:].astype(jnp.int32)

    tile = (V7X_SUBLANES, V7X_LANES)
    xs3 = _dispatch_rows(x1_tm.reshape(t_tok, *tile), dest_km, n_slots)
    ys_tm = _experts(block_e, nb_used, xs3.reshape(n_slots * V7X_SUBLANES, V7X_LANES),
                     w_gu, b_gu, w_down, b_down)
    yk3 = _gather_rows(ys_tm.reshape(n_slots, *tile), dest_km.reshape(n_assign))
    yk4 = yk3.reshape(TOP_K, t_tok * V7X_SUBLANES, V7X_LANES)
    return _combine(x1, yk4, gates_pad, ln_g, ln_b)


def kernel(x, w_in, b_in, conv_a, w_out_a, w_pool, scale_pool, conv_c, conv_c_b, ln_c_g, ln_c_b,
           w_out_c, b_out_c, w_o, ln1_g, ln1_b, w_router, b_router, w_gu, b_gu, w_down, b_down,
           ln2_g, ln2_b):
    bsz, seq_len, d = x.shape
    assert d == D_MODEL
    t_tok = bsz * seq_len
    depth = w_in.shape[0]

    def row(v):
        return v[:, None, :]

    w_in_b = w_in.astype(BF16)
    w_out_a_b = w_out_a.astype(BF16)
    w_pool_b = w_pool.astype(BF16)
    w_out_c_b = w_out_c.astype(BF16)
    w_o_b = w_o.astype(BF16)
    wr_pad = jnp.pad(w_router, ((0, 0), (0, 0), (0, E_PAD - N_EXPERTS))).astype(BF16)
    br_pad = jnp.pad(b_router, ((0, 0), (0, E_PAD - N_EXPERTS)), constant_values=NEG_BIG)
    b_gu_d = jnp.concatenate([b_gu[..., 0::2], b_gu[..., 1::2]], axis=-1)[:, :, None, :]
    b_down_d = b_down[:, :, None, :]

    x2 = x.reshape(t_tok, d)
    for l in range(depth):
        x1, x1_tm = _mixer(x2, w_in_b[l], row(b_in)[l], conv_a[l], w_out_a_b[l], w_pool_b[l],
                           row(scale_pool)[l], conv_c[l], row(conv_c_b)[l], row(ln_c_g)[l],
                           row(ln_c_b)[l], w_out_c_b[l], row(b_out_c)[l], w_o_b[l],
                           row(ln1_g)[l], row(ln1_b)[l], seq_len=seq_len)
        x2 = _moe(x1, x1_tm, wr_pad[l], row(br_pad)[l], w_gu[l], b_gu_d[l], w_down[l],
                  b_down_d[l], row(ln2_g)[l], row(ln2_b)[l])
    return x2.reshape(bsz, seq_len, d)
```

```python
import functools

import jax
import jax.numpy as jnp
from jax import lax
from jax.experimental import pallas as pl
from jax.experimental.pallas import tpu as pltpu
from jax.experimental.pallas import tpu_sc as plsc

D_MODEL = 1024
DEPTH = 4
CONV_A_WIDTH = 3
POOL_WINDOWS = (2, 4, 8, 16)
POOL_GROUP_DIM = D_MODEL // len(POOL_WINDOWS)
CONV_C_WIDTH = 31
N_EXPERTS = 32
TOP_K = 4
D_FF = D_MODEL
SWIGLU_LIMIT = 7.0
SWIGLU_ALPHA = 1.702
LN_EPS = 1e-5
DEEPNORM_ALPHA = (2.0 * DEPTH) ** 0.25

V7X_LANES = 128
V7X_SUBLANES = 8
V7X_VMEM_LIMIT_BYTES = 56 * 1024 * 1024
V7X_MXU_DIM = 256
V7X_SC_CORES = 2
V7X_SC_SUBCORES = 16
V7X_SC_WORKERS = V7X_SC_CORES * V7X_SC_SUBCORES

TM_MIX = 256
HALO_A = 8
HALO_P = 16
HALO_C = 32
CONV_LANE_CHUNK = 128
TAP_ROW_BLOCK = 64
TR_ROUTE = 512
E_PAD = V7X_LANES
BM_EXPERT = 256
TM_COMB = 256
DEINTERLEAVE_COLS = V7X_MXU_DIM
ROW_WORDS_TILES = D_MODEL // 2 // V7X_LANES
SC_ROWS = 32
NEG_BIG = -3.0e38

F32 = jnp.float32
BF16 = jnp.bfloat16


def _layer_norm(x, g, b):
    mu = jnp.mean(x, axis=-1, keepdims=True)
    xc = x - mu
    var = jnp.mean(xc * xc, axis=-1, keepdims=True)
    return xc * lax.rsqrt(var + LN_EPS) * g + b


def _store_token_rows(ref, val):
    rows = val.shape[0]
    words = pltpu.pack_elementwise([val[:, :D_MODEL // 2], val[:, D_MODEL // 2:]],
                                   packed_dtype=BF16)
    for g in range(ROW_WORDS_TILES):
        ref[pl.ds(g, rows, stride=ROW_WORDS_TILES), :] = words[:, g * V7X_LANES:(g + 1) * V7X_LANES]


def _load_token_rows(ref, rows):
    words = jnp.concatenate([ref[pl.ds(g, rows, stride=ROW_WORDS_TILES), :]
                             for g in range(ROW_WORDS_TILES)], axis=1)
    halves = [pltpu.unpack_elementwise(words, index=i, packed_dtype=BF16, unpacked_dtype=F32)
              for i in range(2)]
    return jnp.concatenate(halves, axis=1)


def _realign(ext_ref, shift_ref, halo, rows, lane0, lanes, max_shift):
    n = halo + rows - V7X_SUBLANES
    for b in range(1, min(max_shift, V7X_SUBLANES - 1) + 1):
        shift_ref[b - 1, 0:n, lane0:lane0 + lanes] = (
            ext_ref[V7X_SUBLANES - b:V7X_SUBLANES - b + n, lane0:lane0 + lanes])


def _causal_taps(ext_ref, shift_ref, halo, rows, lane0, lanes, weights):
    outs = []
    for r0 in range(0, rows, TAP_ROW_BLOCK):
        acc = None
        for j, w in enumerate(weights):
            a, b = divmod(j, V7X_SUBLANES)
            if b == 0:
                start = halo - V7X_SUBLANES * a + r0
                term = ext_ref[start:start + TAP_ROW_BLOCK, lane0:lane0 + lanes]
            else:
                start = halo - V7X_SUBLANES * (a + 1) + r0
                term = shift_ref[b - 1, start:start + TAP_ROW_BLOCK, lane0:lane0 + lanes]
            if w is not None:
                term = term * w
            acc = term if acc is None else acc + term
        outs.append(acc)
    return jnp.concatenate(outs, axis=0)


def _mixer_kernel(x_ref, w_in_ref, b_in_ref, conv_a_ref, w_out_a_ref, w_pool_ref, scale_pool_ref,
                  conv_c_ref, conv_c_b_ref, ln_c_g_ref, ln_c_b_ref, w_out_c_ref, b_out_c_ref,
                  w_o_ref, ln1_g_ref, ln1_b_ref,
                  y_ref, y_rows_ref,
                  xb_ref, ext_a, ext_p, ext_c, shift_ref, *, tiles_per_seq):
    tm = x_ref.shape[0]
    tile_in_seq = pl.program_id(0) % tiles_per_seq

    @pl.when(tile_in_seq == 0)
    def _():
        ext_a[0:HALO_A, :] = jnp.zeros((HALO_A, D_MODEL), F32)
        ext_p[0:HALO_P, :] = jnp.zeros((HALO_P, D_MODEL), F32)
        ext_c[0:HALO_C, :] = jnp.zeros((HALO_C, D_MODEL), F32)

    xb_ref[...] = x_ref[...].astype(BF16)

    def proj(col):
        lo, hi = col * D_MODEL, (col + 1) * D_MODEL
        return (jnp.dot(xb_ref[...], w_in_ref[0, :, lo:hi], preferred_element_type=F32)
                + b_in_ref[0, :, lo:hi])

    def gate(col):
        return jax.nn.sigmoid(proj(col))

    ext_c[HALO_C:HALO_C + tm, :] = proj(4) * gate(5)
    conv_c_parts = []

    def conv_c_chunk(i):
        c0 = i * CONV_LANE_CHUNK
        w = [conv_c_ref[0, CONV_C_WIDTH - 1 - j:CONV_C_WIDTH - j, c0:c0 + CONV_LANE_CHUNK]
             for j in range(CONV_C_WIDTH)]
        _realign(ext_c, shift_ref,HALO_C, tm, c0, CONV_LANE_CHUNK, CONV_C_WIDTH - 1)
        conv_c_parts.append(_causal_taps(ext_c, shift_ref,HALO_C, tm, c0, CONV_LANE_CHUNK, w))

    conv_c_chunk(0)
    c_a = proj(1)
    conv_c_chunk(1)
    ext_a[HALO_A:HALO_A + tm, :] = c_a * proj(2)
    conv_c_chunk(2)
    b_a = proj(0)
    conv_c_chunk(3)
    gate_a = gate(6)
    conv_c_chunk(4)
    p_in = proj(3)
    ext_p[HALO_P:HALO_P + tm, :] = p_in
    conv_c_chunk(5)
    gate_b = gate(7)
    conv_c_chunk(6)
    gate_c = gate(8)
    conv_c_chunk(7)

    parts = []
    for c0 in range(0, D_MODEL, CONV_LANE_CHUNK):
        w = [conv_a_ref[0, CONV_A_WIDTH - 1 - j:CONV_A_WIDTH - j, c0:c0 + CONV_LANE_CHUNK]
             for j in range(CONV_A_WIDTH)]
        _realign(ext_a, shift_ref,HALO_A, tm, c0, CONV_LANE_CHUNK, CONV_A_WIDTH - 1)
        parts.append(_causal_taps(ext_a, shift_ref,HALO_A, tm, c0, CONV_LANE_CHUNK, w))
    u_a = jnp.concatenate(parts, axis=1)
    ext_a[0:HALO_A, :] = ext_a[tm:tm + HALO_A, :]
    y_a = jnp.dot((b_a * u_a).astype(BF16), w_out_a_ref[0], preferred_element_type=F32)
    merged = gate_a * y_a

    v = jnp.concatenate(conv_c_parts, axis=1) + conv_c_b_ref[0]
    ext_c[0:HALO_C, :] = ext_c[tm:tm + HALO_C, :]
    v = _layer_norm(v, ln_c_g_ref[0], ln_c_b_ref[0])
    v = v * jax.nn.sigmoid(v)
    y_c = jnp.dot(v.astype(BF16), w_out_c_ref[0], preferred_element_type=F32) + b_out_c_ref[0]
    merged = merged + gate_c * y_c

    pos = tile_in_seq * tm + lax.broadcasted_iota(jnp.int32, (tm, POOL_GROUP_DIM), 0)
    parts = []
    for g, win in enumerate(POOL_WINDOWS):
        lo = g * POOL_GROUP_DIM
        sub = []
        for c0 in range(lo, lo + POOL_GROUP_DIM, CONV_LANE_CHUNK):
            _realign(ext_p, shift_ref,HALO_P, tm, c0, CONV_LANE_CHUNK, win - 1)
            sub.append(_causal_taps(ext_p, shift_ref,HALO_P, tm, c0, CONV_LANE_CHUNK,
                                    [None] * win))
        wsum = jnp.concatenate(sub, axis=1)
        cnt = jnp.minimum(pos + 1, win).astype(F32)
        pooled = wsum / cnt - p_in[:, lo:lo + POOL_GROUP_DIM]
        parts.append(jnp.dot(pooled.astype(BF16), w_pool_ref[0, g], preferred_element_type=F32))
    ext_p[0:HALO_P, :] = ext_p[tm:tm + HALO_P, :]
    y_b = jnp.concatenate(parts, axis=1) * scale_pool_ref[0]
    merged = merged + gate_b * y_b

    h = jnp.dot(merged.astype(BF16), w_o_ref[0], preferred_element_type=F32)
    y = _layer_norm(DEEPNORM_ALPHA * x_ref[...] + h, ln1_g_ref[0], ln1_b_ref[0])
    y_ref[...] = y
    _store_token_rows(y_rows_ref, y)


def _layer_block(arr, layer, buffered_once=False):
    tail = (0,) * (arr.ndim - 1)
    mode = {"pipeline_mode": pl.Buffered(1)} if buffered_once else {}
    return pl.BlockSpec((1,) + arr.shape[1:], lambda *_: (layer,) + tail, **mode)


def _mixer(x2, layer, consts, *, seq_len):
    t_tok = x2.shape[0]
    tm = TM_MIX
    assert seq_len % tm == 0 and t_tok % seq_len == 0
    return pl.pallas_call(
        functools.partial(_mixer_kernel, tiles_per_seq=seq_len // tm),
        out_shape=(jax.ShapeDtypeStruct((t_tok, D_MODEL), F32),
                   jax.ShapeDtypeStruct((t_tok * ROW_WORDS_TILES, V7X_LANES), jnp.int32)),
        grid=(t_tok // tm,),
        in_specs=[pl.BlockSpec((tm, D_MODEL), lambda i: (i, 0))]
                 + [_layer_block(c, layer, buffered_once=True) for c in consts],
        out_specs=(pl.BlockSpec((tm, D_MODEL), lambda i: (i, 0)),
                   pl.BlockSpec((tm * ROW_WORDS_TILES, V7X_LANES), lambda i: (i, 0))),
        scratch_shapes=[pltpu.VMEM((tm, D_MODEL), BF16),
                        pltpu.VMEM((HALO_A + tm, D_MODEL), F32),
                        pltpu.VMEM((HALO_P + tm, D_MODEL), F32),
                        pltpu.VMEM((HALO_C + tm, D_MODEL), F32),
                        pltpu.VMEM((V7X_SUBLANES - 1, HALO_C + tm - V7X_SUBLANES, D_MODEL), F32)],
        compiler_params=pltpu.CompilerParams(dimension_semantics=("arbitrary",),
                                             vmem_limit_bytes=V7X_VMEM_LIMIT_BYTES),
        name="mixer",
    )(x2, *consts)


def _router_kernel(x_ref, wr_ref, br_ref, idx_ref, gate_ref, rank_ref, counts_ref, run_ref):
    tr = x_ref.shape[0]

    @pl.when(pl.program_id(0) == 0)
    def _():
        run_ref[...] = jnp.zeros(run_ref.shape, F32)

    logits = jnp.dot(x_ref[...].astype(BF16), wr_ref[0], preferred_element_type=F32) + br_ref[0]
    lane = lax.broadcasted_iota(jnp.int32, (tr, E_PAD), 1)
    work = logits
    vals, idxs = [], []
    for _ in range(TOP_K):
        m = jnp.max(work, axis=-1, keepdims=True)
        idx = jnp.min(jnp.where(work == m, lane, E_PAD), axis=-1, keepdims=True)
        vals.append(m)
        idxs.append(idx)
        work = jnp.where(lane == idx, NEG_BIG, work)
    exps = [jnp.exp(v - vals[0]) for v in vals]
    denom = exps[0] + exps[1] + exps[2] + exps[3]

    sel = jnp.zeros((tr, E_PAD), F32)
    for k in range(TOP_K):
        sel = sel + jnp.where(lane == idxs[k], 1.0, 0.0)
    r_i = lax.broadcasted_iota(jnp.int32, (tr, tr), 0)
    c_i = lax.broadcasted_iota(jnp.int32, (tr, tr), 1)
    before = jnp.where(c_i < r_i, 1.0, 0.0).astype(BF16)
    prior = jnp.dot(before, sel.astype(BF16), preferred_element_type=F32) + run_ref[0:1, :]

    idx_out = jnp.zeros((tr, E_PAD), jnp.int32)
    gate_out = jnp.zeros((tr, E_PAD), F32)
    rank_out = jnp.zeros((tr, E_PAD), F32)
    for k in range(TOP_K):
        rank_k = jnp.sum(jnp.where(lane == idxs[k], prior, 0.0), axis=-1, keepdims=True)
        idx_out = jnp.where(lane == k, idxs[k], idx_out)
        gate_out = jnp.where(lane == k, exps[k] / denom, gate_out)
        rank_out = jnp.where(lane == k, rank_k, rank_out)
    idx_ref[...] = idx_out
    gate_ref[...] = gate_out
    rank_ref[...] = rank_out.astype(jnp.int32)

    run_ref[0:1, :] = run_ref[0:1, :] + jnp.sum(sel, axis=0, keepdims=True)
    counts_ref[...] = run_ref[...]


def _router(x2, layer, wr_pad, br_pad):
    t_tok = x2.shape[0]
    tr = TR_ROUTE
    tok_spec = pl.BlockSpec((tr, E_PAD), lambda i: (i, 0))
    return pl.pallas_call(
        _router_kernel,
        out_shape=(jax.ShapeDtypeStruct((t_tok, E_PAD), jnp.int32),
                   jax.ShapeDtypeStruct((t_tok, E_PAD), F32),
                   jax.ShapeDtypeStruct((t_tok, E_PAD), jnp.int32),
                   jax.ShapeDtypeStruct((V7X_SUBLANES, E_PAD), F32)),
        grid=(t_tok // tr,),
        in_specs=[pl.BlockSpec((tr, D_MODEL), lambda i: (i, 0)),
                  _layer_block(wr_pad, layer), _layer_block(br_pad, layer)],
        out_specs=(tok_spec, tok_spec, tok_spec,
                   pl.BlockSpec((V7X_SUBLANES, E_PAD), lambda i: (0, 0))),
        scratch_shapes=[pltpu.VMEM((V7X_SUBLANES, E_PAD), F32)],
        compiler_params=pltpu.CompilerParams(dimension_semantics=("arbitrary",)),
        name="router",
    )(x2, wr_pad, br_pad)


def _sc_mesh():
    return plsc.VectorSubcoreMesh(core_axis_name="c", subcore_axis_name="s")


def _sc_worker_id():
    return lax.axis_index("s") * V7X_SC_CORES + lax.axis_index("c")


def _dispatch_rows(x3, dest_km, n_slots):
    t_tok = x3.shape[0]
    tok_per_worker = t_tok // V7X_SC_WORKERS
    assert tok_per_worker % SC_ROWS == 0

    def body(x_hbm, dest_hbm, out_hbm, rows_v, idx_v, sem):
        base = _sc_worker_id() * tok_per_worker

        @pl.loop(0, tok_per_worker // SC_ROWS)
        def _(step):
            t0 = pl.multiple_of(base + step * SC_ROWS, SC_ROWS)
            pltpu.sync_copy(x_hbm.at[pl.ds(t0, SC_ROWS)], rows_v)
            for k in range(TOP_K):
                pltpu.sync_copy(dest_hbm.at[k, pl.ds(t0, SC_ROWS)], idx_v.at[k])
            for k in range(TOP_K):
                pltpu.async_copy(rows_v, out_hbm.at[idx_v.at[k]], sem).wait()

    return pl.kernel(
        body, mesh=_sc_mesh(),
        out_type=jax.ShapeDtypeStruct((n_slots,) + x3.shape[1:], x3.dtype),
        scratch_types=[pltpu.VMEM((SC_ROWS,) + x3.shape[1:], x3.dtype),
                       pltpu.VMEM((TOP_K, SC_ROWS), jnp.int32),
                       pltpu.SemaphoreType.DMA],
    )(x3, dest_km)


def _gather_rows(ys3, src_rows):
    n = src_rows.shape[0]
    rows_per_worker = n // V7X_SC_WORKERS
    assert rows_per_worker % SC_ROWS == 0

    def body(ys_hbm, src_hbm, out_hbm, rows_v, idx_v, sem):
        base = _sc_worker_id() * rows_per_worker

        @pl.loop(0, rows_per_worker // SC_ROWS)
        def _(step):
            a0 = pl.multiple_of(base + step * SC_ROWS, SC_ROWS)
            pltpu.sync_copy(src_hbm.at[pl.ds(a0, SC_ROWS)], idx_v)
            pltpu.async_copy(ys_hbm.at[idx_v], rows_v, sem).wait()
            pltpu.sync_copy(rows_v, out_hbm.at[pl.ds(a0, SC_ROWS)])

    return pl.kernel(
        body, mesh=_sc_mesh(),
        out_type=jax.ShapeDtypeStruct((n,) + ys3.shape[1:], ys3.dtype),
        scratch_types=[pltpu.VMEM((SC_ROWS,) + ys3.shape[1:], ys3.dtype),
                       pltpu.VMEM((SC_ROWS,), jnp.int32),
                       pltpu.SemaphoreType.DMA],
    )(ys3, src_rows)


def _expert_kernel(block_e_ref, nb_used_ref, xs_ref, w_gu_ref, b_gu_ref, w_down_ref, b_down_ref,
                   ys_ref, w_gu_b, w_down_b):
    bm = BM_EXPERT
    b = pl.program_id(0)
    new_expert = jnp.logical_or(b == 0, block_e_ref[b] != block_e_ref[jnp.maximum(b - 1, 0)])

    @pl.when(jnp.logical_and(new_expert, b < nb_used_ref[0]))
    def _():
        cb = DEINTERLEAVE_COLS
        src = lax.broadcasted_iota(jnp.int32, (cb, cb), 0)
        dst = lax.broadcasted_iota(jnp.int32, (cb, cb), 1)
        pick = jnp.where(dst < cb // 2, 2 * dst, 2 * (dst - cb // 2) + 1)
        sel = jnp.where(src == pick, 1.0, 0.0).astype(BF16)
        for c in range(2 * D_FF // cb):
            blk = jnp.dot(w_gu_ref[0, :, c * cb:(c + 1) * cb].astype(BF16), sel,
                          preferred_element_type=F32).astype(BF16)
            lo = c * (cb // 2)
            w_gu_b[:, lo:lo + cb // 2] = blk[:, :cb // 2]
            w_gu_b[:, D_FF + lo:D_FF + lo + cb // 2] = blk[:, cb // 2:]
        w_down_b[...] = w_down_ref[0].astype(BF16)

    @pl.when(b < nb_used_ref[0])
    def _():
        xb = _load_token_rows(xs_ref, bm).astype(BF16)
        h = jnp.dot(xb, w_gu_b[...], preferred_element_type=F32) + b_gu_ref[0]
        gate = jnp.minimum(h[:, :D_FF], SWIGLU_LIMIT)
        up = jnp.clip(h[:, D_FF:], -SWIGLU_LIMIT, SWIGLU_LIMIT)
        act = (up + 1.0) * (gate * jax.nn.sigmoid(SWIGLU_ALPHA * gate))
        y = jnp.dot(act.astype(BF16), w_down_b[...], preferred_element_type=F32) + b_down_ref[0]
        _store_token_rows(ys_ref, y)

    @pl.when(b >= nb_used_ref[0])
    def _():
        ys_ref[...] = jnp.zeros(ys_ref.shape, ys_ref.dtype)


def _experts(block_e, nb_used, xs_rows, layer, w_gu, b_gu, w_down, b_down):
    bm = BM_EXPERT
    n_blocks = xs_rows.shape[0] // (bm * ROW_WORDS_TILES)

    def x_map(b, be, nb):
        return (jnp.minimum(b, nb[0] - 1), 0)

    def e_map(b, be, nb):
        return (layer * N_EXPERTS + be[b], 0, 0)

    row_block = (bm * ROW_WORDS_TILES, V7X_LANES)
    grid_spec = pltpu.PrefetchScalarGridSpec(
        num_scalar_prefetch=2,
        grid=(n_blocks,),
        in_specs=[pl.BlockSpec(row_block, x_map),
                  pl.BlockSpec((1, D_MODEL, 2 * D_FF), e_map),
                  pl.BlockSpec((1, 1, 2 * D_FF), e_map),
                  pl.BlockSpec((1, D_FF, D_MODEL), e_map),
                  pl.BlockSpec((1, 1, D_MODEL), e_map)],
        out_specs=pl.BlockSpec(row_block, lambda b, be, nb: (b, 0)),
        scratch_shapes=[pltpu.VMEM((D_MODEL, 2 * D_FF), BF16),
                        pltpu.VMEM((D_FF, D_MODEL), BF16)],
    )
    return pl.pallas_call(
        _expert_kernel,
        out_shape=jax.ShapeDtypeStruct(xs_rows.shape, xs_rows.dtype),
        grid_spec=grid_spec,
        compiler_params=pltpu.CompilerParams(dimension_semantics=("arbitrary",),
                                             vmem_limit_bytes=V7X_VMEM_LIMIT_BYTES),
        name="experts",
    )(block_e, nb_used, xs_rows, w_gu, b_gu, w_down, b_down)


def _combine_kernel(x_ref, yk_ref, gate_ref, g_ref, b_ref, o_ref):
    tm = x_ref.shape[0]
    gates = gate_ref[...]
    m = jnp.zeros((tm, D_MODEL), F32)
    for k in range(TOP_K):
        m = m + gates[:, k:k + 1] * _load_token_rows(yk_ref.at[k], tm)
    o_ref[...] = _layer_norm(DEEPNORM_ALPHA * x_ref[...] + m, g_ref[0], b_ref[0])


def _combine(x2, yk4, gates_pad, layer, ln_g, ln_b):
    t_tok = x2.shape[0]
    tm = TM_COMB
    return pl.pallas_call(
        _combine_kernel,
        out_shape=jax.ShapeDtypeStruct((t_tok, D_MODEL), F32),
        grid=(t_tok // tm,),
        in_specs=[pl.BlockSpec((tm, D_MODEL), lambda i: (i, 0)),
                  pl.BlockSpec((TOP_K, tm * ROW_WORDS_TILES, V7X_LANES), lambda i: (0, i, 0)),
                  pl.BlockSpec((tm, E_PAD), lambda i: (i, 0)),
                  _layer_block(ln_g, layer), _layer_block(ln_b, layer)],
        out_specs=pl.BlockSpec((tm, D_MODEL), lambda i: (i, 0)),
        compiler_params=pltpu.CompilerParams(dimension_semantics=("arbitrary",)),
        name="combine",
    )(x2, yk4, gates_pad, ln_g, ln_b)


def _moe(x1, x1_rows, layer, wr_pad, br_pad, w_gu, b_gu, w_down, b_down, ln_g, ln_b):
    t_tok = x1.shape[0]
    n_assign = t_tok * TOP_K
    bm = BM_EXPERT
    n_blocks = n_assign // bm + N_EXPERTS
    n_slots = n_blocks * bm

    idx_pad, gates_pad, rank_pad, counts_f = _router(x1, layer, wr_pad, br_pad)
    top_idx = idx_pad[:, :TOP_K]
    rank = rank_pad[:, :TOP_K]

    counts = counts_f[0, :N_EXPERTS].astype(jnp.int32)
    blocks_per_e = (counts + bm - 1) // bm
    blk_end = jnp.cumsum(blocks_per_e)
    row_start = (blk_end - blocks_per_e) * bm
    experts = jnp.arange(N_EXPERTS, dtype=jnp.int32)
    start_of = jnp.sum(jnp.where(top_idx[:, :, None] == experts, row_start, 0), axis=-1)
    dest_km = (start_of + rank).T
    block_e = jnp.minimum(
        jnp.sum(blk_end[None, :] <= jnp.arange(n_blocks, dtype=jnp.int32)[:, None], axis=1),
        N_EXPERTS - 1).astype(jnp.int32)
    nb_used = blk_end[-1:].astype(jnp.int32)

    tile = (ROW_WORDS_TILES, V7X_LANES)
    xs3 = _dispatch_rows(x1_rows.reshape(t_tok, *tile), dest_km, n_slots)
    ys_rows = _experts(block_e, nb_used, xs3.reshape(n_slots * ROW_WORDS_TILES, V7X_LANES),
                       layer, w_gu, b_gu, w_down, b_down)
    yk3 = _gather_rows(ys_rows.reshape(n_slots, *tile), dest_km.reshape(n_assign))
    yk4 = yk3.reshape(TOP_K, t_tok * ROW_WORDS_TILES, V7X_LANES)
    return _combine(x1, yk4, gates_pad, layer, ln_g, ln_b)


def kernel(x, w_in, b_in, conv_a, w_out_a, w_pool, scale_pool, conv_c, conv_c_b, ln_c_g, ln_c_b,
           w_out_c, b_out_c, w_o, ln1_g, ln1_b, w_router, b_router, w_gu, b_gu, w_down, b_down,
           ln2_g, ln2_b):
    bsz, seq_len, d = x.shape
    assert d == D_MODEL
    t_tok = bsz * seq_len
    depth = w_in.shape[0]

    def row(v):
        return v[:, None, :]

    mixer_consts = (w_in.astype(BF16), row(b_in), conv_a, w_out_a.astype(BF16),
                    w_pool.astype(BF16), row(scale_pool), conv_c, row(conv_c_b), row(ln_c_g),
                    row(ln_c_b), w_out_c.astype(BF16), row(b_out_c), w_o.astype(BF16),
                    row(ln1_g), row(ln1_b))
    wr_pad = jnp.pad(w_router, ((0, 0), (0, 0), (0, E_PAD - N_EXPERTS))).astype(BF16)
    br_pad = row(jnp.pad(b_router, ((0, 0), (0, E_PAD - N_EXPERTS)), constant_values=NEG_BIG))
    n_le = depth * N_EXPERTS
    w_gu_f = w_gu.reshape(n_le, D_MODEL, 2 * D_FF)
    b_gu_d = jnp.concatenate([b_gu[..., 0::2], b_gu[..., 1::2]], axis=-1).reshape(n_le, 1, 2 * D_FF)
    w_down_f = w_down.reshape(n_le, D_FF, D_MODEL)
    b_down_d = b_down.reshape(n_le, 1, D_MODEL)
    ln2_g_r, ln2_b_r = row(ln2_g), row(ln2_b)

    x2 = x.reshape(t_tok, d)
    for layer in range(depth):
        x1, x1_rows = _mixer(x2, layer, mixer_consts, seq_len=seq_len)
        x2 = _moe(x1, x1_rows, layer, wr_pad, br_pad, w_gu_f, b_gu_d, w_down_f, b_down_d,
                  ln2_g_r, ln2_b_r)
    return x2.reshape(bsz, seq_len, d)
```

```python
import functools

import jax
import jax.numpy as jnp
from jax import lax
from jax.experimental import pallas as pl
from jax.experimental.pallas import tpu as pltpu
from jax.experimental.pallas import tpu_sc as plsc

D_MODEL = 1024
DEPTH = 4
CONV_A_WIDTH = 3
POOL_WINDOWS = (2, 4, 8, 16)
POOL_GROUP_DIM = D_MODEL // len(POOL_WINDOWS)
CONV_C_WIDTH = 31
N_EXPERTS = 32
TOP_K = 4
D_FF = D_MODEL
SWIGLU_LIMIT = 7.0
SWIGLU_ALPHA = 1.702
LN_EPS = 1e-5
DEEPNORM_ALPHA = (2.0 * DEPTH) ** 0.25

V7X_LANES = 128
V7X_SUBLANES = 8
V7X_VMEM_LIMIT_BYTES = 56 * 1024 * 1024
V7X_MXU_DIM = 256
V7X_SC_CORES = 2
V7X_SC_SUBCORES = 16
V7X_SC_WORKERS = V7X_SC_CORES * V7X_SC_SUBCORES

TM_MIX = 256
HALO_A = 8
HALO_P = 16
HALO_C = 32
CONV_LANE_CHUNK = 128
TAP_ROW_BLOCK = 64
CONV_C_ROW_BLOCK = 32
MIX_COL_ORDER = (4, 5, 1, 2, 0, 6, 3, 7, 8)
TR_ROUTE = 512
E_PAD = V7X_LANES
BM_EXPERT = 256
TM_COMB = 256
DEINTERLEAVE_COLS = V7X_MXU_DIM
ROW_WORDS_TILES = D_MODEL // 2 // V7X_LANES
SC_ROWS = 32
NEG_BIG = -3.0e38

F32 = jnp.float32
BF16 = jnp.bfloat16


def _layer_norm(x, g, b):
    mu = jnp.mean(x, axis=-1, keepdims=True)
    xc = x - mu
    var = jnp.mean(xc * xc, axis=-1, keepdims=True)
    return xc * lax.rsqrt(var + LN_EPS) * g + b


def _store_token_rows(ref, val):
    rows = val.shape[0]
    words = pltpu.pack_elementwise([val[:, :D_MODEL // 2], val[:, D_MODEL // 2:]],
                                   packed_dtype=BF16)
    for g in range(ROW_WORDS_TILES):
        ref[pl.ds(g, rows, stride=ROW_WORDS_TILES), :] = words[:, g * V7X_LANES:(g + 1) * V7X_LANES]


def _load_token_rows(ref, rows):
    words = jnp.concatenate([ref[pl.ds(g, rows, stride=ROW_WORDS_TILES), :]
                             for g in range(ROW_WORDS_TILES)], axis=1)
    halves = [pltpu.unpack_elementwise(words, index=i, packed_dtype=BF16, unpacked_dtype=F32)
              for i in range(2)]
    return jnp.concatenate(halves, axis=1)


def _realign(ext_ref, shift_ref, halo, rows, lane0, lanes, max_shift):
    n = halo + rows - V7X_SUBLANES
    for b in range(1, min(max_shift, V7X_SUBLANES - 1) + 1):
        shift_ref[b - 1, 0:n, lane0:lane0 + lanes] = (
            ext_ref[V7X_SUBLANES - b:V7X_SUBLANES - b + n, lane0:lane0 + lanes])


def _causal_taps(ext_ref, shift_ref, halo, rows, lane0, lanes, weights):
    outs = []
    for r0 in range(0, rows, TAP_ROW_BLOCK):
        acc = None
        for j, w in enumerate(weights):
            a, b = divmod(j, V7X_SUBLANES)
            if b == 0:
                start = halo - V7X_SUBLANES * a + r0
                term = ext_ref[start:start + TAP_ROW_BLOCK, lane0:lane0 + lanes]
            else:
                start = halo - V7X_SUBLANES * (a + 1) + r0
                term = shift_ref[b - 1, start:start + TAP_ROW_BLOCK, lane0:lane0 + lanes]
            if w is not None:
                term = term * w
            acc = term if acc is None else acc + term
        outs.append(acc)
    return jnp.concatenate(outs, axis=0)


def _mixer_kernel(x_ref, w_in_ref, b_in_ref, conv_a_ref, w_out_a_ref, w_pool_ref, scale_pool_ref,
                  conv_c_ref, conv_c_b_ref, ln_c_g_ref, ln_c_b_ref, w_out_c_ref, b_out_c_ref,
                  w_o_ref, ln1_g_ref, ln1_b_ref,
                  y_ref, y_rows_ref,
                  xb_ref, ext_a, ext_p, ext_c, shift_ref, v_ref, z_ref, *, tiles_per_seq):
    tm = x_ref.shape[0]
    tile_in_seq = pl.program_id(0) % tiles_per_seq

    @pl.when(tile_in_seq == 0)
    def _():
        ext_a[0:HALO_A, :] = jnp.zeros((HALO_A, D_MODEL), F32)
        ext_p[0:HALO_P, :] = jnp.zeros((HALO_P, D_MODEL), F32)
        ext_c[0:HALO_C, :] = jnp.zeros((HALO_C, D_MODEL), F32)

    xb_ref[...] = x_ref[...].astype(BF16)

    def proj(slot):
        return (jnp.dot(xb_ref[...], w_in_ref[0, slot], preferred_element_type=F32)
                + b_in_ref[0, slot])

    ext_c[HALO_C:HALO_C + tm, :] = proj(0) * jax.nn.sigmoid(proj(1))
    _realign(ext_c, shift_ref, HALO_C, tm, 0, D_MODEL, CONV_C_WIDTH - 1)

    n_row_blocks = tm // CONV_C_ROW_BLOCK
    n_loop_proj = len(MIX_COL_ORDER) - 2
    assert n_loop_proj <= n_row_blocks
    taps_of_copy = {}
    for j in range(CONV_C_WIDTH):
        a, b = divmod(j, V7X_SUBLANES)
        taps_of_copy.setdefault(b, []).append((a, j))

    def aligned(row):
        return row if isinstance(row, int) else pl.multiple_of(row, V7X_SUBLANES)

    def conv_rows(r0, lane_lo=0, lane_hi=D_MODEL):
        for c0 in range(lane_lo, lane_hi, CONV_LANE_CHUNK):
            lanes = slice(c0, c0 + CONV_LANE_CHUNK)
            acc = conv_c_b_ref[0, :, lanes]
            for b, taps in taps_of_copy.items():
                a_max = max(a for a, _ in taps)
                first = HALO_C - V7X_SUBLANES * (a_max + (1 if b else 0))
                span = CONV_C_ROW_BLOCK + V7X_SUBLANES * a_max
                if b == 0:
                    win = ext_c[pl.ds(aligned(r0 + first), span), lanes]
                else:
                    win = shift_ref[b - 1, pl.ds(aligned(r0 + first), span), lanes]
                for a, j in taps:
                    off = V7X_SUBLANES * (a_max - a)
                    w = conv_c_ref[0, CONV_C_WIDTH - 1 - j:CONV_C_WIDTH - j, lanes]
                    acc = acc + win[off:off + CONV_C_ROW_BLOCK] * w
            v_ref[pl.ds(aligned(r0), CONV_C_ROW_BLOCK), lanes] = acc

    def conv_block(i, carry):
        conv_rows(i * CONV_C_ROW_BLOCK)
        return carry

    lax.fori_loop(0, n_row_blocks, conv_block, 0)
    ext_c[0:HALO_C, :] = ext_c[tm:tm + HALO_C, :]
    for i in range(n_loop_proj):
        z_ref[i] = proj(i + 2)

    ext_a[HALO_A:HALO_A + tm, :] = z_ref[0] * z_ref[1]
    b_a = z_ref[2]
    gate_a = jax.nn.sigmoid(z_ref[3])
    p_in = z_ref[4]
    ext_p[HALO_P:HALO_P + tm, :] = p_in
    gate_b = jax.nn.sigmoid(z_ref[5])
    gate_c = jax.nn.sigmoid(z_ref[6])

    parts = []
    for c0 in range(0, D_MODEL, CONV_LANE_CHUNK):
        w = [conv_a_ref[0, CONV_A_WIDTH - 1 - j:CONV_A_WIDTH - j, c0:c0 + CONV_LANE_CHUNK]
             for j in range(CONV_A_WIDTH)]
        _realign(ext_a, shift_ref,HALO_A, tm, c0, CONV_LANE_CHUNK, CONV_A_WIDTH - 1)
        parts.append(_causal_taps(ext_a, shift_ref,HALO_A, tm, c0, CONV_LANE_CHUNK, w))
    u_a = jnp.concatenate(parts, axis=1)
    ext_a[0:HALO_A, :] = ext_a[tm:tm + HALO_A, :]
    y_a = jnp.dot((b_a * u_a).astype(BF16), w_out_a_ref[0], preferred_element_type=F32)
    merged = gate_a * y_a

    v = _layer_norm(v_ref[...], ln_c_g_ref[0], ln_c_b_ref[0])
    v = v * jax.nn.sigmoid(v)
    y_c = jnp.dot(v.astype(BF16), w_out_c_ref[0], preferred_element_type=F32) + b_out_c_ref[0]
    merged = merged + gate_c * y_c

    pos = tile_in_seq * tm + lax.broadcasted_iota(jnp.int32, (tm, POOL_GROUP_DIM), 0)
    parts = []
    for g, win in enumerate(POOL_WINDOWS):
        lo = g * POOL_GROUP_DIM
        sub = []
        for c0 in range(lo, lo + POOL_GROUP_DIM, CONV_LANE_CHUNK):
            _realign(ext_p, shift_ref,HALO_P, tm, c0, CONV_LANE_CHUNK, win - 1)
            sub.append(_causal_taps(ext_p, shift_ref,HALO_P, tm, c0, CONV_LANE_CHUNK,
                                    [None] * win))
        wsum = jnp.concatenate(sub, axis=1)
        cnt = jnp.minimum(pos + 1, win).astype(F32)
        pooled = wsum / cnt - p_in[:, lo:lo + POOL_GROUP_DIM]
        parts.append(jnp.dot(pooled.astype(BF16), w_pool_ref[0, g], preferred_element_type=F32))
    ext_p[0:HALO_P, :] = ext_p[tm:tm + HALO_P, :]
    y_b = jnp.concatenate(parts, axis=1) * scale_pool_ref[0]
    merged = merged + gate_b * y_b

    h = jnp.dot(merged.astype(BF16), w_o_ref[0], preferred_element_type=F32)
    y = _layer_norm(DEEPNORM_ALPHA * x_ref[...] + h, ln1_g_ref[0], ln1_b_ref[0])
    y_ref[...] = y
    _store_token_rows(y_rows_ref, y)


def _layer_block(arr, layer, buffered_once=False):
    tail = (0,) * (arr.ndim - 1)
    mode = {"pipeline_mode": pl.Buffered(1)} if buffered_once else {}
    return pl.BlockSpec((1,) + arr.shape[1:], lambda *_: (layer,) + tail, **mode)


def _mixer(x2, layer, consts, *, seq_len):
    t_tok = x2.shape[0]
    tm = TM_MIX
    assert seq_len % tm == 0 and t_tok % seq_len == 0
    return pl.pallas_call(
        functools.partial(_mixer_kernel, tiles_per_seq=seq_len // tm),
        out_shape=(jax.ShapeDtypeStruct((t_tok, D_MODEL), F32),
                   jax.ShapeDtypeStruct((t_tok * ROW_WORDS_TILES, V7X_LANES), jnp.int32)),
        grid=(t_tok // tm,),
        in_specs=[pl.BlockSpec((tm, D_MODEL), lambda i: (i, 0))]
                 + [_layer_block(c, layer, buffered_once=True) for c in consts],
        out_specs=(pl.BlockSpec((tm, D_MODEL), lambda i: (i, 0)),
                   pl.BlockSpec((tm * ROW_WORDS_TILES, V7X_LANES), lambda i: (i, 0))),
        scratch_shapes=[pltpu.VMEM((tm, D_MODEL), BF16),
                        pltpu.VMEM((HALO_A + tm, D_MODEL), F32),
                        pltpu.VMEM((HALO_P + tm, D_MODEL), F32),
                        pltpu.VMEM((HALO_C + tm, D_MODEL), F32),
                        pltpu.VMEM((V7X_SUBLANES - 1, HALO_C + tm - V7X_SUBLANES, D_MODEL), F32),
                        pltpu.VMEM((tm, D_MODEL), F32),
                        pltpu.VMEM((len(MIX_COL_ORDER) - 2, tm, D_MODEL), F32)],
        compiler_params=pltpu.CompilerParams(dimension_semantics=("arbitrary",),
                                             vmem_limit_bytes=V7X_VMEM_LIMIT_BYTES),
        name="mixer",
    )(x2, *consts)


def _router_kernel(x_ref, wr_ref, br_ref, idx_ref, gate_ref, rank_ref, counts_ref, run_ref):
    tr = x_ref.shape[0]

    @pl.when(pl.program_id(0) == 0)
    def _():
        run_ref[...] = jnp.zeros(run_ref.shape, F32)

    logits = lax.dot_general(wr_ref[0], x_ref[...].astype(BF16), (((1,), (1,)), ((), ())),
                             preferred_element_type=F32) + br_ref[0]
    expert = lax.broadcasted_iota(jnp.int32, (N_EXPERTS, tr), 0)
    work = logits
    vals, idxs = [], []
    for _ in range(TOP_K):
        m = jnp.max(work, axis=0, keepdims=True)
        idx = jnp.min(jnp.where(work == m, expert, N_EXPERTS), axis=0, keepdims=True)
        vals.append(m)
        idxs.append(idx)
        work = jnp.where(expert == idx, NEG_BIG, work)
    exps = [jnp.exp(v - vals[0]) for v in vals]
    denom = exps[0] + exps[1] + exps[2] + exps[3]

    sel = jnp.zeros((N_EXPERTS, tr), F32)
    for k in range(TOP_K):
        sel = sel + jnp.where(expert == idxs[k], 1.0, 0.0)
    earlier = lax.broadcasted_iota(jnp.int32, (tr, tr), 0)
    later = lax.broadcasted_iota(jnp.int32, (tr, tr), 1)
    before = jnp.where(earlier < later, 1.0, 0.0).astype(BF16)
    prior = jnp.dot(sel.astype(BF16), before, preferred_element_type=F32) + run_ref[:, 0:1]

    row = lax.broadcasted_iota(jnp.int32, (V7X_SUBLANES, tr), 0)
    idx_out = jnp.zeros((V7X_SUBLANES, tr), jnp.int32)
    gate_out = jnp.zeros((V7X_SUBLANES, tr), F32)
    rank_out = jnp.zeros((V7X_SUBLANES, tr), F32)
    for k in range(TOP_K):
        rank_k = jnp.sum(jnp.where(expert == idxs[k], prior, 0.0), axis=0, keepdims=True)
        idx_out = jnp.where(row == k, idxs[k], idx_out)
        gate_out = jnp.where(row == k, exps[k] / denom, gate_out)
        rank_out = jnp.where(row == k, rank_k, rank_out)
    idx_ref[...] = idx_out
    gate_ref[...] = gate_out
    rank_ref[...] = rank_out.astype(jnp.int32)

    run_ref[...] = run_ref[...] + jnp.sum(sel, axis=1, keepdims=True)
    counts_ref[...] = run_ref[...]


def _router(x2, layer, wr_t, br_t):
    t_tok = x2.shape[0]
    tr = TR_ROUTE
    tok_spec = pl.BlockSpec((V7X_SUBLANES, tr), lambda i: (0, i))
    return pl.pallas_call(
        _router_kernel,
        out_shape=(jax.ShapeDtypeStruct((V7X_SUBLANES, t_tok), jnp.int32),
                   jax.ShapeDtypeStruct((V7X_SUBLANES, t_tok), F32),
                   jax.ShapeDtypeStruct((V7X_SUBLANES, t_tok), jnp.int32),
                   jax.ShapeDtypeStruct((N_EXPERTS, V7X_LANES), F32)),
        grid=(t_tok // tr,),
        in_specs=[pl.BlockSpec((tr, D_MODEL), lambda i: (i, 0)),
                  _layer_block(wr_t, layer), _layer_block(br_t, layer)],
        out_specs=(tok_spec, tok_spec, tok_spec,
                   pl.BlockSpec((N_EXPERTS, V7X_LANES), lambda i: (0, 0))),
        scratch_shapes=[pltpu.VMEM((N_EXPERTS, V7X_LANES), F32)],
        compiler_params=pltpu.CompilerParams(dimension_semantics=("arbitrary",)),
        name="router",
    )(x2, wr_t, br_t)


def _sc_mesh():
    return plsc.VectorSubcoreMesh(core_axis_name="c", subcore_axis_name="s")


def _sc_worker_id():
    return lax.axis_index("s") * V7X_SC_CORES + lax.axis_index("c")


def _dispatch_rows(x3, dest_km, n_slots):
    t_tok = x3.shape[0]
    tok_per_worker = t_tok // V7X_SC_WORKERS
    assert tok_per_worker % SC_ROWS == 0

    def body(x_hbm, dest_hbm, out_hbm, rows_v, idx_v, sem):
        base = _sc_worker_id() * tok_per_worker

        @pl.loop(0, tok_per_worker // SC_ROWS)
        def _(step):
            t0 = pl.multiple_of(base + step * SC_ROWS, SC_ROWS)
            pltpu.sync_copy(x_hbm.at[pl.ds(t0, SC_ROWS)], rows_v)
            for k in range(TOP_K):
                pltpu.sync_copy(dest_hbm.at[k, pl.ds(t0, SC_ROWS)], idx_v.at[k])
            for k in range(TOP_K):
                pltpu.async_copy(rows_v, out_hbm.at[idx_v.at[k]], sem).wait()

    return pl.kernel(
        body, mesh=_sc_mesh(),
        out_type=jax.ShapeDtypeStruct((n_slots,) + x3.shape[1:], x3.dtype),
        scratch_types=[pltpu.VMEM((SC_ROWS,) + x3.shape[1:], x3.dtype),
                       pltpu.VMEM((TOP_K, SC_ROWS), jnp.int32),
                       pltpu.SemaphoreType.DMA],
    )(x3, dest_km)


def _gather_rows(ys3, src_rows):
    n = src_rows.shape[0]
    rows_per_worker = n // V7X_SC_WORKERS
    assert rows_per_worker % SC_ROWS == 0

    def body(ys_hbm, src_hbm, out_hbm, rows_v, idx_v, sem):
        base = _sc_worker_id() * rows_per_worker

        @pl.loop(0, rows_per_worker // SC_ROWS)
        def _(step):
            a0 = pl.multiple_of(base + step * SC_ROWS, SC_ROWS)
            pltpu.sync_copy(src_hbm.at[pl.ds(a0, SC_ROWS)], idx_v)
            pltpu.async_copy(ys_hbm.at[idx_v], rows_v, sem).wait()
            pltpu.sync_copy(rows_v, out_hbm.at[pl.ds(a0, SC_ROWS)])

    return pl.kernel(
        body, mesh=_sc_mesh(),
        out_type=jax.ShapeDtypeStruct((n,) + ys3.shape[1:], ys3.dtype),
        scratch_types=[pltpu.VMEM((SC_ROWS,) + ys3.shape[1:], ys3.dtype),
                       pltpu.VMEM((SC_ROWS,), jnp.int32),
                       pltpu.SemaphoreType.DMA],
    )(ys3, src_rows)


def _expert_kernel(block_e_ref, nb_used_ref, next_e_ref, xs_ref, w_gu_hbm, b_gu_ref, w_down_hbm,
                   b_down_ref, ys_ref, w_gu_f, w_down_f, w_gu_b, w_down_b, sem_gu, sem_down,
                   slot_ref, *, layer):
    bm = BM_EXPERT
    b = pl.program_id(0)
    e = block_e_ref[b]
    live = b < nb_used_ref[0]
    new_expert = jnp.logical_or(b == 0, e != block_e_ref[jnp.maximum(b - 1, 0)])

    def weight_copies(expert, slot):
        idx = layer * N_EXPERTS + expert
        return (pltpu.make_async_copy(w_gu_hbm.at[idx], w_gu_f.at[slot], sem_gu.at[slot]),
                pltpu.make_async_copy(w_down_hbm.at[idx], w_down_f.at[slot], sem_down.at[slot]))

    @pl.when(b == 0)
    def _():
        slot_ref[0] = 0
        for copy in weight_copies(e, 0):
            copy.start()

    @pl.when(jnp.logical_and(new_expert, live))
    def _():
        slot = slot_ref[0]
        nxt = next_e_ref[e]

        @pl.when(nxt >= 0)
        def _():
            for copy in weight_copies(nxt, 1 - slot):
                copy.start()

        for copy in weight_copies(e, slot):
            copy.wait()
        slot_ref[0] = 1 - slot

        cb = DEINTERLEAVE_COLS
        src = lax.broadcasted_iota(jnp.int32, (cb, cb), 0)
        dst = lax.broadcasted_iota(jnp.int32, (cb, cb), 1)
        pick = jnp.where(dst < cb // 2, 2 * dst, 2 * (dst - cb // 2) + 1)
        sel = jnp.where(src == pick, 1.0, 0.0).astype(BF16)
        for c in range(2 * D_FF // cb):
            blk = jnp.dot(w_gu_f[slot, :, c * cb:(c + 1) * cb].astype(BF16), sel,
                          preferred_element_type=F32).astype(BF16)
            lo = c * (cb // 2)
            w_gu_b[:, lo:lo + cb // 2] = blk[:, :cb // 2]
            w_gu_b[:, D_FF + lo:D_FF + lo + cb // 2] = blk[:, cb // 2:]
        w_down_b[...] = w_down_f[slot].astype(BF16)

    @pl.when(live)
    def _():
        xb = _load_token_rows(xs_ref, bm).astype(BF16)
        h = jnp.dot(xb, w_gu_b[...], preferred_element_type=F32) + b_gu_ref[0]
        gate = jnp.minimum(h[:, :D_FF], SWIGLU_LIMIT)
        up = jnp.clip(h[:, D_FF:], -SWIGLU_LIMIT, SWIGLU_LIMIT)
        act = (up + 1.0) * (gate * jax.nn.sigmoid(SWIGLU_ALPHA * gate))
        y = jnp.dot(act.astype(BF16), w_down_b[...], preferred_element_type=F32) + b_down_ref[0]
        _store_token_rows(ys_ref, y)

    @pl.when(jnp.logical_not(live))
    def _():
        ys_ref[...] = jnp.zeros(ys_ref.shape, ys_ref.dtype)


def _experts(block_e, nb_used, next_e, xs_rows, layer, w_gu, b_gu, w_down, b_down):
    bm = BM_EXPERT
    n_blocks = xs_rows.shape[0] // (bm * ROW_WORDS_TILES)

    def x_map(b, be, nb, nx):
        return (jnp.minimum(b, nb[0] - 1), 0)

    def e_map(b, be, nb, nx):
        return (layer * N_EXPERTS + be[b], 0, 0)

    row_block = (bm * ROW_WORDS_TILES, V7X_LANES)
    grid_spec = pltpu.PrefetchScalarGridSpec(
        num_scalar_prefetch=3,
        grid=(n_blocks,),
        in_specs=[pl.BlockSpec(row_block, x_map),
                  pl.BlockSpec(memory_space=pl.ANY),
                  pl.BlockSpec((1, 1, 2 * D_FF), e_map),
                  pl.BlockSpec(memory_space=pl.ANY),
                  pl.BlockSpec((1, 1, D_MODEL), e_map)],
        out_specs=pl.BlockSpec(row_block, lambda b, be, nb, nx: (b, 0)),
        scratch_shapes=[pltpu.VMEM((2, D_MODEL, 2 * D_FF), F32),
                        pltpu.VMEM((2, D_FF, D_MODEL), F32),
                        pltpu.VMEM((D_MODEL, 2 * D_FF), BF16),
                        pltpu.VMEM((D_FF, D_MODEL), BF16),
                        pltpu.SemaphoreType.DMA((2,)),
                        pltpu.SemaphoreType.DMA((2,)),
                        pltpu.SMEM((1,), jnp.int32)],
    )
    return pl.pallas_call(
        functools.partial(_expert_kernel, layer=layer),
        out_shape=jax.ShapeDtypeStruct(xs_rows.shape, xs_rows.dtype),
        grid_spec=grid_spec,
        compiler_params=pltpu.CompilerParams(dimension_semantics=("arbitrary",),
                                             vmem_limit_bytes=V7X_VMEM_LIMIT_BYTES),
        name="experts",
    )(block_e, nb_used, next_e, xs_rows, w_gu, b_gu, w_down, b_down)


def _combine_kernel(x_ref, yk_ref, gate_ref, g_ref, b_ref, o_ref):
    tm = x_ref.shape[0]
    gates = gate_ref[...]
    m = jnp.zeros((tm, D_MODEL), F32)
    for k in range(TOP_K):
        m = m + gates[:, k:k + 1] * _load_token_rows(yk_ref.at[k], tm)
    o_ref[...] = _layer_norm(DEEPNORM_ALPHA * x_ref[...] + m, g_ref[0], b_ref[0])


def _combine(x2, yk4, gates_pad, layer, ln_g, ln_b):
    t_tok = x2.shape[0]
    tm = TM_COMB
    return pl.pallas_call(
        _combine_kernel,
        out_shape=jax.ShapeDtypeStruct((t_tok, D_MODEL), F32),
        grid=(t_tok // tm,),
        in_specs=[pl.BlockSpec((tm, D_MODEL), lambda i: (i, 0)),
                  pl.BlockSpec((TOP_K, tm * ROW_WORDS_TILES, V7X_LANES), lambda i: (0, i, 0)),
                  pl.BlockSpec((tm, E_PAD), lambda i: (i, 0)),
                  _layer_block(ln_g, layer), _layer_block(ln_b, layer)],
        out_specs=pl.BlockSpec((tm, D_MODEL), lambda i: (i, 0)),
        compiler_params=pltpu.CompilerParams(dimension_semantics=("arbitrary",)),
        name="combine",
    )(x2, yk4, gates_pad, ln_g, ln_b)


def _moe(x1, x1_rows, layer, wr_t, br_t, w_gu, b_gu, w_down, b_down, ln_g, ln_b):
    t_tok = x1.shape[0]
    n_assign = t_tok * TOP_K
    bm = BM_EXPERT
    n_blocks = n_assign // bm + N_EXPERTS
    n_slots = n_blocks * bm

    idx_t, gate_t, rank_t, counts_f = _router(x1, layer, wr_t, br_t)
    top_idx = idx_t[:TOP_K]
    gates_pad = jnp.pad(gate_t[:TOP_K].T, ((0, 0), (0, E_PAD - TOP_K)))

    counts = counts_f[:, 0].astype(jnp.int32)
    blocks_per_e = (counts + bm - 1) // bm
    blk_end = jnp.cumsum(blocks_per_e)
    row_start = (blk_end - blocks_per_e) * bm
    experts = jnp.arange(N_EXPERTS, dtype=jnp.int32)[:, None, None]
    start_of = jnp.sum(jnp.where(top_idx[None] == experts, row_start[:, None, None], 0), axis=0)
    dest_km = start_of + rank_t[:TOP_K]
    block_e = jnp.minimum(
        jnp.sum(blk_end[None, :] <= jnp.arange(n_blocks, dtype=jnp.int32)[:, None], axis=1),
        N_EXPERTS - 1).astype(jnp.int32)
    nb_used = blk_end[-1:].astype(jnp.int32)
    e_ids = jnp.arange(N_EXPERTS, dtype=jnp.int32)
    later_with_rows = jnp.where((e_ids[None, :] > e_ids[:, None]) & (counts[None, :] > 0),
                                e_ids[None, :], N_EXPERTS)
    next_e = jnp.min(later_with_rows, axis=1)
    next_e = jnp.where(next_e == N_EXPERTS, -1, next_e).astype(jnp.int32)

    tile = (ROW_WORDS_TILES, V7X_LANES)
    xs3 = _dispatch_rows(x1_rows.reshape(t_tok, *tile), dest_km, n_slots)
    ys_rows = _experts(block_e, nb_used, next_e,
                       xs3.reshape(n_slots * ROW_WORDS_TILES, V7X_LANES),
                       layer, w_gu, b_gu, w_down, b_down)
    yk3 = _gather_rows(ys_rows.reshape(n_slots, *tile), dest_km.reshape(n_assign))
    yk4 = yk3.reshape(TOP_K, t_tok * ROW_WORDS_TILES, V7X_LANES)
    return _combine(x1, yk4, gates_pad, layer, ln_g, ln_b)


def kernel(x, w_in, b_in, conv_a, w_out_a, w_pool, scale_pool, conv_c, conv_c_b, ln_c_g, ln_c_b,
           w_out_c, b_out_c, w_o, ln1_g, ln1_b, w_router, b_router, w_gu, b_gu, w_down, b_down,
           ln2_g, ln2_b):
    bsz, seq_len, d = x.shape
    assert d == D_MODEL
    t_tok = bsz * seq_len
    depth = w_in.shape[0]

    def row(v):
        return v[:, None, :]

    w_in_g = jnp.stack([w_in[:, :, c * D_MODEL:(c + 1) * D_MODEL].astype(BF16)
                        for c in MIX_COL_ORDER], axis=1)
    b_in_g = jnp.stack([b_in[:, None, c * D_MODEL:(c + 1) * D_MODEL]
                        for c in MIX_COL_ORDER], axis=1)
    mixer_consts = (w_in_g, b_in_g, conv_a, w_out_a.astype(BF16),
                    w_pool.astype(BF16), row(scale_pool), conv_c, row(conv_c_b), row(ln_c_g),
                    row(ln_c_b), w_out_c.astype(BF16), row(b_out_c), w_o.astype(BF16),
                    row(ln1_g), row(ln1_b))
    wr_t = jnp.transpose(w_router, (0, 2, 1)).astype(BF16)
    br_t = jnp.broadcast_to(b_router[:, :, None], (depth, N_EXPERTS, TR_ROUTE))
    n_le = depth * N_EXPERTS
    w_gu_f = w_gu.reshape(n_le, D_MODEL, 2 * D_FF)
    b_gu_d = jnp.concatenate([b_gu[..., 0::2], b_gu[..., 1::2]], axis=-1).reshape(n_le, 1, 2 * D_FF)
    w_down_f = w_down.reshape(n_le, D_FF, D_MODEL)
    b_down_d = b_down.reshape(n_le, 1, D_MODEL)
    ln2_g_r, ln2_b_r = row(ln2_g), row(ln2_b)

    x2 = x.reshape(t_tok, d)
    for layer in range(depth):
        x1, x1_rows = _mixer(x2, layer, mixer_consts, seq_len=seq_len)
        x2 = _moe(x1, x1_rows, layer, wr_t, br_t, w_gu_f, b_gu_d, w_down_f, b_down_d,
                  ln2_g_r, ln2_b_r)
    return x2.reshape(bsz, seq_len, d)
```

```python
import functools

import jax
import jax.numpy as jnp
from jax import lax
from jax.experimental import pallas as pl
from jax.experimental.pallas import tpu as pltpu
from jax.experimental.pallas import tpu_sc as plsc

D_MODEL = 1024
DEPTH = 4
CONV_A_WIDTH = 3
POOL_WINDOWS = (2, 4, 8, 16)
POOL_GROUP_DIM = D_MODEL // len(POOL_WINDOWS)
CONV_C_WIDTH = 31
N_EXPERTS = 32
TOP_K = 4
D_FF = D_MODEL
SWIGLU_LIMIT = 7.0
SWIGLU_ALPHA = 1.702
LN_EPS = 1e-5
DEEPNORM_ALPHA = (2.0 * DEPTH) ** 0.25

V7X_LANES = 128
V7X_SUBLANES = 8
V7X_VMEM_LIMIT_BYTES = 56 * 1024 * 1024
V7X_MXU_DIM = 256
V7X_SC_CORES = 2
V7X_SC_SUBCORES = 16
V7X_SC_WORKERS = V7X_SC_CORES * V7X_SC_SUBCORES

TM_MIX = 256
HALO_A = 8
HALO_P = 16
HALO_C = 32
CONV_LANE_CHUNK = 128
TAP_ROW_BLOCK = 64
CONV_C_ROW_BLOCK = 32
MIX_COL_ORDER = (4, 5, 1, 2, 0, 6, 3, 7, 8)
TR_ROUTE = 512
E_PAD = V7X_LANES
BM_EXPERT = 256
TM_COMB = 256
DEINTERLEAVE_COLS = V7X_MXU_DIM
ROW_WORDS_TILES = D_MODEL // 2 // V7X_LANES
SC_ROWS = 32
TOKEN_GROUPS = 2
NEG_BIG = -3.0e38

F32 = jnp.float32
BF16 = jnp.bfloat16


def _layer_norm(x, g, b):
    mu = jnp.mean(x, axis=-1, keepdims=True)
    xc = x - mu
    var = jnp.mean(xc * xc, axis=-1, keepdims=True)
    return xc * lax.rsqrt(var + LN_EPS) * g + b


def _store_token_rows(ref, val):
    rows = val.shape[0]
    words = pltpu.pack_elementwise([val[:, :D_MODEL // 2], val[:, D_MODEL // 2:]],
                                   packed_dtype=BF16)
    for g in range(ROW_WORDS_TILES):
        ref[pl.ds(g, rows, stride=ROW_WORDS_TILES), :] = words[:, g * V7X_LANES:(g + 1) * V7X_LANES]


def _load_token_rows(ref, rows):
    words = jnp.concatenate([ref[pl.ds(g, rows, stride=ROW_WORDS_TILES), :]
                             for g in range(ROW_WORDS_TILES)], axis=1)
    halves = [pltpu.unpack_elementwise(words, index=i, packed_dtype=BF16, unpacked_dtype=F32)
              for i in range(2)]
    return jnp.concatenate(halves, axis=1)


def _realign(ext_ref, shift_ref, halo, rows, lane0, lanes, max_shift):
    n = halo + rows - V7X_SUBLANES
    for b in range(1, min(max_shift, V7X_SUBLANES - 1) + 1):
        shift_ref[b - 1, 0:n, lane0:lane0 + lanes] = (
            ext_ref[V7X_SUBLANES - b:V7X_SUBLANES - b + n, lane0:lane0 + lanes])


def _causal_taps(ext_ref, shift_ref, halo, rows, lane0, lanes, weights):
    outs = []
    for r0 in range(0, rows, TAP_ROW_BLOCK):
        acc = None
        for j, w in enumerate(weights):
            a, b = divmod(j, V7X_SUBLANES)
            if b == 0:
                start = halo - V7X_SUBLANES * a + r0
                term = ext_ref[start:start + TAP_ROW_BLOCK, lane0:lane0 + lanes]
            else:
                start = halo - V7X_SUBLANES * (a + 1) + r0
                term = shift_ref[b - 1, start:start + TAP_ROW_BLOCK, lane0:lane0 + lanes]
            if w is not None:
                term = term * w
            acc = term if acc is None else acc + term
        outs.append(acc)
    return jnp.concatenate(outs, axis=0)


def _mixer_kernel(x_ref, w_in_ref, b_in_ref, conv_a_ref, w_out_a_ref, w_pool_ref, scale_pool_ref,
                  conv_c_ref, conv_c_b_ref, ln_c_g_ref, ln_c_b_ref, w_out_c_ref, b_out_c_ref,
                  w_o_ref, ln1_g_ref, ln1_b_ref,
                  y_ref, y_rows_ref,
                  xb_ref, ext_a, ext_p, ext_c, shift_ref, v_ref, z_ref, *, tiles_per_seq):
    tm = x_ref.shape[0]
    tile_in_seq = pl.program_id(0) % tiles_per_seq

    @pl.when(tile_in_seq == 0)
    def _():
        ext_a[0:HALO_A, :] = jnp.zeros((HALO_A, D_MODEL), F32)
        ext_p[0:HALO_P, :] = jnp.zeros((HALO_P, D_MODEL), F32)
        ext_c[0:HALO_C, :] = jnp.zeros((HALO_C, D_MODEL), F32)

    xb_ref[...] = x_ref[...].astype(BF16)

    def proj(slot):
        return (jnp.dot(xb_ref[...], w_in_ref[0, slot], preferred_element_type=F32)
                + b_in_ref[0, slot])

    ext_c[HALO_C:HALO_C + tm, :] = proj(0) * jax.nn.sigmoid(proj(1))
    _realign(ext_c, shift_ref, HALO_C, tm, 0, D_MODEL, CONV_C_WIDTH - 1)

    n_row_blocks = tm // CONV_C_ROW_BLOCK
    n_loop_proj = len(MIX_COL_ORDER) - 2
    assert n_loop_proj <= n_row_blocks
    taps_of_copy = {}
    for j in range(CONV_C_WIDTH):
        a, b = divmod(j, V7X_SUBLANES)
        taps_of_copy.setdefault(b, []).append((a, j))

    def aligned(row):
        return row if isinstance(row, int) else pl.multiple_of(row, V7X_SUBLANES)

    def conv_rows(r0, lane_lo=0, lane_hi=D_MODEL):
        for c0 in range(lane_lo, lane_hi, CONV_LANE_CHUNK):
            lanes = slice(c0, c0 + CONV_LANE_CHUNK)
            acc = conv_c_b_ref[0, :, lanes]
            for b, taps in taps_of_copy.items():
                a_max = max(a for a, _ in taps)
                first = HALO_C - V7X_SUBLANES * (a_max + (1 if b else 0))
                span = CONV_C_ROW_BLOCK + V7X_SUBLANES * a_max
                if b == 0:
                    win = ext_c[pl.ds(aligned(r0 + first), span), lanes]
                else:
                    win = shift_ref[b - 1, pl.ds(aligned(r0 + first), span), lanes]
                for a, j in taps:
                    off = V7X_SUBLANES * (a_max - a)
                    w = conv_c_ref[0, CONV_C_WIDTH - 1 - j:CONV_C_WIDTH - j, lanes]
                    acc = acc + win[off:off + CONV_C_ROW_BLOCK] * w
            v_ref[pl.ds(aligned(r0), CONV_C_ROW_BLOCK), lanes] = acc

    def conv_block(i, carry):
        conv_rows(i * CONV_C_ROW_BLOCK)
        return carry

    lax.fori_loop(0, n_row_blocks, conv_block, 0)
    ext_c[0:HALO_C, :] = ext_c[tm:tm + HALO_C, :]
    for i in range(n_loop_proj):
        z_ref[i] = proj(i + 2)

    ext_a[HALO_A:HALO_A + tm, :] = z_ref[0] * z_ref[1]
    b_a = z_ref[2]
    gate_a = jax.nn.sigmoid(z_ref[3])
    p_in = z_ref[4]
    ext_p[HALO_P:HALO_P + tm, :] = p_in
    gate_b = jax.nn.sigmoid(z_ref[5])
    gate_c = jax.nn.sigmoid(z_ref[6])

    parts = []
    for c0 in range(0, D_MODEL, CONV_LANE_CHUNK):
        w = [conv_a_ref[0, CONV_A_WIDTH - 1 - j:CONV_A_WIDTH - j, c0:c0 + CONV_LANE_CHUNK]
             for j in range(CONV_A_WIDTH)]
        _realign(ext_a, shift_ref,HALO_A, tm, c0, CONV_LANE_CHUNK, CONV_A_WIDTH - 1)
        parts.append(_causal_taps(ext_a, shift_ref,HALO_A, tm, c0, CONV_LANE_CHUNK, w))
    u_a = jnp.concatenate(parts, axis=1)
    ext_a[0:HALO_A, :] = ext_a[tm:tm + HALO_A, :]
    y_a = jnp.dot((b_a * u_a).astype(BF16), w_out_a_ref[0], preferred_element_type=F32)
    merged = gate_a * y_a

    v = _layer_norm(v_ref[...], ln_c_g_ref[0], ln_c_b_ref[0])
    v = v * jax.nn.sigmoid(v)
    y_c = jnp.dot(v.astype(BF16), w_out_c_ref[0], preferred_element_type=F32) + b_out_c_ref[0]
    merged = merged + gate_c * y_c

    pos = tile_in_seq * tm + lax.broadcasted_iota(jnp.int32, (tm, POOL_GROUP_DIM), 0)
    parts = []
    for g, win in enumerate(POOL_WINDOWS):
        lo = g * POOL_GROUP_DIM
        sub = []
        for c0 in range(lo, lo + POOL_GROUP_DIM, CONV_LANE_CHUNK):
            _realign(ext_p, shift_ref,HALO_P, tm, c0, CONV_LANE_CHUNK, win - 1)
            sub.append(_causal_taps(ext_p, shift_ref,HALO_P, tm, c0, CONV_LANE_CHUNK,
                                    [None] * win))
        wsum = jnp.concatenate(sub, axis=1)
        cnt = jnp.minimum(pos + 1, win).astype(F32)
        pooled = wsum / cnt - p_in[:, lo:lo + POOL_GROUP_DIM]
        parts.append(jnp.dot(pooled.astype(BF16), w_pool_ref[0, g], preferred_element_type=F32))
    ext_p[0:HALO_P, :] = ext_p[tm:tm + HALO_P, :]
    y_b = jnp.concatenate(parts, axis=1) * scale_pool_ref[0]
    merged = merged + gate_b * y_b

    h = jnp.dot(merged.astype(BF16), w_o_ref[0], preferred_element_type=F32)
    y = _layer_norm(DEEPNORM_ALPHA * x_ref[...] + h, ln1_g_ref[0], ln1_b_ref[0])
    y_ref[...] = y
    _store_token_rows(y_rows_ref, y)


def _layer_block(arr, layer, buffered_once=False):
    tail = (0,) * (arr.ndim - 1)
    mode = {"pipeline_mode": pl.Buffered(1)} if buffered_once else {}
    return pl.BlockSpec((1,) + arr.shape[1:], lambda *_: (layer,) + tail, **mode)


def _mixer(x2, layer, consts, *, seq_len):
    t_tok = x2.shape[0]
    tm = TM_MIX
    assert seq_len % tm == 0 and t_tok % seq_len == 0
    return pl.pallas_call(
        functools.partial(_mixer_kernel, tiles_per_seq=seq_len // tm),
        out_shape=(jax.ShapeDtypeStruct((t_tok, D_MODEL), F32),
                   jax.ShapeDtypeStruct((t_tok * ROW_WORDS_TILES, V7X_LANES), jnp.int32)),
        grid=(t_tok // tm,),
        in_specs=[pl.BlockSpec((tm, D_MODEL), lambda i: (i, 0))]
                 + [_layer_block(c, layer, buffered_once=True) for c in consts],
        out_specs=(pl.BlockSpec((tm, D_MODEL), lambda i: (i, 0)),
                   pl.BlockSpec((tm * ROW_WORDS_TILES, V7X_LANES), lambda i: (i, 0))),
        scratch_shapes=[pltpu.VMEM((tm, D_MODEL), BF16),
                        pltpu.VMEM((HALO_A + tm, D_MODEL), F32),
                        pltpu.VMEM((HALO_P + tm, D_MODEL), F32),
                        pltpu.VMEM((HALO_C + tm, D_MODEL), F32),
                        pltpu.VMEM((V7X_SUBLANES - 1, HALO_C + tm - V7X_SUBLANES, D_MODEL), F32),
                        pltpu.VMEM((tm, D_MODEL), F32),
                        pltpu.VMEM((len(MIX_COL_ORDER) - 2, tm, D_MODEL), F32)],
        compiler_params=pltpu.CompilerParams(dimension_semantics=("arbitrary",),
                                             vmem_limit_bytes=V7X_VMEM_LIMIT_BYTES),
        name="mixer",
    )(x2, *consts)


def _router_kernel(x_ref, wr_ref, br_ref, idx_ref, gate_ref, rank_ref, counts_ref, run_ref):
    tr = x_ref.shape[0]

    @pl.when(pl.program_id(0) == 0)
    def _():
        run_ref[...] = jnp.zeros(run_ref.shape, F32)

    logits = lax.dot_general(wr_ref[0], x_ref[...].astype(BF16), (((1,), (1,)), ((), ())),
                             preferred_element_type=F32) + br_ref[0]
    expert = lax.broadcasted_iota(jnp.int32, (N_EXPERTS, tr), 0)
    work = logits
    vals, idxs = [], []
    for _ in range(TOP_K):
        m = jnp.max(work, axis=0, keepdims=True)
        idx = jnp.min(jnp.where(work == m, expert, N_EXPERTS), axis=0, keepdims=True)
        vals.append(m)
        idxs.append(idx)
        work = jnp.where(expert == idx, NEG_BIG, work)
    exps = [jnp.exp(v - vals[0]) for v in vals]
    denom = exps[0] + exps[1] + exps[2] + exps[3]

    sel = jnp.zeros((N_EXPERTS, tr), F32)
    for k in range(TOP_K):
        sel = sel + jnp.where(expert == idxs[k], 1.0, 0.0)
    earlier = lax.broadcasted_iota(jnp.int32, (tr, tr), 0)
    later = lax.broadcasted_iota(jnp.int32, (tr, tr), 1)
    before = jnp.where(earlier < later, 1.0, 0.0).astype(BF16)
    prior = jnp.dot(sel.astype(BF16), before, preferred_element_type=F32) + run_ref[:, 0:1]

    row = lax.broadcasted_iota(jnp.int32, (V7X_SUBLANES, tr), 0)
    idx_out = jnp.zeros((V7X_SUBLANES, tr), jnp.int32)
    gate_out = jnp.zeros((V7X_SUBLANES, tr), F32)
    rank_out = jnp.zeros((V7X_SUBLANES, tr), F32)
    for k in range(TOP_K):
        rank_k = jnp.sum(jnp.where(expert == idxs[k], prior, 0.0), axis=0, keepdims=True)
        idx_out = jnp.where(row == k, idxs[k], idx_out)
        gate_out = jnp.where(row == k, exps[k] / denom, gate_out)
        rank_out = jnp.where(row == k, rank_k, rank_out)
    idx_ref[...] = idx_out
    gate_ref[...] = gate_out
    rank_ref[...] = rank_out.astype(jnp.int32)

    run_ref[...] = run_ref[...] + jnp.sum(sel, axis=1, keepdims=True)
    counts_ref[...] = run_ref[...]


def _router(x2, layer, wr_t, br_t):
    t_tok = x2.shape[0]
    tr = TR_ROUTE
    tok_spec = pl.BlockSpec((V7X_SUBLANES, tr), lambda i: (0, i))
    return pl.pallas_call(
        _router_kernel,
        out_shape=(jax.ShapeDtypeStruct((V7X_SUBLANES, t_tok), jnp.int32),
                   jax.ShapeDtypeStruct((V7X_SUBLANES, t_tok), F32),
                   jax.ShapeDtypeStruct((V7X_SUBLANES, t_tok), jnp.int32),
                   jax.ShapeDtypeStruct((N_EXPERTS, V7X_LANES), F32)),
        grid=(t_tok // tr,),
        in_specs=[pl.BlockSpec((tr, D_MODEL), lambda i: (i, 0)),
                  _layer_block(wr_t, layer), _layer_block(br_t, layer)],
        out_specs=(tok_spec, tok_spec, tok_spec,
                   pl.BlockSpec((N_EXPERTS, V7X_LANES), lambda i: (0, 0))),
        scratch_shapes=[pltpu.VMEM((N_EXPERTS, V7X_LANES), F32)],
        compiler_params=pltpu.CompilerParams(dimension_semantics=("arbitrary",)),
        name="router",
    )(x2, wr_t, br_t)


def _sc_mesh():
    return plsc.VectorSubcoreMesh(core_axis_name="c", subcore_axis_name="s")


def _sc_worker_id():
    return lax.axis_index("s") * V7X_SC_CORES + lax.axis_index("c")


def _dispatch_rows(x3, dest_km, n_slots):
    t_tok = x3.shape[0]
    tok_per_worker = t_tok // V7X_SC_WORKERS
    assert tok_per_worker % SC_ROWS == 0

    def body(x_hbm, dest_hbm, out_hbm, rows_v, idx_v, sem):
        base = _sc_worker_id() * tok_per_worker

        @pl.loop(0, tok_per_worker // SC_ROWS)
        def _(step):
            t0 = pl.multiple_of(base + step * SC_ROWS, SC_ROWS)
            pltpu.sync_copy(x_hbm.at[pl.ds(t0, SC_ROWS)], rows_v)
            for k in range(TOP_K):
                pltpu.sync_copy(dest_hbm.at[k, pl.ds(t0, SC_ROWS)], idx_v.at[k])
            for k in range(TOP_K):
                pltpu.async_copy(rows_v, out_hbm.at[idx_v.at[k]], sem).wait()

    return pl.kernel(
        body, mesh=_sc_mesh(),
        out_type=jax.ShapeDtypeStruct((n_slots,) + x3.shape[1:], x3.dtype),
        scratch_types=[pltpu.VMEM((SC_ROWS,) + x3.shape[1:], x3.dtype),
                       pltpu.VMEM((TOP_K, SC_ROWS), jnp.int32),
                       pltpu.SemaphoreType.DMA],
    )(x3, dest_km)


def _gather_rows(ys3, src_rows):
    n = src_rows.shape[0]
    rows_per_worker = n // V7X_SC_WORKERS
    assert rows_per_worker % SC_ROWS == 0

    def body(ys_hbm, src_hbm, out_hbm, rows_v, idx_v, sem):
        base = _sc_worker_id() * rows_per_worker

        @pl.loop(0, rows_per_worker // SC_ROWS)
        def _(step):
            a0 = pl.multiple_of(base + step * SC_ROWS, SC_ROWS)
            pltpu.sync_copy(src_hbm.at[pl.ds(a0, SC_ROWS)], idx_v)
            pltpu.async_copy(ys_hbm.at[idx_v], rows_v, sem).wait()
            pltpu.sync_copy(rows_v, out_hbm.at[pl.ds(a0, SC_ROWS)])

    return pl.kernel(
        body, mesh=_sc_mesh(),
        out_type=jax.ShapeDtypeStruct((n,) + ys3.shape[1:], ys3.dtype),
        scratch_types=[pltpu.VMEM((SC_ROWS,) + ys3.shape[1:], ys3.dtype),
                       pltpu.VMEM((SC_ROWS,), jnp.int32),
                       pltpu.SemaphoreType.DMA],
    )(ys3, src_rows)


def _expert_kernel(block_e_ref, nb_used_ref, next_e_ref, xs_ref, w_gu_hbm, b_gu_ref, w_down_hbm,
                   b_down_ref, ys_ref, w_gu_f, w_down_f, w_gu_b, w_down_b, sem_gu, sem_down,
                   slot_ref, *, layer):
    bm = BM_EXPERT
    b = pl.program_id(0)
    e = block_e_ref[b]
    live = b < nb_used_ref[0]
    new_expert = jnp.logical_or(b == 0, e != block_e_ref[jnp.maximum(b - 1, 0)])

    def weight_copies(expert, slot):
        idx = layer * N_EXPERTS + expert
        return (pltpu.make_async_copy(w_gu_hbm.at[idx], w_gu_f.at[slot], sem_gu.at[slot]),
                pltpu.make_async_copy(w_down_hbm.at[idx], w_down_f.at[slot], sem_down.at[slot]))

    @pl.when(b == 0)
    def _():
        slot_ref[0] = 0
        for copy in weight_copies(e, 0):
            copy.start()

    @pl.when(jnp.logical_and(new_expert, live))
    def _():
        slot = slot_ref[0]
        nxt = next_e_ref[e]

        @pl.when(nxt >= 0)
        def _():
            for copy in weight_copies(nxt, 1 - slot):
                copy.start()

        for copy in weight_copies(e, slot):
            copy.wait()
        slot_ref[0] = 1 - slot

        cb = DEINTERLEAVE_COLS
        src = lax.broadcasted_iota(jnp.int32, (cb, cb), 0)
        dst = lax.broadcasted_iota(jnp.int32, (cb, cb), 1)
        pick = jnp.where(dst < cb // 2, 2 * dst, 2 * (dst - cb // 2) + 1)
        sel = jnp.where(src == pick, 1.0, 0.0).astype(BF16)
        for c in range(2 * D_FF // cb):
            blk = jnp.dot(w_gu_f[slot, :, c * cb:(c + 1) * cb].astype(BF16), sel,
                          preferred_element_type=F32).astype(BF16)
            lo = c * (cb // 2)
            w_gu_b[:, lo:lo + cb // 2] = blk[:, :cb // 2]
            w_gu_b[:, D_FF + lo:D_FF + lo + cb // 2] = blk[:, cb // 2:]
        w_down_b[...] = w_down_f[slot].astype(BF16)

    @pl.when(live)
    def _():
        xb = _load_token_rows(xs_ref, bm).astype(BF16)
        h = jnp.dot(xb, w_gu_b[...], preferred_element_type=F32) + b_gu_ref[0]
        gate = jnp.minimum(h[:, :D_FF], SWIGLU_LIMIT)
        up = jnp.clip(h[:, D_FF:], -SWIGLU_LIMIT, SWIGLU_LIMIT)
        act = (up + 1.0) * (gate * jax.nn.sigmoid(SWIGLU_ALPHA * gate))
        y = jnp.dot(act.astype(BF16), w_down_b[...], preferred_element_type=F32) + b_down_ref[0]
        _store_token_rows(ys_ref, y)

    @pl.when(jnp.logical_not(live))
    def _():
        ys_ref[...] = jnp.zeros(ys_ref.shape, ys_ref.dtype)


def _experts(block_e, nb_used, next_e, xs_rows, layer, w_gu, b_gu, w_down, b_down):
    bm = BM_EXPERT
    n_blocks = xs_rows.shape[0] // (bm * ROW_WORDS_TILES)

    def x_map(b, be, nb, nx):
        return (jnp.minimum(b, nb[0] - 1), 0)

    def e_map(b, be, nb, nx):
        return (layer * N_EXPERTS + be[b], 0, 0)

    row_block = (bm * ROW_WORDS_TILES, V7X_LANES)
    grid_spec = pltpu.PrefetchScalarGridSpec(
        num_scalar_prefetch=3,
        grid=(n_blocks,),
        in_specs=[pl.BlockSpec(row_block, x_map),
                  pl.BlockSpec(memory_space=pl.ANY),
                  pl.BlockSpec((1, 1, 2 * D_FF), e_map),
                  pl.BlockSpec(memory_space=pl.ANY),
                  pl.BlockSpec((1, 1, D_MODEL), e_map)],
        out_specs=pl.BlockSpec(row_block, lambda b, be, nb, nx: (b, 0)),
        scratch_shapes=[pltpu.VMEM((2, D_MODEL, 2 * D_FF), F32),
                        pltpu.VMEM((2, D_FF, D_MODEL), F32),
                        pltpu.VMEM((D_MODEL, 2 * D_FF), BF16),
                        pltpu.VMEM((D_FF, D_MODEL), BF16),
                        pltpu.SemaphoreType.DMA((2,)),
                        pltpu.SemaphoreType.DMA((2,)),
                        pltpu.SMEM((1,), jnp.int32)],
    )
    return pl.pallas_call(
        functools.partial(_expert_kernel, layer=layer),
        out_shape=jax.ShapeDtypeStruct(xs_rows.shape, xs_rows.dtype),
        grid_spec=grid_spec,
        compiler_params=pltpu.CompilerParams(dimension_semantics=("arbitrary",),
                                             vmem_limit_bytes=V7X_VMEM_LIMIT_BYTES),
        name="experts",
    )(block_e, nb_used, next_e, xs_rows, w_gu, b_gu, w_down, b_down)


def _combine_kernel(x_ref, yk_ref, gate_ref, g_ref, b_ref, o_ref):
    tm = x_ref.shape[0]
    gates = gate_ref[...]
    m = jnp.zeros((tm, D_MODEL), F32)
    for k in range(TOP_K):
        m = m + gates[:, k:k + 1] * _load_token_rows(yk_ref.at[k], tm)
    o_ref[...] = _layer_norm(DEEPNORM_ALPHA * x_ref[...] + m, g_ref[0], b_ref[0])


def _combine(x2, yk4, gates_pad, layer, ln_g, ln_b):
    t_tok = x2.shape[0]
    tm = TM_COMB
    return pl.pallas_call(
        _combine_kernel,
        out_shape=jax.ShapeDtypeStruct((t_tok, D_MODEL), F32),
        grid=(t_tok // tm,),
        in_specs=[pl.BlockSpec((tm, D_MODEL), lambda i: (i, 0)),
                  pl.BlockSpec((TOP_K, tm * ROW_WORDS_TILES, V7X_LANES), lambda i: (0, i, 0)),
                  pl.BlockSpec((tm, E_PAD), lambda i: (i, 0)),
                  _layer_block(ln_g, layer), _layer_block(ln_b, layer)],
        out_specs=pl.BlockSpec((tm, D_MODEL), lambda i: (i, 0)),
        compiler_params=pltpu.CompilerParams(dimension_semantics=("arbitrary",)),
        name="combine",
    )(x2, yk4, gates_pad, ln_g, ln_b)


def _moe(x1, x1_rows, layer, wr_t, br_t, w_gu, b_gu, w_down, b_down, ln_g, ln_b):
    t_tok = x1.shape[0]
    n_assign = t_tok * TOP_K
    bm = BM_EXPERT
    n_blocks = n_assign // bm + N_EXPERTS
    n_slots = n_blocks * bm

    idx_t, gate_t, rank_t, counts_f = _router(x1, layer, wr_t, br_t)
    top_idx = idx_t[:TOP_K]
    gates_pad = jnp.pad(gate_t[:TOP_K].T, ((0, 0), (0, E_PAD - TOP_K)))

    counts = counts_f[:, 0].astype(jnp.int32)
    blocks_per_e = (counts + bm - 1) // bm
    blk_end = jnp.cumsum(blocks_per_e)
    row_start = (blk_end - blocks_per_e) * bm
    experts = jnp.arange(N_EXPERTS, dtype=jnp.int32)[:, None, None]
    start_of = jnp.sum(jnp.where(top_idx[None] == experts, row_start[:, None, None], 0), axis=0)
    dest_km = start_of + rank_t[:TOP_K]
    block_e = jnp.minimum(
        jnp.sum(blk_end[None, :] <= jnp.arange(n_blocks, dtype=jnp.int32)[:, None], axis=1),
        N_EXPERTS - 1).astype(jnp.int32)
    nb_used = blk_end[-1:].astype(jnp.int32)
    e_ids = jnp.arange(N_EXPERTS, dtype=jnp.int32)
    later_with_rows = jnp.where((e_ids[None, :] > e_ids[:, None]) & (counts[None, :] > 0),
                                e_ids[None, :], N_EXPERTS)
    next_e = jnp.min(later_with_rows, axis=1)
    next_e = jnp.where(next_e == N_EXPERTS, -1, next_e).astype(jnp.int32)

    tile = (ROW_WORDS_TILES, V7X_LANES)
    xs3 = _dispatch_rows(x1_rows.reshape(t_tok, *tile), dest_km, n_slots)
    ys_rows = _experts(block_e, nb_used, next_e,
                       xs3.reshape(n_slots * ROW_WORDS_TILES, V7X_LANES),
                       layer, w_gu, b_gu, w_down, b_down)
    yk3 = _gather_rows(ys_rows.reshape(n_slots, *tile), dest_km.reshape(n_assign))
    yk4 = yk3.reshape(TOP_K, t_tok * ROW_WORDS_TILES, V7X_LANES)
    return _combine(x1, yk4, gates_pad, layer, ln_g, ln_b)


def kernel(x, w_in, b_in, conv_a, w_out_a, w_pool, scale_pool, conv_c, conv_c_b, ln_c_g, ln_c_b,
           w_out_c, b_out_c, w_o, ln1_g, ln1_b, w_router, b_router, w_gu, b_gu, w_down, b_down,
           ln2_g, ln2_b):
    bsz, seq_len, d = x.shape
    assert d == D_MODEL
    t_tok = bsz * seq_len
    depth = w_in.shape[0]

    def row(v):
        return v[:, None, :]

    w_in_g = jnp.stack([w_in[:, :, c * D_MODEL:(c + 1) * D_MODEL].astype(BF16)
                        for c in MIX_COL_ORDER], axis=1)
    b_in_g = jnp.stack([b_in[:, None, c * D_MODEL:(c + 1) * D_MODEL]
                        for c in MIX_COL_ORDER], axis=1)
    mixer_consts = (w_in_g, b_in_g, conv_a, w_out_a.astype(BF16),
                    w_pool.astype(BF16), row(scale_pool), conv_c, row(conv_c_b), row(ln_c_g),
                    row(ln_c_b), w_out_c.astype(BF16), row(b_out_c), w_o.astype(BF16),
                    row(ln1_g), row(ln1_b))
    wr_t = jnp.transpose(w_router, (0, 2, 1)).astype(BF16)
    br_t = jnp.broadcast_to(b_router[:, :, None], (depth, N_EXPERTS, TR_ROUTE))
    n_le = depth * N_EXPERTS
    w_gu_f = w_gu.reshape(n_le, D_MODEL, 2 * D_FF)
    b_gu_d = jnp.concatenate([b_gu[..., 0::2], b_gu[..., 1::2]], axis=-1).reshape(n_le, 1, 2 * D_FF)
    w_down_f = w_down.reshape(n_le, D_FF, D_MODEL)
    b_down_d = b_down.reshape(n_le, 1, D_MODEL)
    ln2_g_r, ln2_b_r = row(ln2_g), row(ln2_b)

    assert bsz % TOKEN_GROUPS == 0
    per_group = t_tok // TOKEN_GROUPS
    x2 = x.reshape(t_tok, d)
    groups = [x2[g * per_group:(g + 1) * per_group] for g in range(TOKEN_GROUPS)]
    for layer in range(depth):
        mixed = [_mixer(xg, layer, mixer_consts, seq_len=seq_len) for xg in groups]
        groups = [_moe(x1, x1_rows, layer, wr_t, br_t, w_gu_f, b_gu_d, w_down_f, b_down_d,
                       ln2_g_r, ln2_b_r) for x1, x1_rows in mixed]
    return jnp.concatenate(groups, axis=0).reshape(bsz, seq_len, d)
```

```python
import functools

import jax
import jax.numpy as jnp
from jax import lax
from jax.experimental import pallas as pl
from jax.experimental.pallas import tpu as pltpu
from jax.experimental.pallas import tpu_sc as plsc

D_MODEL = 1024
DEPTH = 4
CONV_A_WIDTH = 3
POOL_WINDOWS = (2, 4, 8, 16)
POOL_GROUP_DIM = D_MODEL // len(POOL_WINDOWS)
CONV_C_WIDTH = 31
N_EXPERTS = 32
TOP_K = 4
D_FF = D_MODEL
SWIGLU_LIMIT = 7.0
SWIGLU_ALPHA = 1.702
LN_EPS = 1e-5
DEEPNORM_ALPHA = (2.0 * DEPTH) ** 0.25

V7X_LANES = 128
V7X_SUBLANES = 8
V7X_VMEM_LIMIT_BYTES = 56 * 1024 * 1024
V7X_MXU_DIM = 256
V7X_SC_CORES = 2
V7X_SC_SUBCORES = 16
V7X_SC_WORKERS = V7X_SC_CORES * V7X_SC_SUBCORES

TM_MIX = 256
HALO_A = 8
HALO_P = 16
HALO_C = 32
CONV_LANE_CHUNK = 128
TAP_ROW_BLOCK = 64
CONV_C_ROW_BLOCK = 32
MIX_COL_ORDER = (4, 5, 1, 2, 0, 6, 3, 7, 8)
TR_ROUTE = 512
GATE_LANES = V7X_LANES
BM_EXPERT = 256
TM_COMB = 256
DEINTERLEAVE_COLS = V7X_MXU_DIM
ROW_WORDS_TILES = D_MODEL // 2 // V7X_LANES
SC_ROWS = 32
TOKEN_GROUPS = 2
NEG_BIG = -3.0e38

F32 = jnp.float32
BF16 = jnp.bfloat16


def _layer_norm(x, g, b):
    mu = jnp.mean(x, axis=-1, keepdims=True)
    xc = x - mu
    var = jnp.mean(xc * xc, axis=-1, keepdims=True)
    return xc * lax.rsqrt(var + LN_EPS) * g + b


def _store_token_rows(ref, val):
    rows = val.shape[0]
    words = pltpu.pack_elementwise([val[:, :D_MODEL // 2], val[:, D_MODEL // 2:]],
                                   packed_dtype=BF16)
    for g in range(ROW_WORDS_TILES):
        ref[pl.ds(g, rows, stride=ROW_WORDS_TILES), :] = words[:, g * V7X_LANES:(g + 1) * V7X_LANES]


def _load_token_rows(ref, rows):
    words = jnp.concatenate([ref[pl.ds(g, rows, stride=ROW_WORDS_TILES), :]
                             for g in range(ROW_WORDS_TILES)], axis=1)
    halves = [pltpu.unpack_elementwise(words, index=i, packed_dtype=BF16, unpacked_dtype=F32)
              for i in range(2)]
    return jnp.concatenate(halves, axis=1)


def _combine_rows(x1, yk_ref, gates, ln_g, ln_b):
    rows = x1.shape[0]
    m = jnp.zeros((rows, D_MODEL), F32)
    for k in range(TOP_K):
        m = m + gates[:, k:k + 1] * _load_token_rows(yk_ref.at[k], rows)
    return _layer_norm(DEEPNORM_ALPHA * x1 + m, ln_g, ln_b)


def _realign(ext_ref, shift_ref, halo, rows, lane0, lanes, max_shift):
    n = halo + rows - V7X_SUBLANES
    for b in range(1, min(max_shift, V7X_SUBLANES - 1) + 1):
        shift_ref[b - 1, 0:n, lane0:lane0 + lanes] = (
            ext_ref[V7X_SUBLANES - b:V7X_SUBLANES - b + n, lane0:lane0 + lanes])


def _causal_taps(ext_ref, shift_ref, halo, rows, lane0, lanes, weights):
    outs = []
    for r0 in range(0, rows, TAP_ROW_BLOCK):
        acc = None
        for j, w in enumerate(weights):
            a, b = divmod(j, V7X_SUBLANES)
            if b == 0:
                start = halo - V7X_SUBLANES * a + r0
                term = ext_ref[start:start + TAP_ROW_BLOCK, lane0:lane0 + lanes]
            else:
                start = halo - V7X_SUBLANES * (a + 1) + r0
                term = shift_ref[b - 1, start:start + TAP_ROW_BLOCK, lane0:lane0 + lanes]
            if w is not None:
                term = term * w
            acc = term if acc is None else acc + term
        outs.append(acc)
    return jnp.concatenate(outs, axis=0)


N_MIXER_CONSTS = 16


def _mixer_kernel(*refs, tiles_per_seq, fused_combine):
    n_in = 5 if fused_combine else 1
    inputs, refs = refs[:n_in], refs[n_in:]
    (w_in_ref, b_in_ref, conv_a_ref, w_out_a_ref, w_pool_ref, scale_pool_ref, conv_c_ref,
     conv_c_b_ref, ln_c_g_ref, ln_c_b_ref, w_out_c_ref, b_out_c_ref, w_o_ref, ln1_g_ref,
     ln1_b_ref, wr_ref) = refs[:N_MIXER_CONSTS]
    (y_ref, y_rows_ref, logits_ref,
     x_f32, xb_ref, ext_a, ext_p, ext_c, shift_ref, v_ref) = refs[N_MIXER_CONSTS:]
    tm = y_ref.shape[0]
    tile_in_seq = pl.program_id(0) % tiles_per_seq

    @pl.when(tile_in_seq == 0)
    def _():
        ext_a[0:HALO_A, :] = jnp.zeros((HALO_A, D_MODEL), F32)
        ext_p[0:HALO_P, :] = jnp.zeros((HALO_P, D_MODEL), F32)
        ext_c[0:HALO_C, :] = jnp.zeros((HALO_C, D_MODEL), F32)

    if fused_combine:
        x_prev_ref, yk_ref, gate_ref, ln2_g_ref, ln2_b_ref = inputs
        x_f32[...] = _combine_rows(x_prev_ref[...], yk_ref, gate_ref[...], ln2_g_ref[0],
                                   ln2_b_ref[0])
    else:
        x_f32[...] = inputs[0][...]
    xb_ref[...] = x_f32[...].astype(BF16)

    def proj(slot):
        return (jnp.dot(xb_ref[...], w_in_ref[0, slot], preferred_element_type=F32)
                + b_in_ref[0, slot])

    ext_c[HALO_C:HALO_C + tm, :] = proj(0) * jax.nn.sigmoid(proj(1))
    _realign(ext_c, shift_ref, HALO_C, tm, 0, D_MODEL, CONV_C_WIDTH - 1)
    taps_of_copy = {}
    for j in range(CONV_C_WIDTH):
        a, b = divmod(j, V7X_SUBLANES)
        taps_of_copy.setdefault(b, []).append((a, j))

    def conv_block(i, carry):
        r0 = i * CONV_C_ROW_BLOCK
        for c0 in range(0, D_MODEL, CONV_LANE_CHUNK):
            lanes = slice(c0, c0 + CONV_LANE_CHUNK)
            acc = conv_c_b_ref[0, :, lanes]
            for b, taps in taps_of_copy.items():
                a_max = max(a for a, _ in taps)
                first = HALO_C - V7X_SUBLANES * (a_max + (1 if b else 0))
                start = pl.multiple_of(r0 + first, V7X_SUBLANES)
                span = CONV_C_ROW_BLOCK + V7X_SUBLANES * a_max
                if b == 0:
                    win = ext_c[pl.ds(start, span), lanes]
                else:
                    win = shift_ref[b - 1, pl.ds(start, span), lanes]
                for a, j in taps:
                    off = V7X_SUBLANES * (a_max - a)
                    w = conv_c_ref[0, CONV_C_WIDTH - 1 - j:CONV_C_WIDTH - j, lanes]
                    acc = acc + win[off:off + CONV_C_ROW_BLOCK] * w
            v_ref[pl.ds(pl.multiple_of(r0, CONV_C_ROW_BLOCK), CONV_C_ROW_BLOCK), lanes] = acc
        return carry

    lax.fori_loop(0, tm // CONV_C_ROW_BLOCK, conv_block, 0)
    ext_c[0:HALO_C, :] = ext_c[tm:tm + HALO_C, :]

    ext_a[HALO_A:HALO_A + tm, :] = proj(2) * proj(3)
    b_a = proj(4)
    gate_a = jax.nn.sigmoid(proj(5))
    parts = []
    for c0 in range(0, D_MODEL, CONV_LANE_CHUNK):
        w = [conv_a_ref[0, CONV_A_WIDTH - 1 - j:CONV_A_WIDTH - j, c0:c0 + CONV_LANE_CHUNK]
             for j in range(CONV_A_WIDTH)]
        _realign(ext_a, shift_ref, HALO_A, tm, c0, CONV_LANE_CHUNK, CONV_A_WIDTH - 1)
        parts.append(_causal_taps(ext_a, shift_ref, HALO_A, tm, c0, CONV_LANE_CHUNK, w))
    u_a = jnp.concatenate(parts, axis=1)
    ext_a[0:HALO_A, :] = ext_a[tm:tm + HALO_A, :]
    y_a = jnp.dot((b_a * u_a).astype(BF16), w_out_a_ref[0], preferred_element_type=F32)
    merged = gate_a * y_a

    v = _layer_norm(v_ref[...], ln_c_g_ref[0], ln_c_b_ref[0])
    v = v * jax.nn.sigmoid(v)
    y_c = jnp.dot(v.astype(BF16), w_out_c_ref[0], preferred_element_type=F32) + b_out_c_ref[0]
    merged = merged + jax.nn.sigmoid(proj(8)) * y_c

    p_in = proj(6)
    ext_p[HALO_P:HALO_P + tm, :] = p_in
    pos = tile_in_seq * tm + lax.broadcasted_iota(jnp.int32, (tm, POOL_GROUP_DIM), 0)
    parts = []
    for g, win in enumerate(POOL_WINDOWS):
        lo = g * POOL_GROUP_DIM
        sub = []
        for c0 in range(lo, lo + POOL_GROUP_DIM, CONV_LANE_CHUNK):
            _realign(ext_p, shift_ref, HALO_P, tm, c0, CONV_LANE_CHUNK, win - 1)
            sub.append(_causal_taps(ext_p, shift_ref, HALO_P, tm, c0, CONV_LANE_CHUNK,
                                    [None] * win))
        wsum = jnp.concatenate(sub, axis=1)
        cnt = jnp.minimum(pos + 1, win).astype(F32)
        pooled = wsum / cnt - p_in[:, lo:lo + POOL_GROUP_DIM]
        parts.append(jnp.dot(pooled.astype(BF16), w_pool_ref[0, g], preferred_element_type=F32))
    ext_p[0:HALO_P, :] = ext_p[tm:tm + HALO_P, :]
    y_b = jnp.concatenate(parts, axis=1) * scale_pool_ref[0]
    merged = merged + jax.nn.sigmoid(proj(7)) * y_b

    h = jnp.dot(merged.astype(BF16), w_o_ref[0], preferred_element_type=F32)
    y = _layer_norm(DEEPNORM_ALPHA * x_f32[...] + h, ln1_g_ref[0], ln1_b_ref[0])
    y_ref[...] = y
    _store_token_rows(y_rows_ref, y)
    logits_ref[...] = lax.dot_general(wr_ref[0], y.astype(BF16), (((1,), (1,)), ((), ())),
                                      preferred_element_type=F32)


def _layer_block(arr, layer, buffered_once=False):
    tail = (0,) * (arr.ndim - 1)
    mode = {"pipeline_mode": pl.Buffered(1)} if buffered_once else {}
    return pl.BlockSpec((1,) + arr.shape[1:], lambda *_: (layer,) + tail, **mode)


def _mixer(inputs, layer, consts, *, seq_len, n_tok, row0=0, combine_params=None):
    tm = TM_MIX
    assert seq_len % tm == 0 and n_tok % seq_len == 0 and row0 % tm == 0
    assert len(consts) == N_MIXER_CONSTS
    tile0 = row0 // tm
    tok_spec = pl.BlockSpec((tm, D_MODEL), lambda i: (i + tile0, 0))
    if combine_params is None:
        in_specs, args = [tok_spec], list(inputs)
    else:
        ln_g, ln_b, prev_layer = combine_params
        in_specs = [tok_spec,
                    pl.BlockSpec((TOP_K, tm * ROW_WORDS_TILES, V7X_LANES), lambda i: (0, i, 0)),
                    pl.BlockSpec((tm, GATE_LANES), lambda i: (i, 0)),
                    _layer_block(ln_g, prev_layer), _layer_block(ln_b, prev_layer)]
        args = list(inputs) + [ln_g, ln_b]
    return pl.pallas_call(
        functools.partial(_mixer_kernel, tiles_per_seq=seq_len // tm,
                          fused_combine=combine_params is not None),
        out_shape=(jax.ShapeDtypeStruct((n_tok, D_MODEL), F32),
                   jax.ShapeDtypeStruct((n_tok * ROW_WORDS_TILES, V7X_LANES), jnp.int32),
                   jax.ShapeDtypeStruct((N_EXPERTS, n_tok), F32)),
        grid=(n_tok // tm,),
        in_specs=in_specs + [_layer_block(c, layer, buffered_once=True) for c in consts],
        out_specs=(pl.BlockSpec((tm, D_MODEL), lambda i: (i, 0)),
                   pl.BlockSpec((tm * ROW_WORDS_TILES, V7X_LANES), lambda i: (i, 0)),
                   pl.BlockSpec((N_EXPERTS, tm), lambda i: (0, i))),
        scratch_shapes=[pltpu.VMEM((tm, D_MODEL), F32),
                        pltpu.VMEM((tm, D_MODEL), BF16),
                        pltpu.VMEM((HALO_A + tm, D_MODEL), F32),
                        pltpu.VMEM((HALO_P + tm, D_MODEL), F32),
                        pltpu.VMEM((HALO_C + tm, D_MODEL), F32),
                        pltpu.VMEM((V7X_SUBLANES - 1, HALO_C + tm - V7X_SUBLANES, D_MODEL), F32),
                        pltpu.VMEM((tm, D_MODEL), F32)],
        compiler_params=pltpu.CompilerParams(dimension_semantics=("arbitrary",),
                                             vmem_limit_bytes=V7X_VMEM_LIMIT_BYTES),
        name="mixer",
    )(*args, *consts)


def _router_kernel(logits_ref, br_ref, idx_ref, gate_ref, rank_ref, counts_ref, run_ref):
    tr = logits_ref.shape[1]

    @pl.when(pl.program_id(0) == 0)
    def _():
        run_ref[...] = jnp.zeros(run_ref.shape, F32)

    expert = lax.broadcasted_iota(jnp.int32, (N_EXPERTS, tr), 0)
    work = logits_ref[...] + br_ref[0]
    vals, idxs = [], []
    for _ in range(TOP_K):
        m = jnp.max(work, axis=0, keepdims=True)
        idx = jnp.min(jnp.where(work == m, expert, N_EXPERTS), axis=0, keepdims=True)
        vals.append(m)
        idxs.append(idx)
        work = jnp.where(expert == idx, NEG_BIG, work)
    exps = [jnp.exp(v - vals[0]) for v in vals]
    denom = exps[0] + exps[1] + exps[2] + exps[3]

    sel = jnp.zeros((N_EXPERTS, tr), F32)
    for k in range(TOP_K):
        sel = sel + jnp.where(expert == idxs[k], 1.0, 0.0)
    earlier = lax.broadcasted_iota(jnp.int32, (tr, tr), 0)
    later = lax.broadcasted_iota(jnp.int32, (tr, tr), 1)
    before = jnp.where(earlier < later, 1.0, 0.0).astype(BF16)
    prior = jnp.dot(sel.astype(BF16), before, preferred_element_type=F32) + run_ref[:, 0:1]

    row = lax.broadcasted_iota(jnp.int32, (V7X_SUBLANES, tr), 0)
    idx_out = jnp.zeros((V7X_SUBLANES, tr), jnp.int32)
    gate_out = jnp.zeros((V7X_SUBLANES, tr), F32)
    rank_out = jnp.zeros((V7X_SUBLANES, tr), F32)
    for k in range(TOP_K):
        rank_k = jnp.sum(jnp.where(expert == idxs[k], prior, 0.0), axis=0, keepdims=True)
        idx_out = jnp.where(row == k, idxs[k], idx_out)
        gate_out = jnp.where(row == k, exps[k] / denom, gate_out)
        rank_out = jnp.where(row == k, rank_k, rank_out)
    idx_ref[...] = idx_out
    gate_ref[...] = gate_out
    rank_ref[...] = rank_out.astype(jnp.int32)

    run_ref[...] = run_ref[...] + jnp.sum(sel, axis=1, keepdims=True)
    counts_ref[...] = run_ref[...]


def _router(logits_t, layer, br_t):
    t_tok = logits_t.shape[1]
    tr = TR_ROUTE
    tok_spec = pl.BlockSpec((V7X_SUBLANES, tr), lambda i: (0, i))
    return pl.pallas_call(
        _router_kernel,
        out_shape=(jax.ShapeDtypeStruct((V7X_SUBLANES, t_tok), jnp.int32),
                   jax.ShapeDtypeStruct((V7X_SUBLANES, t_tok), F32),
                   jax.ShapeDtypeStruct((V7X_SUBLANES, t_tok), jnp.int32),
                   jax.ShapeDtypeStruct((N_EXPERTS, V7X_LANES), F32)),
        grid=(t_tok // tr,),
        in_specs=[pl.BlockSpec((N_EXPERTS, tr), lambda i: (0, i)), _layer_block(br_t, layer)],
        out_specs=(tok_spec, tok_spec, tok_spec,
                   pl.BlockSpec((N_EXPERTS, V7X_LANES), lambda i: (0, 0))),
        scratch_shapes=[pltpu.VMEM((N_EXPERTS, V7X_LANES), F32)],
        compiler_params=pltpu.CompilerParams(dimension_semantics=("arbitrary",)),
        name="router",
    )(logits_t, br_t)


def _sc_mesh():
    return plsc.VectorSubcoreMesh(core_axis_name="c", subcore_axis_name="s")


def _sc_worker_id():
    return lax.axis_index("s") * V7X_SC_CORES + lax.axis_index("c")


def _dispatch_rows(x3, dest_km, n_slots):
    t_tok = x3.shape[0]
    tok_per_worker = t_tok // V7X_SC_WORKERS
    assert tok_per_worker % SC_ROWS == 0

    def body(x_hbm, dest_hbm, out_hbm, rows_v, idx_v, sem):
        base = _sc_worker_id() * tok_per_worker

        @pl.loop(0, tok_per_worker // SC_ROWS)
        def _(step):
            t0 = pl.multiple_of(base + step * SC_ROWS, SC_ROWS)
            pltpu.sync_copy(x_hbm.at[pl.ds(t0, SC_ROWS)], rows_v)
            for k in range(TOP_K):
                pltpu.sync_copy(dest_hbm.at[k, pl.ds(t0, SC_ROWS)], idx_v.at[k])
            for k in range(TOP_K):
                pltpu.async_copy(rows_v, out_hbm.at[idx_v.at[k]], sem).wait()

    return pl.kernel(
        body, mesh=_sc_mesh(),
        out_type=jax.ShapeDtypeStruct((n_slots,) + x3.shape[1:], x3.dtype),
        scratch_types=[pltpu.VMEM((SC_ROWS,) + x3.shape[1:], x3.dtype),
                       pltpu.VMEM((TOP_K, SC_ROWS), jnp.int32),
                       pltpu.SemaphoreType.DMA],
    )(x3, dest_km)


def _gather_rows(ys3, src_rows):
    n = src_rows.shape[0]
    rows_per_worker = n // V7X_SC_WORKERS
    assert rows_per_worker % SC_ROWS == 0

    def body(ys_hbm, src_hbm, out_hbm, rows_v, idx_v, sem):
        base = _sc_worker_id() * rows_per_worker

        @pl.loop(0, rows_per_worker // SC_ROWS)
        def _(step):
            a0 = pl.multiple_of(base + step * SC_ROWS, SC_ROWS)
            pltpu.sync_copy(src_hbm.at[pl.ds(a0, SC_ROWS)], idx_v)
            pltpu.async_copy(ys_hbm.at[idx_v], rows_v, sem).wait()
            pltpu.sync_copy(rows_v, out_hbm.at[pl.ds(a0, SC_ROWS)])

    return pl.kernel(
        body, mesh=_sc_mesh(),
        out_type=jax.ShapeDtypeStruct((n,) + ys3.shape[1:], ys3.dtype),
        scratch_types=[pltpu.VMEM((SC_ROWS,) + ys3.shape[1:], ys3.dtype),
                       pltpu.VMEM((SC_ROWS,), jnp.int32),
                       pltpu.SemaphoreType.DMA],
    )(ys3, src_rows)


def _expert_kernel(block_e_ref, nb_used_ref, next_e_ref, xs_ref, w_gu_hbm, b_gu_ref, w_down_hbm,
                   b_down_ref, ys_ref, w_gu_f, w_down_f, w_gu_b, w_down_b, sem_gu, sem_down,
                   slot_ref, *, layer):
    bm = BM_EXPERT
    b = pl.program_id(0)
    e = block_e_ref[b]
    live = b < nb_used_ref[0]
    new_expert = jnp.logical_or(b == 0, e != block_e_ref[jnp.maximum(b - 1, 0)])

    def weight_copies(expert, slot):
        idx = layer * N_EXPERTS + expert
        return (pltpu.make_async_copy(w_gu_hbm.at[idx], w_gu_f.at[slot], sem_gu.at[slot]),
                pltpu.make_async_copy(w_down_hbm.at[idx], w_down_f.at[slot], sem_down.at[slot]))

    @pl.when(b == 0)
    def _():
        slot_ref[0] = 0
        for copy in weight_copies(e, 0):
            copy.start()

    @pl.when(jnp.logical_and(new_expert, live))
    def _():
        slot = slot_ref[0]
        nxt = next_e_ref[e]

        @pl.when(nxt >= 0)
        def _():
            for copy in weight_copies(nxt, 1 - slot):
                copy.start()

        for copy in weight_copies(e, slot):
            copy.wait()
        slot_ref[0] = 1 - slot

        cb = DEINTERLEAVE_COLS
        src = lax.broadcasted_iota(jnp.int32, (cb, cb), 0)
        dst = lax.broadcasted_iota(jnp.int32, (cb, cb), 1)
        pick = jnp.where(dst < cb // 2, 2 * dst, 2 * (dst - cb // 2) + 1)
        sel = jnp.where(src == pick, 1.0, 0.0).astype(BF16)
        for c in range(2 * D_FF // cb):
            blk = jnp.dot(w_gu_f[slot, :, c * cb:(c + 1) * cb].astype(BF16), sel,
                          preferred_element_type=F32).astype(BF16)
            lo = c * (cb // 2)
            w_gu_b[:, lo:lo + cb // 2] = blk[:, :cb // 2]
            w_gu_b[:, D_FF + lo:D_FF + lo + cb // 2] = blk[:, cb // 2:]
        w_down_b[...] = w_down_f[slot].astype(BF16)

    @pl.when(live)
    def _():
        xb = _load_token_rows(xs_ref, bm).astype(BF16)
        h = jnp.dot(xb, w_gu_b[...], preferred_element_type=F32) + b_gu_ref[0]
        gate = jnp.minimum(h[:, :D_FF], SWIGLU_LIMIT)
        up = jnp.clip(h[:, D_FF:], -SWIGLU_LIMIT, SWIGLU_LIMIT)
        act = (up + 1.0) * (gate * jax.nn.sigmoid(SWIGLU_ALPHA * gate))
        y = jnp.dot(act.astype(BF16), w_down_b[...], preferred_element_type=F32) + b_down_ref[0]
        _store_token_rows(ys_ref, y)

    @pl.when(jnp.logical_not(live))
    def _():
        ys_ref[...] = jnp.zeros(ys_ref.shape, ys_ref.dtype)


def _experts(block_e, nb_used, next_e, xs_rows, layer, w_gu, b_gu, w_down, b_down):
    bm = BM_EXPERT
    n_blocks = xs_rows.shape[0] // (bm * ROW_WORDS_TILES)

    def x_map(b, be, nb, nx):
        return (jnp.minimum(b, nb[0] - 1), 0)

    def e_map(b, be, nb, nx):
        return (layer * N_EXPERTS + be[b], 0, 0)

    row_block = (bm * ROW_WORDS_TILES, V7X_LANES)
    grid_spec = pltpu.PrefetchScalarGridSpec(
        num_scalar_prefetch=3,
        grid=(n_blocks,),
        in_specs=[pl.BlockSpec(row_block, x_map),
                  pl.BlockSpec(memory_space=pl.ANY),
                  pl.BlockSpec((1, 1, 2 * D_FF), e_map),
                  pl.BlockSpec(memory_space=pl.ANY),
                  pl.BlockSpec((1, 1, D_MODEL), e_map)],
        out_specs=pl.BlockSpec(row_block, lambda b, be, nb, nx: (b, 0)),
        scratch_shapes=[pltpu.VMEM((2, D_MODEL, 2 * D_FF), F32),
                        pltpu.VMEM((2, D_FF, D_MODEL), F32),
                        pltpu.VMEM((D_MODEL, 2 * D_FF), BF16),
                        pltpu.VMEM((D_FF, D_MODEL), BF16),
                        pltpu.SemaphoreType.DMA((2,)),
                        pltpu.SemaphoreType.DMA((2,)),
                        pltpu.SMEM((1,), jnp.int32)],
    )
    return pl.pallas_call(
        functools.partial(_expert_kernel, layer=layer),
        out_shape=jax.ShapeDtypeStruct(xs_rows.shape, xs_rows.dtype),
        grid_spec=grid_spec,
        compiler_params=pltpu.CompilerParams(dimension_semantics=("arbitrary",),
                                             vmem_limit_bytes=V7X_VMEM_LIMIT_BYTES),
        name="experts",
    )(block_e, nb_used, next_e, xs_rows, w_gu, b_gu, w_down, b_down)


def _combine_kernel(x_ref, yk_ref, gate_ref, g_ref, b_ref, o_ref):
    o_ref[...] = _combine_rows(x_ref[...], yk_ref, gate_ref[...], g_ref[0], b_ref[0])


def _combine(x1, yk4, gates_pad, layer, ln_g, ln_b):
    t_tok = x1.shape[0]
    tm = TM_COMB
    return pl.pallas_call(
        _combine_kernel,
        out_shape=jax.ShapeDtypeStruct((t_tok, D_MODEL), F32),
        grid=(t_tok // tm,),
        in_specs=[pl.BlockSpec((tm, D_MODEL), lambda i: (i, 0)),
                  pl.BlockSpec((TOP_K, tm * ROW_WORDS_TILES, V7X_LANES), lambda i: (0, i, 0)),
                  pl.BlockSpec((tm, GATE_LANES), lambda i: (i, 0)),
                  _layer_block(ln_g, layer), _layer_block(ln_b, layer)],
        out_specs=pl.BlockSpec((tm, D_MODEL), lambda i: (i, 0)),
        compiler_params=pltpu.CompilerParams(dimension_semantics=("arbitrary",)),
        name="combine",
    )(x1, yk4, gates_pad, ln_g, ln_b)


def _moe_rows(x1_rows, logits_t, layer, br_t, w_gu, b_gu, w_down, b_down):
    t_tok = logits_t.shape[1]
    n_assign = t_tok * TOP_K
    bm = BM_EXPERT
    n_blocks = n_assign // bm + N_EXPERTS
    n_slots = n_blocks * bm

    idx_t, gate_t, rank_t, counts_f = _router(logits_t, layer, br_t)
    top_idx = idx_t[:TOP_K]
    gates_pad = jnp.pad(gate_t[:TOP_K].T, ((0, 0), (0, GATE_LANES - TOP_K)))

    counts = counts_f[:, 0].astype(jnp.int32)
    blocks_per_e = (counts + bm - 1) // bm
    blk_end = jnp.cumsum(blocks_per_e)
    row_start = (blk_end - blocks_per_e) * bm
    e_ids = jnp.arange(N_EXPERTS, dtype=jnp.int32)
    start_of = jnp.sum(jnp.where(top_idx[None] == e_ids[:, None, None],
                                 row_start[:, None, None], 0), axis=0)
    dest_km = start_of + rank_t[:TOP_K]
    block_e = jnp.minimum(
        jnp.sum(blk_end[None, :] <= jnp.arange(n_blocks, dtype=jnp.int32)[:, None], axis=1),
        N_EXPERTS - 1).astype(jnp.int32)
    nb_used = blk_end[-1:].astype(jnp.int32)
    later_with_rows = jnp.where((e_ids[None, :] > e_ids[:, None]) & (counts[None, :] > 0),
                                e_ids[None, :], N_EXPERTS)
    next_e = jnp.min(later_with_rows, axis=1)
    next_e = jnp.where(next_e == N_EXPERTS, -1, next_e).astype(jnp.int32)

    tile = (ROW_WORDS_TILES, V7X_LANES)
    xs3 = _dispatch_rows(x1_rows.reshape(t_tok, *tile), dest_km, n_slots)
    ys_rows = _experts(block_e, nb_used, next_e,
                       xs3.reshape(n_slots * ROW_WORDS_TILES, V7X_LANES),
                       layer, w_gu, b_gu, w_down, b_down)
    yk3 = _gather_rows(ys_rows.reshape(n_slots, *tile), dest_km.reshape(n_assign))
    return yk3.reshape(TOP_K, t_tok * ROW_WORDS_TILES, V7X_LANES), gates_pad


def kernel(x, w_in, b_in, conv_a, w_out_a, w_pool, scale_pool, conv_c, conv_c_b, ln_c_g, ln_c_b,
           w_out_c, b_out_c, w_o, ln1_g, ln1_b, w_router, b_router, w_gu, b_gu, w_down, b_down,
           ln2_g, ln2_b):
    bsz, seq_len, d = x.shape
    assert d == D_MODEL
    t_tok = bsz * seq_len
    depth = w_in.shape[0]

    def row(v):
        return v[:, None, :]

    w_in_g = jnp.stack([w_in[:, :, c * D_MODEL:(c + 1) * D_MODEL].astype(BF16)
                        for c in MIX_COL_ORDER], axis=1)
    b_in_g = jnp.stack([b_in[:, None, c * D_MODEL:(c + 1) * D_MODEL]
                        for c in MIX_COL_ORDER], axis=1)
    wr_t = jnp.transpose(w_router, (0, 2, 1)).astype(BF16)
    mixer_consts = (w_in_g, b_in_g, conv_a, w_out_a.astype(BF16),
                    w_pool.astype(BF16), row(scale_pool), conv_c, row(conv_c_b), row(ln_c_g),
                    row(ln_c_b), w_out_c.astype(BF16), row(b_out_c), w_o.astype(BF16),
                    row(ln1_g), row(ln1_b), wr_t)
    br_t = jnp.broadcast_to(b_router[:, :, None], (depth, N_EXPERTS, TR_ROUTE))
    n_le = depth * N_EXPERTS
    w_gu_f = w_gu.reshape(n_le, D_MODEL, 2 * D_FF)
    b_gu_d = jnp.concatenate([b_gu[..., 0::2], b_gu[..., 1::2]], axis=-1).reshape(n_le, 1, 2 * D_FF)
    w_down_f = w_down.reshape(n_le, D_FF, D_MODEL)
    b_down_d = b_down.reshape(n_le, 1, D_MODEL)
    ln2_g_r, ln2_b_r = row(ln2_g), row(ln2_b)

    assert bsz % TOKEN_GROUPS == 0
    per_group = t_tok // TOKEN_GROUPS
    x2 = x.reshape(t_tok, d)
    state = [None] * TOKEN_GROUPS
    for layer in range(depth):
        for g in range(TOKEN_GROUPS):
            if layer == 0:
                x1, x1_rows, logits_t = _mixer((x2,), layer, mixer_consts, seq_len=seq_len,
                                               n_tok=per_group, row0=g * per_group)
            else:
                x1, x1_rows, logits_t = _mixer(state[g], layer, mixer_consts, seq_len=seq_len,
                                               n_tok=per_group,
                                               combine_params=(ln2_g_r, ln2_b_r, layer - 1))
            yk4, gates = _moe_rows(x1_rows, logits_t, layer, br_t, w_gu_f, b_gu_d, w_down_f,
                                   b_down_d)
            state[g] = (x1, yk4, gates)
    outs = [_combine(x1, yk4, gates, depth - 1, ln2_g_r, ln2_b_r) for x1, yk4, gates in state]
    return jnp.concatenate(outs, axis=0).reshape(bsz, seq_len, d)
```

```python
import functools

import jax
import jax.numpy as jnp
from jax import lax
from jax.experimental import pallas as pl
from jax.experimental.pallas import tpu as pltpu
from jax.experimental.pallas import tpu_sc as plsc

D_MODEL = 1024
DEPTH = 4
CONV_A_WIDTH = 3
POOL_WINDOWS = (2, 4, 8, 16)
POOL_GROUP_DIM = D_MODEL // len(POOL_WINDOWS)
CONV_C_WIDTH = 31
N_EXPERTS = 32
TOP_K = 4
D_FF = D_MODEL
SWIGLU_LIMIT = 7.0
SWIGLU_ALPHA = 1.702
LN_EPS = 1e-5
DEEPNORM_ALPHA = (2.0 * DEPTH) ** 0.25

V7X_LANES = 128
V7X_SUBLANES = 8
V7X_VMEM_LIMIT_BYTES = 56 * 1024 * 1024
V7X_MXU_DIM = 256
V7X_SC_CORES = 2
V7X_SC_SUBCORES = 16
V7X_SC_WORKERS = V7X_SC_CORES * V7X_SC_SUBCORES

TM_MIX = 256
HALO_A = 8
HALO_P = 16
HALO_C = 32
CONV_LANE_CHUNK = 128
TAP_ROW_BLOCK = 64
CONV_C_ROW_BLOCK = 32
MIX_COL_ORDER = (4, 5, 1, 2, 0, 6, 3, 7, 8)
TR_ROUTE = 512
GATE_LANES = V7X_LANES
BM_EXPERT = 256
TM_COMB = 256
DEINTERLEAVE_COLS = V7X_MXU_DIM
ROW_WORDS_TILES = D_MODEL // 2 // V7X_LANES
SC_ROWS = 32
TOKEN_GROUPS = 2
NEG_BIG = -3.0e38

F32 = jnp.float32
BF16 = jnp.bfloat16


def _layer_norm(x, g, b):
    mu = jnp.mean(x, axis=-1, keepdims=True)
    xc = x - mu
    var = jnp.mean(xc * xc, axis=-1, keepdims=True)
    return xc * lax.rsqrt(var + LN_EPS) * g + b


def _store_token_rows(ref, val):
    rows = val.shape[0]
    words = pltpu.pack_elementwise([val[:, :D_MODEL // 2], val[:, D_MODEL // 2:]],
                                   packed_dtype=BF16)
    for g in range(ROW_WORDS_TILES):
        ref[pl.ds(g, rows, stride=ROW_WORDS_TILES), :] = words[:, g * V7X_LANES:(g + 1) * V7X_LANES]


def _load_token_rows(ref, rows):
    words = jnp.concatenate([ref[pl.ds(g, rows, stride=ROW_WORDS_TILES), :]
                             for g in range(ROW_WORDS_TILES)], axis=1)
    halves = [pltpu.unpack_elementwise(words, index=i, packed_dtype=BF16, unpacked_dtype=F32)
              for i in range(2)]
    return jnp.concatenate(halves, axis=1)


def _combine_rows(x1, yk_ref, gates, ln_g, ln_b):
    rows = x1.shape[0]
    m = jnp.zeros((rows, D_MODEL), F32)
    for k in range(TOP_K):
        m = m + gates[:, k:k + 1] * _load_token_rows(yk_ref.at[k], rows)
    return _layer_norm(DEEPNORM_ALPHA * x1 + m, ln_g, ln_b)


def _realign(ext_ref, shift_ref, halo, rows, lane0, lanes, max_shift):
    n = halo + rows - V7X_SUBLANES
    for b in range(1, min(max_shift, V7X_SUBLANES - 1) + 1):
        shift_ref[b - 1, 0:n, lane0:lane0 + lanes] = (
            ext_ref[V7X_SUBLANES - b:V7X_SUBLANES - b + n, lane0:lane0 + lanes])


def _causal_taps(ext_ref, shift_ref, halo, rows, lane0, lanes, weights):
    outs = []
    for r0 in range(0, rows, TAP_ROW_BLOCK):
        acc = None
        for j, w in enumerate(weights):
            a, b = divmod(j, V7X_SUBLANES)
            if b == 0:
                start = halo - V7X_SUBLANES * a + r0
                term = ext_ref[start:start + TAP_ROW_BLOCK, lane0:lane0 + lanes]
            else:
                start = halo - V7X_SUBLANES * (a + 1) + r0
                term = shift_ref[b - 1, start:start + TAP_ROW_BLOCK, lane0:lane0 + lanes]
            if w is not None:
                term = term * w
            acc = term if acc is None else acc + term
        outs.append(acc)
    return jnp.concatenate(outs, axis=0)


N_MIXER_CONSTS = 16


def _mixer_kernel(*refs, tiles_per_seq, fused_combine):
    n_in = 5 if fused_combine else 1
    inputs, refs = refs[:n_in], refs[n_in:]
    (w_in_ref, b_in_ref, conv_a_ref, w_out_a_ref, w_pool_ref, scale_pool_ref, conv_c_ref,
     conv_c_b_ref, ln_c_g_ref, ln_c_b_ref, w_out_c_ref, b_out_c_ref, w_o_ref, ln1_g_ref,
     ln1_b_ref, wr_ref) = refs[:N_MIXER_CONSTS]
    (y_ref, y_rows_ref, logits_ref,
     x_f32, xb_ref, ext_a, ext_p, ext_c, shift_ref, v_ref) = refs[N_MIXER_CONSTS:]
    tm = y_ref.shape[0]
    tile_in_seq = pl.program_id(0) % tiles_per_seq

    @pl.when(tile_in_seq == 0)
    def _():
        ext_a[0:HALO_A, :] = jnp.zeros((HALO_A, D_MODEL), F32)
        ext_p[0:HALO_P, :] = jnp.zeros((HALO_P, D_MODEL), F32)
        ext_c[0:HALO_C, :] = jnp.zeros((HALO_C, D_MODEL), F32)

    if fused_combine:
        x_prev_ref, yk_ref, gate_ref, ln2_g_ref, ln2_b_ref = inputs
        x_f32[...] = _combine_rows(x_prev_ref[...], yk_ref, gate_ref[...], ln2_g_ref[0],
                                   ln2_b_ref[0])
    else:
        x_f32[...] = inputs[0][...]
    xb_ref[...] = x_f32[...].astype(BF16)

    def proj(slot):
        lo = MIX_COL_ORDER[slot] * D_MODEL
        return (jnp.dot(xb_ref[...], w_in_ref[0, :, lo:lo + D_MODEL], preferred_element_type=F32)
                + b_in_ref[0, :, lo:lo + D_MODEL])

    ext_c[HALO_C:HALO_C + tm, :] = proj(0) * jax.nn.sigmoid(proj(1))
    ext_a[HALO_A:HALO_A + tm, :] = proj(2) * proj(3)
    _realign(ext_c, shift_ref, HALO_C, tm, 0, D_MODEL, CONV_C_WIDTH - 1)
    taps_of_copy = {}
    for j in range(CONV_C_WIDTH):
        a, b = divmod(j, V7X_SUBLANES)
        taps_of_copy.setdefault(b, []).append((a, j))

    def conv_block(i, carry):
        r0 = i * CONV_C_ROW_BLOCK
        for c0 in range(0, D_MODEL, CONV_LANE_CHUNK):
            lanes = slice(c0, c0 + CONV_LANE_CHUNK)
            acc = conv_c_b_ref[0, :, lanes]
            for b, taps in taps_of_copy.items():
                a_max = max(a for a, _ in taps)
                first = HALO_C - V7X_SUBLANES * (a_max + (1 if b else 0))
                start = pl.multiple_of(r0 + first, V7X_SUBLANES)
                span = CONV_C_ROW_BLOCK + V7X_SUBLANES * a_max
                if b == 0:
                    win = ext_c[pl.ds(start, span), lanes]
                else:
                    win = shift_ref[b - 1, pl.ds(start, span), lanes]
                for a, j in taps:
                    off = V7X_SUBLANES * (a_max - a)
                    w = conv_c_ref[0, CONV_C_WIDTH - 1 - j:CONV_C_WIDTH - j, lanes]
                    acc = acc + win[off:off + CONV_C_ROW_BLOCK] * w
            v_ref[pl.ds(pl.multiple_of(r0, CONV_C_ROW_BLOCK), CONV_C_ROW_BLOCK), lanes] = acc
        return carry

    lax.fori_loop(0, tm // CONV_C_ROW_BLOCK, conv_block, 0)
    ext_c[0:HALO_C, :] = ext_c[tm:tm + HALO_C, :]

    b_a = proj(4)
    gate_a = jax.nn.sigmoid(proj(5))
    parts = []
    for c0 in range(0, D_MODEL, CONV_LANE_CHUNK):
        w = [conv_a_ref[0, CONV_A_WIDTH - 1 - j:CONV_A_WIDTH - j, c0:c0 + CONV_LANE_CHUNK]
             for j in range(CONV_A_WIDTH)]
        _realign(ext_a, shift_ref, HALO_A, tm, c0, CONV_LANE_CHUNK, CONV_A_WIDTH - 1)
        parts.append(_causal_taps(ext_a, shift_ref, HALO_A, tm, c0, CONV_LANE_CHUNK, w))
    u_a = jnp.concatenate(parts, axis=1)
    ext_a[0:HALO_A, :] = ext_a[tm:tm + HALO_A, :]
    y_a = jnp.dot((b_a * u_a).astype(BF16), w_out_a_ref[0], preferred_element_type=F32)
    merged = gate_a * y_a

    v = _layer_norm(v_ref[...], ln_c_g_ref[0], ln_c_b_ref[0])
    v = v * jax.nn.sigmoid(v)
    y_c = jnp.dot(v.astype(BF16), w_out_c_ref[0], preferred_element_type=F32) + b_out_c_ref[0]
    merged = merged + jax.nn.sigmoid(proj(8)) * y_c

    p_in = proj(6)
    ext_p[HALO_P:HALO_P + tm, :] = p_in
    pos = tile_in_seq * tm + lax.broadcasted_iota(jnp.int32, (tm, POOL_GROUP_DIM), 0)
    parts = []
    for g, win in enumerate(POOL_WINDOWS):
        lo = g * POOL_GROUP_DIM
        sub = []
        for c0 in range(lo, lo + POOL_GROUP_DIM, CONV_LANE_CHUNK):
            _realign(ext_p, shift_ref, HALO_P, tm, c0, CONV_LANE_CHUNK, win - 1)
            sub.append(_causal_taps(ext_p, shift_ref, HALO_P, tm, c0, CONV_LANE_CHUNK,
                                    [None] * win))
        wsum = jnp.concatenate(sub, axis=1)
        cnt = jnp.minimum(pos + 1, win).astype(F32)
        pooled = wsum / cnt - p_in[:, lo:lo + POOL_GROUP_DIM]
        parts.append(jnp.dot(pooled.astype(BF16), w_pool_ref[0, g], preferred_element_type=F32))
    ext_p[0:HALO_P, :] = ext_p[tm:tm + HALO_P, :]
    y_b = jnp.concatenate(parts, axis=1) * scale_pool_ref[0]
    merged = merged + jax.nn.sigmoid(proj(7)) * y_b

    h = jnp.dot(merged.astype(BF16), w_o_ref[0], preferred_element_type=F32)
    y = _layer_norm(DEEPNORM_ALPHA * x_f32[...] + h, ln1_g_ref[0], ln1_b_ref[0])
    y_ref[...] = y
    _store_token_rows(y_rows_ref, y)
    logits_ref[...] = lax.dot_general(wr_ref[0], y.astype(BF16), (((1,), (1,)), ((), ())),
                                      preferred_element_type=F32)


def _layer_block(arr, layer, buffered_once=False):
    tail = (0,) * (arr.ndim - 1)
    mode = {"pipeline_mode": pl.Buffered(1)} if buffered_once else {}
    return pl.BlockSpec((1,) + arr.shape[1:], lambda *_: (layer,) + tail, **mode)


def _mixer(inputs, layer, consts, *, seq_len, n_tok, row0=0, combine_params=None):
    tm = TM_MIX
    assert seq_len % tm == 0 and n_tok % seq_len == 0 and row0 % tm == 0
    assert len(consts) == N_MIXER_CONSTS
    tile0 = row0 // tm
    tok_spec = pl.BlockSpec((tm, D_MODEL), lambda i: (i + tile0, 0))
    if combine_params is None:
        in_specs, args = [tok_spec], list(inputs)
    else:
        ln_g, ln_b, prev_layer = combine_params
        in_specs = [tok_spec,
                    pl.BlockSpec((TOP_K, tm * ROW_WORDS_TILES, V7X_LANES), lambda i: (0, i, 0)),
                    pl.BlockSpec((tm, GATE_LANES), lambda i: (i, 0)),
                    _layer_block(ln_g, prev_layer), _layer_block(ln_b, prev_layer)]
        args = list(inputs) + [ln_g, ln_b]
    return pl.pallas_call(
        functools.partial(_mixer_kernel, tiles_per_seq=seq_len // tm,
                          fused_combine=combine_params is not None),
        out_shape=(jax.ShapeDtypeStruct((n_tok, D_MODEL), F32),
                   jax.ShapeDtypeStruct((n_tok * ROW_WORDS_TILES, V7X_LANES), jnp.int32),
                   jax.ShapeDtypeStruct((N_EXPERTS, n_tok), F32)),
        grid=(n_tok // tm,),
        in_specs=in_specs + [_layer_block(c, layer, buffered_once=True) for c in consts],
        out_specs=(pl.BlockSpec((tm, D_MODEL), lambda i: (i, 0)),
                   pl.BlockSpec((tm * ROW_WORDS_TILES, V7X_LANES), lambda i: (i, 0)),
                   pl.BlockSpec((N_EXPERTS, tm), lambda i: (0, i))),
        scratch_shapes=[pltpu.VMEM((tm, D_MODEL), F32),
                        pltpu.VMEM((tm, D_MODEL), BF16),
                        pltpu.VMEM((HALO_A + tm, D_MODEL), F32),
                        pltpu.VMEM((HALO_P + tm, D_MODEL), F32),
                        pltpu.VMEM((HALO_C + tm, D_MODEL), F32),
                        pltpu.VMEM((V7X_SUBLANES - 1, HALO_C + tm - V7X_SUBLANES, D_MODEL), F32),
                        pltpu.VMEM((tm, D_MODEL), F32)],
        compiler_params=pltpu.CompilerParams(dimension_semantics=("arbitrary",),
                                             vmem_limit_bytes=V7X_VMEM_LIMIT_BYTES),
        name="mixer",
    )(*args, *consts)


def _router_kernel(logits_ref, br_ref, idx_ref, gate_ref, rank_ref, counts_ref, run_ref):
    tr = logits_ref.shape[1]

    @pl.when(pl.program_id(0) == 0)
    def _():
        run_ref[...] = jnp.zeros(run_ref.shape, F32)

    expert = lax.broadcasted_iota(jnp.int32, (N_EXPERTS, tr), 0)
    work = logits_ref[...] + br_ref[0]
    vals, idxs = [], []
    for _ in range(TOP_K):
        m = jnp.max(work, axis=0, keepdims=True)
        idx = jnp.min(jnp.where(work == m, expert, N_EXPERTS), axis=0, keepdims=True)
        vals.append(m)
        idxs.append(idx)
        work = jnp.where(expert == idx, NEG_BIG, work)
    exps = [jnp.exp(v - vals[0]) for v in vals]
    denom = exps[0] + exps[1] + exps[2] + exps[3]

    sel = jnp.zeros((N_EXPERTS, tr), F32)
    for k in range(TOP_K):
        sel = sel + jnp.where(expert == idxs[k], 1.0, 0.0)
    earlier = lax.broadcasted_iota(jnp.int32, (tr, tr), 0)
    later = lax.broadcasted_iota(jnp.int32, (tr, tr), 1)
    before = jnp.where(earlier < later, 1.0, 0.0).astype(BF16)
    prior = jnp.dot(sel.astype(BF16), before, preferred_element_type=F32) + run_ref[:, 0:1]

    row = lax.broadcasted_iota(jnp.int32, (V7X_SUBLANES, tr), 0)
    idx_out = jnp.zeros((V7X_SUBLANES, tr), jnp.int32)
    gate_out = jnp.zeros((V7X_SUBLANES, tr), F32)
    rank_out = jnp.zeros((V7X_SUBLANES, tr), F32)
    for k in range(TOP_K):
        rank_k = jnp.sum(jnp.where(expert == idxs[k], prior, 0.0), axis=0, keepdims=True)
        idx_out = jnp.where(row == k, idxs[k], idx_out)
        gate_out = jnp.where(row == k, exps[k] / denom, gate_out)
        rank_out = jnp.where(row == k, rank_k, rank_out)
    idx_ref[...] = idx_out
    rank_ref[...] = rank_out.astype(jnp.int32)
    gate_rows = jnp.concatenate(
        [gate_out, jnp.zeros((GATE_LANES - V7X_SUBLANES, tr), F32)], axis=0)
    gate_ref[...] = gate_rows.T

    run_ref[...] = run_ref[...] + jnp.sum(sel, axis=1, keepdims=True)
    counts_ref[...] = run_ref[...]


def _router(logits_t, layer, br_t):
    t_tok = logits_t.shape[1]
    tr = TR_ROUTE
    tok_spec = pl.BlockSpec((V7X_SUBLANES, tr), lambda i: (0, i))
    return pl.pallas_call(
        _router_kernel,
        out_shape=(jax.ShapeDtypeStruct((V7X_SUBLANES, t_tok), jnp.int32),
                   jax.ShapeDtypeStruct((t_tok, GATE_LANES), F32),
                   jax.ShapeDtypeStruct((V7X_SUBLANES, t_tok), jnp.int32),
                   jax.ShapeDtypeStruct((N_EXPERTS, V7X_LANES), F32)),
        grid=(t_tok // tr,),
        in_specs=[pl.BlockSpec((N_EXPERTS, tr), lambda i: (0, i)), _layer_block(br_t, layer)],
        out_specs=(tok_spec, pl.BlockSpec((tr, GATE_LANES), lambda i: (i, 0)), tok_spec,
                   pl.BlockSpec((N_EXPERTS, V7X_LANES), lambda i: (0, 0))),
        scratch_shapes=[pltpu.VMEM((N_EXPERTS, V7X_LANES), F32)],
        compiler_params=pltpu.CompilerParams(dimension_semantics=("arbitrary",)),
        name="router",
    )(logits_t, br_t)


def _sc_mesh():
    return plsc.VectorSubcoreMesh(core_axis_name="c", subcore_axis_name="s")


def _sc_worker_id():
    return lax.axis_index("s") * V7X_SC_CORES + lax.axis_index("c")


def _dispatch_rows(x3, dest_km, n_slots):
    t_tok = x3.shape[0]
    tok_per_worker = t_tok // V7X_SC_WORKERS
    assert tok_per_worker % SC_ROWS == 0

    def body(x_hbm, dest_hbm, out_hbm, rows_v, idx_v, sem):
        base = _sc_worker_id() * tok_per_worker

        @pl.loop(0, tok_per_worker // SC_ROWS)
        def _(step):
            t0 = pl.multiple_of(base + step * SC_ROWS, SC_ROWS)
            pltpu.sync_copy(x_hbm.at[pl.ds(t0, SC_ROWS)], rows_v)
            for k in range(TOP_K):
                pltpu.sync_copy(dest_hbm.at[k, pl.ds(t0, SC_ROWS)], idx_v.at[k])
            for k in range(TOP_K):
                pltpu.async_copy(rows_v, out_hbm.at[idx_v.at[k]], sem).wait()

    return pl.kernel(
        body, mesh=_sc_mesh(),
        out_type=jax.ShapeDtypeStruct((n_slots,) + x3.shape[1:], x3.dtype),
        scratch_types=[pltpu.VMEM((SC_ROWS,) + x3.shape[1:], x3.dtype),
                       pltpu.VMEM((TOP_K, SC_ROWS), jnp.int32),
                       pltpu.SemaphoreType.DMA],
    )(x3, dest_km)


def _gather_rows(ys3, src_rows):
    n = src_rows.shape[0]
    rows_per_worker = n // V7X_SC_WORKERS
    assert rows_per_worker % SC_ROWS == 0

    def body(ys_hbm, src_hbm, out_hbm, rows_v, idx_v, sem):
        base = _sc_worker_id() * rows_per_worker

        @pl.loop(0, rows_per_worker // SC_ROWS)
        def _(step):
            a0 = pl.multiple_of(base + step * SC_ROWS, SC_ROWS)
            pltpu.sync_copy(src_hbm.at[pl.ds(a0, SC_ROWS)], idx_v)
            pltpu.async_copy(ys_hbm.at[idx_v], rows_v, sem).wait()
            pltpu.sync_copy(rows_v, out_hbm.at[pl.ds(a0, SC_ROWS)])

    return pl.kernel(
        body, mesh=_sc_mesh(),
        out_type=jax.ShapeDtypeStruct((n,) + ys3.shape[1:], ys3.dtype),
        scratch_types=[pltpu.VMEM((SC_ROWS,) + ys3.shape[1:], ys3.dtype),
                       pltpu.VMEM((SC_ROWS,), jnp.int32),
                       pltpu.SemaphoreType.DMA],
    )(ys3, src_rows)


def _expert_kernel(block_e_ref, nb_used_ref, next_e_ref, xs_ref, w_gu_hbm, b_gu_ref, w_down_hbm,
                   b_down_ref, ys_ref, w_gu_f, w_down_f, w_gu_b, w_down_b, sem_gu, sem_down,
                   slot_ref, *, layer):
    bm = BM_EXPERT
    b = pl.program_id(0)
    e = block_e_ref[b]
    live = b < nb_used_ref[0]
    new_expert = jnp.logical_or(b == 0, e != block_e_ref[jnp.maximum(b - 1, 0)])

    def weight_copies(expert, slot):
        idx = layer * N_EXPERTS + expert
        return (pltpu.make_async_copy(w_gu_hbm.at[idx], w_gu_f.at[slot], sem_gu.at[slot]),
                pltpu.make_async_copy(w_down_hbm.at[idx], w_down_f.at[slot], sem_down.at[slot]))

    @pl.when(b == 0)
    def _():
        slot_ref[0] = 0
        for copy in weight_copies(e, 0):
            copy.start()

    @pl.when(jnp.logical_and(new_expert, live))
    def _():
        slot = slot_ref[0]
        nxt = next_e_ref[e]

        @pl.when(nxt >= 0)
        def _():
            for copy in weight_copies(nxt, 1 - slot):
                copy.start()

        for copy in weight_copies(e, slot):
            copy.wait()
        slot_ref[0] = 1 - slot

        cb = DEINTERLEAVE_COLS
        src = lax.broadcasted_iota(jnp.int32, (cb, cb), 0)
        dst = lax.broadcasted_iota(jnp.int32, (cb, cb), 1)
        pick = jnp.where(dst < cb // 2, 2 * dst, 2 * (dst - cb // 2) + 1)
        sel = jnp.where(src == pick, 1.0, 0.0).astype(BF16)
        for c in range(2 * D_FF // cb):
            blk = jnp.dot(w_gu_f[slot, :, c * cb:(c + 1) * cb].astype(BF16), sel,
                          preferred_element_type=F32).astype(BF16)
            lo = c * (cb // 2)
            w_gu_b[:, lo:lo + cb // 2] = blk[:, :cb // 2]
            w_gu_b[:, D_FF + lo:D_FF + lo + cb // 2] = blk[:, cb // 2:]
        w_down_b[...] = w_down_f[slot].astype(BF16)

    @pl.when(live)
    def _():
        xb = _load_token_rows(xs_ref, bm).astype(BF16)
        h = jnp.dot(xb, w_gu_b[...], preferred_element_type=F32) + b_gu_ref[0]
        gate = jnp.minimum(h[:, :D_FF], SWIGLU_LIMIT)
        up = jnp.clip(h[:, D_FF:], -SWIGLU_LIMIT, SWIGLU_LIMIT)
        act = (up + 1.0) * (gate * jax.nn.sigmoid(SWIGLU_ALPHA * gate))
        y = jnp.dot(act.astype(BF16), w_down_b[...], preferred_element_type=F32) + b_down_ref[0]
        _store_token_rows(ys_ref, y)

    @pl.when(jnp.logical_not(live))
    def _():
        ys_ref[...] = jnp.zeros(ys_ref.shape, ys_ref.dtype)


def _experts(block_e, nb_used, next_e, xs_rows, layer, w_gu, b_gu, w_down, b_down):
    bm = BM_EXPERT
    n_blocks = xs_rows.shape[0] // (bm * ROW_WORDS_TILES)

    def x_map(b, be, nb, nx):
        return (jnp.minimum(b, nb[0] - 1), 0)

    def e_map(b, be, nb, nx):
        return (layer * N_EXPERTS + be[b], 0, 0)

    row_block = (bm * ROW_WORDS_TILES, V7X_LANES)
    grid_spec = pltpu.PrefetchScalarGridSpec(
        num_scalar_prefetch=3,
        grid=(n_blocks,),
        in_specs=[pl.BlockSpec(row_block, x_map),
                  pl.BlockSpec(memory_space=pl.ANY),
                  pl.BlockSpec((1, 1, 2 * D_FF), e_map),
                  pl.BlockSpec(memory_space=pl.ANY),
                  pl.BlockSpec((1, 1, D_MODEL), e_map)],
        out_specs=pl.BlockSpec(row_block, lambda b, be, nb, nx: (b, 0)),
        scratch_shapes=[pltpu.VMEM((2, D_MODEL, 2 * D_FF), F32),
                        pltpu.VMEM((2, D_FF, D_MODEL), F32),
                        pltpu.VMEM((D_MODEL, 2 * D_FF), BF16),
                        pltpu.VMEM((D_FF, D_MODEL), BF16),
                        pltpu.SemaphoreType.DMA((2,)),
                        pltpu.SemaphoreType.DMA((2,)),
                        pltpu.SMEM((1,), jnp.int32)],
    )
    return pl.pallas_call(
        functools.partial(_expert_kernel, layer=layer),
        out_shape=jax.ShapeDtypeStruct(xs_rows.shape, xs_rows.dtype),
        grid_spec=grid_spec,
        compiler_params=pltpu.CompilerParams(dimension_semantics=("arbitrary",),
                                             vmem_limit_bytes=V7X_VMEM_LIMIT_BYTES),
        name="experts",
    )(block_e, nb_used, next_e, xs_rows, w_gu, b_gu, w_down, b_down)


def _combine_kernel(x_ref, yk_ref, gate_ref, g_ref, b_ref, *out_refs):
    o_ref = out_refs[-1]
    o_ref[...] = _combine_rows(x_ref[...], yk_ref, gate_ref[...], g_ref[0], b_ref[0])


def _combine(x1, yk4, gates_pad, layer, ln_g, ln_b, *, out_rows, row0, out_so_far=None):
    t_tok = x1.shape[0]
    tm = TM_COMB
    assert row0 % tm == 0
    tile0 = row0 // tm
    in_specs = [pl.BlockSpec((tm, D_MODEL), lambda i: (i, 0)),
                pl.BlockSpec((TOP_K, tm * ROW_WORDS_TILES, V7X_LANES), lambda i: (0, i, 0)),
                pl.BlockSpec((tm, GATE_LANES), lambda i: (i, 0)),
                _layer_block(ln_g, layer), _layer_block(ln_b, layer)]
    args = [x1, yk4, gates_pad, ln_g, ln_b]
    aliases = {}
    if out_so_far is not None:
        in_specs.append(pl.BlockSpec(memory_space=pl.ANY))
        args.append(out_so_far)
        aliases = {len(args) - 1: 0}
    return pl.pallas_call(
        _combine_kernel,
        out_shape=jax.ShapeDtypeStruct((out_rows, D_MODEL), F32),
        grid=(t_tok // tm,),
        in_specs=in_specs,
        out_specs=pl.BlockSpec((tm, D_MODEL), lambda i: (i + tile0, 0)),
        input_output_aliases=aliases,
        compiler_params=pltpu.CompilerParams(dimension_semantics=("arbitrary",)),
        name="combine",
    )(*args)


def _moe_rows(x1_rows, logits_t, layer, br_t, w_gu, b_gu, w_down, b_down):
    t_tok = logits_t.shape[1]
    n_assign = t_tok * TOP_K
    bm = BM_EXPERT
    n_blocks = n_assign // bm + N_EXPERTS
    n_slots = n_blocks * bm

    idx_t, gates_pad, rank_t, counts_f = _router(logits_t, layer, br_t)
    top_idx = idx_t[:TOP_K]

    counts = counts_f[:, 0].astype(jnp.int32)
    blocks_per_e = (counts + bm - 1) // bm
    blk_end = jnp.cumsum(blocks_per_e)
    row_start = (blk_end - blocks_per_e) * bm
    e_ids = jnp.arange(N_EXPERTS, dtype=jnp.int32)
    start_of = jnp.sum(jnp.where(top_idx[None] == e_ids[:, None, None],
                                 row_start[:, None, None], 0), axis=0)
    dest_km = start_of + rank_t[:TOP_K]
    block_e = jnp.minimum(
        jnp.sum(blk_end[None, :] <= jnp.arange(n_blocks, dtype=jnp.int32)[:, None], axis=1),
        N_EXPERTS - 1).astype(jnp.int32)
    nb_used = blk_end[-1:].astype(jnp.int32)
    later_with_rows = jnp.where((e_ids[None, :] > e_ids[:, None]) & (counts[None, :] > 0),
                                e_ids[None, :], N_EXPERTS)
    next_e = jnp.min(later_with_rows, axis=1)
    next_e = jnp.where(next_e == N_EXPERTS, -1, next_e).astype(jnp.int32)

    tile = (ROW_WORDS_TILES, V7X_LANES)
    xs3 = _dispatch_rows(x1_rows.reshape(t_tok, *tile), dest_km, n_slots)
    ys_rows = _experts(block_e, nb_used, next_e,
                       xs3.reshape(n_slots * ROW_WORDS_TILES, V7X_LANES),
                       layer, w_gu, b_gu, w_down, b_down)
    yk3 = _gather_rows(ys_rows.reshape(n_slots, *tile), dest_km.reshape(n_assign))
    return yk3.reshape(TOP_K, t_tok * ROW_WORDS_TILES, V7X_LANES), gates_pad


def kernel(x, w_in, b_in, conv_a, w_out_a, w_pool, scale_pool, conv_c, conv_c_b, ln_c_g, ln_c_b,
           w_out_c, b_out_c, w_o, ln1_g, ln1_b, w_router, b_router, w_gu, b_gu, w_down, b_down,
           ln2_g, ln2_b):
    bsz, seq_len, d = x.shape
    assert d == D_MODEL
    t_tok = bsz * seq_len
    depth = w_in.shape[0]

    def row(v):
        return v[:, None, :]

    wr_t = jnp.transpose(w_router, (0, 2, 1)).astype(BF16)
    mixer_consts = (w_in.astype(BF16), row(b_in), conv_a, w_out_a.astype(BF16),
                    w_pool.astype(BF16), row(scale_pool), conv_c, row(conv_c_b), row(ln_c_g),
                    row(ln_c_b), w_out_c.astype(BF16), row(b_out_c), w_o.astype(BF16),
                    row(ln1_g), row(ln1_b), wr_t)
    br_t = jnp.broadcast_to(b_router[:, :, None], (depth, N_EXPERTS, TR_ROUTE))
    n_le = depth * N_EXPERTS
    w_gu_f = w_gu.reshape(n_le, D_MODEL, 2 * D_FF)
    b_gu_d = jnp.concatenate([b_gu[..., 0::2], b_gu[..., 1::2]], axis=-1).reshape(n_le, 1, 2 * D_FF)
    w_down_f = w_down.reshape(n_le, D_FF, D_MODEL)
    b_down_d = b_down.reshape(n_le, 1, D_MODEL)
    ln2_g_r, ln2_b_r = row(ln2_g), row(ln2_b)

    assert bsz % TOKEN_GROUPS == 0
    per_group = t_tok // TOKEN_GROUPS
    x2 = x.reshape(t_tok, d)
    state = [None] * TOKEN_GROUPS
    for layer in range(depth):
        for g in range(TOKEN_GROUPS):
            if layer == 0:
                x1, x1_rows, logits_t = _mixer((x2,), layer, mixer_consts, seq_len=seq_len,
                                               n_tok=per_group, row0=g * per_group)
            else:
                x1, x1_rows, logits_t = _mixer(state[g], layer, mixer_consts, seq_len=seq_len,
                                               n_tok=per_group,
                                               combine_params=(ln2_g_r, ln2_b_r, layer - 1))
            yk4, gates = _moe_rows(x1_rows, logits_t, layer, br_t, w_gu_f, b_gu_d, w_down_f,
                                   b_down_d)
            state[g] = (x1, yk4, gates)
    out = None
    for g, (x1, yk4, gates) in enumerate(state):
        out = _combine(x1, yk4, gates, depth - 1, ln2_g_r, ln2_b_r, out_rows=t_tok,
                       row0=g * per_group, out_so_far=out)
    return out.reshape(bsz, seq_len, d)
```

```python
import functools

import jax
import jax.numpy as jnp
from jax import lax
from jax.experimental import pallas as pl
from jax.experimental.pallas import tpu as pltpu
from jax.experimental.pallas import tpu_sc as plsc

D_MODEL = 1024
DEPTH = 4
CONV_A_WIDTH = 3
POOL_WINDOWS = (2, 4, 8, 16)
POOL_GROUP_DIM = D_MODEL // len(POOL_WINDOWS)
CONV_C_WIDTH = 31
N_EXPERTS = 32
TOP_K = 4
D_FF = D_MODEL
SWIGLU_LIMIT = 7.0
SWIGLU_ALPHA = 1.702
LN_EPS = 1e-5
DEEPNORM_ALPHA = (2.0 * DEPTH) ** 0.25

V7X_LANES = 128
V7X_SUBLANES = 8
V7X_VMEM_LIMIT_BYTES = 56 * 1024 * 1024
V7X_MXU_DIM = 256
V7X_SC_CORES = 2
V7X_SC_SUBCORES = 16
V7X_SC_WORKERS = V7X_SC_CORES * V7X_SC_SUBCORES

TM_MIX = 256
HALO_A = 8
HALO_P = 16
HALO_C = 32
CONV_LANE_CHUNK = 128
TAP_ROW_BLOCK = 64
CONV_C_ROW_BLOCK = 32
MIX_COL_ORDER = (4, 5, 1, 2, 0, 6, 3, 7, 8)
TR_ROUTE = 512
GATE_LANES = V7X_LANES
BM_EXPERT = 256
TM_COMB = 256
DEINTERLEAVE_COLS = V7X_MXU_DIM
ROW_WORDS_TILES = D_MODEL // 2 // V7X_LANES
SC_ROWS = 32
TOKEN_GROUPS = 2
NEG_BIG = -3.0e38

F32 = jnp.float32
BF16 = jnp.bfloat16


def _layer_norm(x, g, b):
    mu = jnp.mean(x, axis=-1, keepdims=True)
    xc = x - mu
    var = jnp.mean(xc * xc, axis=-1, keepdims=True)
    return xc * lax.rsqrt(var + LN_EPS) * g + b


def _store_token_rows(ref, val):
    rows = val.shape[0]
    words = pltpu.pack_elementwise([val[:, :D_MODEL // 2], val[:, D_MODEL // 2:]],
                                   packed_dtype=BF16)
    for g in range(ROW_WORDS_TILES):
        ref[pl.ds(g, rows, stride=ROW_WORDS_TILES), :] = words[:, g * V7X_LANES:(g + 1) * V7X_LANES]


def _load_token_rows(ref, rows):
    words = jnp.concatenate([ref[pl.ds(g, rows, stride=ROW_WORDS_TILES), :]
                             for g in range(ROW_WORDS_TILES)], axis=1)
    halves = [pltpu.unpack_elementwise(words, index=i, packed_dtype=BF16, unpacked_dtype=F32)
              for i in range(2)]
    return jnp.concatenate(halves, axis=1)


def _combine_rows(x1, yk_ref, gates, ln_g, ln_b):
    rows = x1.shape[0]
    m = jnp.zeros((rows, D_MODEL), F32)
    for k in range(TOP_K):
        m = m + gates[:, k:k + 1] * _load_token_rows(yk_ref.at[k], rows)
    return _layer_norm(DEEPNORM_ALPHA * x1 + m, ln_g, ln_b)


def _realign(ext_ref, shift_ref, halo, rows, lane0, lanes, max_shift):
    n = halo + rows - V7X_SUBLANES
    for b in range(1, min(max_shift, V7X_SUBLANES - 1) + 1):
        shift_ref[b - 1, 0:n, lane0:lane0 + lanes] = (
            ext_ref[V7X_SUBLANES - b:V7X_SUBLANES - b + n, lane0:lane0 + lanes])


def _causal_taps(ext_ref, shift_ref, halo, rows, lane0, lanes, weights):
    outs = []
    for r0 in range(0, rows, TAP_ROW_BLOCK):
        acc = None
        for j, w in enumerate(weights):
            a, b = divmod(j, V7X_SUBLANES)
            if b == 0:
                start = halo - V7X_SUBLANES * a + r0
                term = ext_ref[start:start + TAP_ROW_BLOCK, lane0:lane0 + lanes]
            else:
                start = halo - V7X_SUBLANES * (a + 1) + r0
                term = shift_ref[b - 1, start:start + TAP_ROW_BLOCK, lane0:lane0 + lanes]
            if w is not None:
                term = term * w
            acc = term if acc is None else acc + term
        outs.append(acc)
    return jnp.concatenate(outs, axis=0)


N_MIXER_CONSTS = 16


def _mixer_kernel(*refs, tiles_per_seq, fused_combine):
    n_in = 5 if fused_combine else 1
    inputs, refs = refs[:n_in], refs[n_in:]
    (w_in_ref, b_in_ref, conv_a_ref, w_out_a_ref, w_pool_ref, scale_pool_ref, conv_c_ref,
     conv_c_b_ref, ln_c_g_ref, ln_c_b_ref, w_out_c_ref, b_out_c_ref, w_o_ref, ln1_g_ref,
     ln1_b_ref, wr_ref) = refs[:N_MIXER_CONSTS]
    (y_ref, y_rows_ref, logits_ref,
     x_f32, xb_ref, ext_a, ext_p, ext_c, shift_ref, v_ref) = refs[N_MIXER_CONSTS:]
    tm = y_ref.shape[0]
    tile_in_seq = pl.program_id(0) % tiles_per_seq

    @pl.when(tile_in_seq == 0)
    def _():
        ext_a[0:HALO_A, :] = jnp.zeros((HALO_A, D_MODEL), F32)
        ext_p[0:HALO_P, :] = jnp.zeros((HALO_P, D_MODEL), F32)
        ext_c[0:HALO_C, :] = jnp.zeros((HALO_C, D_MODEL), F32)

    if fused_combine:
        x_prev_ref, yk_ref, gate_ref, ln2_g_ref, ln2_b_ref = inputs
        x_f32[...] = _combine_rows(x_prev_ref[...], yk_ref, gate_ref[...], ln2_g_ref[0],
                                   ln2_b_ref[0])
    else:
        x_f32[...] = inputs[0][...]
    xb_ref[...] = x_f32[...].astype(BF16)

    def proj(slot):
        lo = MIX_COL_ORDER[slot] * D_MODEL
        return (jnp.dot(xb_ref[...], w_in_ref[0, :, lo:lo + D_MODEL], preferred_element_type=F32)
                + b_in_ref[0, :, lo:lo + D_MODEL])

    ext_c[HALO_C:HALO_C + tm, :] = proj(0) * jax.nn.sigmoid(proj(1))
    ext_a[HALO_A:HALO_A + tm, :] = proj(2) * proj(3)
    _realign(ext_c, shift_ref, HALO_C, tm, 0, D_MODEL, CONV_C_WIDTH - 1)
    taps_of_copy = {}
    for j in range(CONV_C_WIDTH):
        a, b = divmod(j, V7X_SUBLANES)
        taps_of_copy.setdefault(b, []).append((a, j))

    def conv_block(i, carry):
        r0 = i * CONV_C_ROW_BLOCK
        for c0 in range(0, D_MODEL, CONV_LANE_CHUNK):
            lanes = slice(c0, c0 + CONV_LANE_CHUNK)
            acc = conv_c_b_ref[0, :, lanes]
            for b, taps in taps_of_copy.items():
                a_max = max(a for a, _ in taps)
                first = HALO_C - V7X_SUBLANES * (a_max + (1 if b else 0))
                start = pl.multiple_of(r0 + first, V7X_SUBLANES)
                span = CONV_C_ROW_BLOCK + V7X_SUBLANES * a_max
                if b == 0:
                    win = ext_c[pl.ds(start, span), lanes]
                else:
                    win = shift_ref[b - 1, pl.ds(start, span), lanes]
                for a, j in taps:
                    off = V7X_SUBLANES * (a_max - a)
                    w = conv_c_ref[0, CONV_C_WIDTH - 1 - j:CONV_C_WIDTH - j, lanes]
                    acc = acc + win[off:off + CONV_C_ROW_BLOCK] * w
            v_ref[pl.ds(pl.multiple_of(r0, CONV_C_ROW_BLOCK), CONV_C_ROW_BLOCK), lanes] = acc
        return carry

    lax.fori_loop(0, tm // CONV_C_ROW_BLOCK, conv_block, 0)
    ext_c[0:HALO_C, :] = ext_c[tm:tm + HALO_C, :]

    b_a = proj(4)
    gate_a = jax.nn.sigmoid(proj(5))
    parts = []
    for c0 in range(0, D_MODEL, CONV_LANE_CHUNK):
        w = [conv_a_ref[0, CONV_A_WIDTH - 1 - j:CONV_A_WIDTH - j, c0:c0 + CONV_LANE_CHUNK]
             for j in range(CONV_A_WIDTH)]
        _realign(ext_a, shift_ref, HALO_A, tm, c0, CONV_LANE_CHUNK, CONV_A_WIDTH - 1)
        parts.append(_causal_taps(ext_a, shift_ref, HALO_A, tm, c0, CONV_LANE_CHUNK, w))
    u_a = jnp.concatenate(parts, axis=1)
    ext_a[0:HALO_A, :] = ext_a[tm:tm + HALO_A, :]
    y_a = jnp.dot((b_a * u_a).astype(BF16), w_out_a_ref[0], preferred_element_type=F32)
    merged = gate_a * y_a

    v = _layer_norm(v_ref[...], ln_c_g_ref[0], ln_c_b_ref[0])
    v = v * jax.nn.sigmoid(v)
    y_c = jnp.dot(v.astype(BF16), w_out_c_ref[0], preferred_element_type=F32) + b_out_c_ref[0]
    merged = merged + jax.nn.sigmoid(proj(8)) * y_c

    p_in = proj(6)
    ext_p[HALO_P:HALO_P + tm, :] = p_in
    pos = tile_in_seq * tm + lax.broadcasted_iota(jnp.int32, (tm, POOL_GROUP_DIM), 0)
    parts = []
    for g, win in enumerate(POOL_WINDOWS):
        lo = g * POOL_GROUP_DIM
        sub = []
        for c0 in range(lo, lo + POOL_GROUP_DIM, CONV_LANE_CHUNK):
            _realign(ext_p, shift_ref, HALO_P, tm, c0, CONV_LANE_CHUNK, win - 1)
            sub.append(_causal_taps(ext_p, shift_ref, HALO_P, tm, c0, CONV_LANE_CHUNK,
                                    [None] * win))
        wsum = jnp.concatenate(sub, axis=1)
        cnt = jnp.minimum(pos + 1, win).astype(F32)
        pooled = wsum / cnt - p_in[:, lo:lo + POOL_GROUP_DIM]
        parts.append(jnp.dot(pooled.astype(BF16), w_pool_ref[0, g], preferred_element_type=F32))
    ext_p[0:HALO_P, :] = ext_p[tm:tm + HALO_P, :]
    y_b = jnp.concatenate(parts, axis=1) * scale_pool_ref[0]
    merged = merged + jax.nn.sigmoid(proj(7)) * y_b

    h = jnp.dot(merged.astype(BF16), w_o_ref[0], preferred_element_type=F32)
    y = _layer_norm(DEEPNORM_ALPHA * x_f32[...] + h, ln1_g_ref[0], ln1_b_ref[0])
    y_ref[...] = y
    _store_token_rows(y_rows_ref, y)
    logits_ref[...] = lax.dot_general(wr_ref[0], y.astype(BF16), (((1,), (1,)), ((), ())),
                                      preferred_element_type=F32)


def _layer_block(arr, layer, buffered_once=False):
    tail = (0,) * (arr.ndim - 1)
    mode = {"pipeline_mode": pl.Buffered(1)} if buffered_once else {}
    return pl.BlockSpec((1,) + arr.shape[1:], lambda *_: (layer,) + tail, **mode)


def _mixer(inputs, layer, consts, *, seq_len, n_tok, row0=0, combine_params=None):
    tm = TM_MIX
    assert seq_len % tm == 0 and n_tok % seq_len == 0 and row0 % tm == 0
    assert len(consts) == N_MIXER_CONSTS
    tile0 = row0 // tm
    tok_spec = pl.BlockSpec((tm, D_MODEL), lambda i: (i + tile0, 0))
    if combine_params is None:
        in_specs, args = [tok_spec], list(inputs)
    else:
        ln_g, ln_b, prev_layer = combine_params
        in_specs = [tok_spec,
                    pl.BlockSpec((TOP_K, tm * ROW_WORDS_TILES, V7X_LANES), lambda i: (0, i, 0)),
                    pl.BlockSpec((tm, GATE_LANES), lambda i: (i, 0)),
                    _layer_block(ln_g, prev_layer), _layer_block(ln_b, prev_layer)]
        args = list(inputs) + [ln_g, ln_b]
    return pl.pallas_call(
        functools.partial(_mixer_kernel, tiles_per_seq=seq_len // tm,
                          fused_combine=combine_params is not None),
        out_shape=(jax.ShapeDtypeStruct((n_tok, D_MODEL), F32),
                   jax.ShapeDtypeStruct((n_tok * ROW_WORDS_TILES, V7X_LANES), jnp.int32),
                   jax.ShapeDtypeStruct((N_EXPERTS, n_tok), F32)),
        grid=(n_tok // tm,),
        in_specs=in_specs + [_layer_block(c, layer, buffered_once=True) for c in consts],
        out_specs=(pl.BlockSpec((tm, D_MODEL), lambda i: (i, 0)),
                   pl.BlockSpec((tm * ROW_WORDS_TILES, V7X_LANES), lambda i: (i, 0)),
                   pl.BlockSpec((N_EXPERTS, tm), lambda i: (0, i))),
        scratch_shapes=[pltpu.VMEM((tm, D_MODEL), F32),
                        pltpu.VMEM((tm, D_MODEL), BF16),
                        pltpu.VMEM((HALO_A + tm, D_MODEL), F32),
                        pltpu.VMEM((HALO_P + tm, D_MODEL), F32),
                        pltpu.VMEM((HALO_C + tm, D_MODEL), F32),
                        pltpu.VMEM((V7X_SUBLANES - 1, HALO_C + tm - V7X_SUBLANES, D_MODEL), F32),
                        pltpu.VMEM((tm, D_MODEL), F32)],
        compiler_params=pltpu.CompilerParams(dimension_semantics=("arbitrary",),
                                             vmem_limit_bytes=V7X_VMEM_LIMIT_BYTES),
        name="mixer",
    )(*args, *consts)


def _router_kernel(logits_ref, br_ref, idx_ref, gate_ref, rank_ref, counts_ref, run_ref):
    tr = logits_ref.shape[1]

    @pl.when(pl.program_id(0) == 0)
    def _():
        run_ref[...] = jnp.zeros(run_ref.shape, F32)

    expert = lax.broadcasted_iota(jnp.int32, (N_EXPERTS, tr), 0)
    work = logits_ref[...] + br_ref[0]
    vals, idxs = [], []
    for _ in range(TOP_K):
        m = jnp.max(work, axis=0, keepdims=True)
        idx = jnp.min(jnp.where(work == m, expert, N_EXPERTS), axis=0, keepdims=True)
        vals.append(m)
        idxs.append(idx)
        work = jnp.where(expert == idx, NEG_BIG, work)
    exps = [jnp.exp(v - vals[0]) for v in vals]
    denom = exps[0] + exps[1] + exps[2] + exps[3]

    sel = jnp.zeros((N_EXPERTS, tr), F32)
    for k in range(TOP_K):
        sel = sel + jnp.where(expert == idxs[k], 1.0, 0.0)
    earlier = lax.broadcasted_iota(jnp.int32, (tr, tr), 0)
    later = lax.broadcasted_iota(jnp.int32, (tr, tr), 1)
    before = jnp.where(earlier < later, 1.0, 0.0).astype(BF16)
    prior = jnp.dot(sel.astype(BF16), before, preferred_element_type=F32) + run_ref[:, 0:1]

    row = lax.broadcasted_iota(jnp.int32, (V7X_SUBLANES, tr), 0)
    idx_out = jnp.zeros((V7X_SUBLANES, tr), jnp.int32)
    gate_out = jnp.zeros((V7X_SUBLANES, tr), F32)
    rank_out = jnp.zeros((V7X_SUBLANES, tr), F32)
    for k in range(TOP_K):
        rank_k = jnp.sum(jnp.where(expert == idxs[k], prior, 0.0), axis=0, keepdims=True)
        idx_out = jnp.where(row == k, idxs[k], idx_out)
        gate_out = jnp.where(row == k, exps[k] / denom, gate_out)
        rank_out = jnp.where(row == k, rank_k, rank_out)
    idx_ref[...] = idx_out
    rank_ref[...] = rank_out.astype(jnp.int32)
    gate_rows = jnp.concatenate(
        [gate_out, jnp.zeros((GATE_LANES - V7X_SUBLANES, tr), F32)], axis=0)
    gate_ref[...] = gate_rows.T

    run_ref[...] = run_ref[...] + jnp.sum(sel, axis=1, keepdims=True)
    counts_ref[...] = run_ref[...]


def _router(logits_t, layer, br_t):
    t_tok = logits_t.shape[1]
    tr = TR_ROUTE
    tok_spec = pl.BlockSpec((V7X_SUBLANES, tr), lambda i: (0, i))
    return pl.pallas_call(
        _router_kernel,
        out_shape=(jax.ShapeDtypeStruct((V7X_SUBLANES, t_tok), jnp.int32),
                   jax.ShapeDtypeStruct((t_tok, GATE_LANES), F32),
                   jax.ShapeDtypeStruct((V7X_SUBLANES, t_tok), jnp.int32),
                   jax.ShapeDtypeStruct((N_EXPERTS, V7X_LANES), F32)),
        grid=(t_tok // tr,),
        in_specs=[pl.BlockSpec((N_EXPERTS, tr), lambda i: (0, i)), _layer_block(br_t, layer)],
        out_specs=(tok_spec, pl.BlockSpec((tr, GATE_LANES), lambda i: (i, 0)), tok_spec,
                   pl.BlockSpec((N_EXPERTS, V7X_LANES), lambda i: (0, 0))),
        scratch_shapes=[pltpu.VMEM((N_EXPERTS, V7X_LANES), F32)],
        compiler_params=pltpu.CompilerParams(dimension_semantics=("arbitrary",)),
        name="router",
    )(logits_t, br_t)


def _sc_mesh():
    return plsc.VectorSubcoreMesh(core_axis_name="c", subcore_axis_name="s")


def _sc_worker_id():
    return lax.axis_index("s") * V7X_SC_CORES + lax.axis_index("c")


def _dispatch_rows(x3, dest_km, n_slots):
    t_tok = x3.shape[0]
    tok_per_worker = t_tok // V7X_SC_WORKERS
    assert tok_per_worker % SC_ROWS == 0

    def body(x_hbm, dest_hbm, out_hbm, rows_v, idx_v, sem):
        base = _sc_worker_id() * tok_per_worker

        @pl.loop(0, tok_per_worker // SC_ROWS)
        def _(step):
            t0 = pl.multiple_of(base + step * SC_ROWS, SC_ROWS)
            pltpu.sync_copy(x_hbm.at[pl.ds(t0, SC_ROWS)], rows_v)
            for k in range(TOP_K):
                pltpu.sync_copy(dest_hbm.at[k, pl.ds(t0, SC_ROWS)], idx_v.at[k])
            for k in range(TOP_K):
                pltpu.async_copy(rows_v, out_hbm.at[idx_v.at[k]], sem).wait()

    return pl.kernel(
        body, mesh=_sc_mesh(),
        out_type=jax.ShapeDtypeStruct((n_slots,) + x3.shape[1:], x3.dtype),
        scratch_types=[pltpu.VMEM((SC_ROWS,) + x3.shape[1:], x3.dtype),
                       pltpu.VMEM((TOP_K, SC_ROWS), jnp.int32),
                       pltpu.SemaphoreType.DMA],
    )(x3, dest_km)


def _gather_rows(ys3, src_rows):
    n = src_rows.shape[0]
    rows_per_worker = n // V7X_SC_WORKERS
    assert rows_per_worker % SC_ROWS == 0

    def body(ys_hbm, src_hbm, out_hbm, rows_v, idx_v, sem):
        base = _sc_worker_id() * rows_per_worker

        @pl.loop(0, rows_per_worker // SC_ROWS)
        def _(step):
            a0 = pl.multiple_of(base + step * SC_ROWS, SC_ROWS)
            pltpu.sync_copy(src_hbm.at[pl.ds(a0, SC_ROWS)], idx_v)
            pltpu.async_copy(ys_hbm.at[idx_v], rows_v, sem).wait()
            pltpu.sync_copy(rows_v, out_hbm.at[pl.ds(a0, SC_ROWS)])

    return pl.kernel(
        body, mesh=_sc_mesh(),
        out_type=jax.ShapeDtypeStruct((n,) + ys3.shape[1:], ys3.dtype),
        scratch_types=[pltpu.VMEM((SC_ROWS,) + ys3.shape[1:], ys3.dtype),
                       pltpu.VMEM((SC_ROWS,), jnp.int32),
                       pltpu.SemaphoreType.DMA],
    )(ys3, src_rows)


def _expert_kernel(block_e_ref, nb_used_ref, next_e_ref, rows_ref, xs_ref, w_gu_hbm, b_gu_ref,
                   w_down_hbm, b_down_ref, ys_ref, w_gu_f, w_down_f, w_gu_b, w_down_b, sem_gu,
                   sem_down, slot_ref, *, layer):
    bm = BM_EXPERT
    b = pl.program_id(0)
    e = block_e_ref[b]
    del nb_used_ref
    n_rows = rows_ref[b]
    live = n_rows > 0
    new_expert = jnp.logical_or(b == 0, e != block_e_ref[jnp.maximum(b - 1, 0)])

    def weight_copies(expert, slot):
        idx = layer * N_EXPERTS + expert
        return (pltpu.make_async_copy(w_gu_hbm.at[idx], w_gu_f.at[slot], sem_gu.at[slot]),
                pltpu.make_async_copy(w_down_hbm.at[idx], w_down_f.at[slot], sem_down.at[slot]))

    @pl.when(b == 0)
    def _():
        slot_ref[0] = 0
        for copy in weight_copies(e, 0):
            copy.start()

    @pl.when(jnp.logical_and(new_expert, live))
    def _():
        slot = slot_ref[0]
        nxt = next_e_ref[e]

        @pl.when(nxt >= 0)
        def _():
            for copy in weight_copies(nxt, 1 - slot):
                copy.start()

        for copy in weight_copies(e, slot):
            copy.wait()
        slot_ref[0] = 1 - slot

        cb = DEINTERLEAVE_COLS
        src = lax.broadcasted_iota(jnp.int32, (cb, cb), 0)
        dst = lax.broadcasted_iota(jnp.int32, (cb, cb), 1)
        pick = jnp.where(dst < cb // 2, 2 * dst, 2 * (dst - cb // 2) + 1)
        sel = jnp.where(src == pick, 1.0, 0.0).astype(BF16)
        for c in range(2 * D_FF // cb):
            blk = jnp.dot(w_gu_f[slot, :, c * cb:(c + 1) * cb].astype(BF16), sel,
                          preferred_element_type=F32).astype(BF16)
            lo = c * (cb // 2)
            w_gu_b[:, lo:lo + cb // 2] = blk[:, :cb // 2]
            w_gu_b[:, D_FF + lo:D_FF + lo + cb // 2] = blk[:, cb // 2:]
        w_down_b[...] = w_down_f[slot].astype(BF16)

    def expert_rows(rows):
        xb = _load_token_rows(xs_ref, rows).astype(BF16)
        h = jnp.dot(xb, w_gu_b[...], preferred_element_type=F32) + b_gu_ref[e]
        gate = jnp.minimum(h[:, :D_FF], SWIGLU_LIMIT)
        up = jnp.clip(h[:, D_FF:], -SWIGLU_LIMIT, SWIGLU_LIMIT)
        act = (up + 1.0) * (gate * jax.nn.sigmoid(SWIGLU_ALPHA * gate))
        y = jnp.dot(act.astype(BF16), w_down_b[...], preferred_element_type=F32) + b_down_ref[e]
        _store_token_rows(ys_ref, y)

    @pl.when(n_rows > bm // 2)
    def _():
        expert_rows(bm)

    @pl.when(jnp.logical_and(live, n_rows <= bm // 2))
    def _():
        expert_rows(bm // 2)
        ys_ref[bm // 2 * ROW_WORDS_TILES:, :] = jnp.zeros(
            (bm // 2 * ROW_WORDS_TILES, V7X_LANES), ys_ref.dtype)

    @pl.when(jnp.logical_not(live))
    def _():
        ys_ref[...] = jnp.zeros(ys_ref.shape, ys_ref.dtype)


def _experts(block_e, nb_used, next_e, block_rows, xs_rows, layer, w_gu, b_gu, w_down, b_down):
    bm = BM_EXPERT
    n_blocks = xs_rows.shape[0] // (bm * ROW_WORDS_TILES)

    def x_map(b, be, nb, nx, br):
        return (jnp.minimum(b, nb[0] - 1), 0)

    def layer_biases(width):
        return pl.BlockSpec((N_EXPERTS, 1, width), lambda b, be, nb, nx, br: (layer, 0, 0))

    row_block = (bm * ROW_WORDS_TILES, V7X_LANES)
    grid_spec = pltpu.PrefetchScalarGridSpec(
        num_scalar_prefetch=4,
        grid=(n_blocks,),
        in_specs=[pl.BlockSpec(row_block, x_map),
                  pl.BlockSpec(memory_space=pl.ANY),
                  layer_biases(2 * D_FF),
                  pl.BlockSpec(memory_space=pl.ANY),
                  layer_biases(D_MODEL)],
        out_specs=pl.BlockSpec(row_block, lambda b, be, nb, nx, br: (b, 0)),
        scratch_shapes=[pltpu.VMEM((2, D_MODEL, 2 * D_FF), F32),
                        pltpu.VMEM((2, D_FF, D_MODEL), F32),
                        pltpu.VMEM((D_MODEL, 2 * D_FF), BF16),
                        pltpu.VMEM((D_FF, D_MODEL), BF16),
                        pltpu.SemaphoreType.DMA((2,)),
                        pltpu.SemaphoreType.DMA((2,)),
                        pltpu.SMEM((1,), jnp.int32)],
    )
    return pl.pallas_call(
        functools.partial(_expert_kernel, layer=layer),
        out_shape=jax.ShapeDtypeStruct(xs_rows.shape, xs_rows.dtype),
        grid_spec=grid_spec,
        compiler_params=pltpu.CompilerParams(dimension_semantics=("arbitrary",),
                                             vmem_limit_bytes=V7X_VMEM_LIMIT_BYTES),
        name="experts",
    )(block_e, nb_used, next_e, block_rows, xs_rows, w_gu, b_gu, w_down, b_down)


def _combine_kernel(x_ref, yk_ref, gate_ref, g_ref, b_ref, *out_refs):
    o_ref = out_refs[-1]
    o_ref[...] = _combine_rows(x_ref[...], yk_ref, gate_ref[...], g_ref[0], b_ref[0])


def _combine(x1, yk4, gates_pad, layer, ln_g, ln_b, *, out_rows, row0, out_so_far=None):
    t_tok = x1.shape[0]
    tm = TM_COMB
    assert row0 % tm == 0
    tile0 = row0 // tm
    in_specs = [pl.BlockSpec((tm, D_MODEL), lambda i: (i, 0)),
                pl.BlockSpec((TOP_K, tm * ROW_WORDS_TILES, V7X_LANES), lambda i: (0, i, 0)),
                pl.BlockSpec((tm, GATE_LANES), lambda i: (i, 0)),
                _layer_block(ln_g, layer), _layer_block(ln_b, layer)]
    args = [x1, yk4, gates_pad, ln_g, ln_b]
    aliases = {}
    if out_so_far is not None:
        in_specs.append(pl.BlockSpec(memory_space=pl.ANY))
        args.append(out_so_far)
        aliases = {len(args) - 1: 0}
    return pl.pallas_call(
        _combine_kernel,
        out_shape=jax.ShapeDtypeStruct((out_rows, D_MODEL), F32),
        grid=(t_tok // tm,),
        in_specs=in_specs,
        out_specs=pl.BlockSpec((tm, D_MODEL), lambda i: (i + tile0, 0)),
        input_output_aliases=aliases,
        compiler_params=pltpu.CompilerParams(dimension_semantics=("arbitrary",)),
        name="combine",
    )(*args)


def _moe_rows(x1_rows, logits_t, layer, br_t, w_gu, b_gu, w_down, b_down):
    t_tok = logits_t.shape[1]
    n_assign = t_tok * TOP_K
    bm = BM_EXPERT
    n_blocks = n_assign // bm + N_EXPERTS
    n_slots = n_blocks * bm

    idx_t, gates_pad, rank_t, counts_f = _router(logits_t, layer, br_t)
    top_idx = idx_t[:TOP_K]

    counts = counts_f[:, 0].astype(jnp.int32)
    blocks_per_e = (counts + bm - 1) // bm
    blk_end = jnp.cumsum(blocks_per_e)
    row_start = (blk_end - blocks_per_e) * bm
    e_ids = jnp.arange(N_EXPERTS, dtype=jnp.int32)
    start_of = jnp.sum(jnp.where(top_idx[None] == e_ids[:, None, None],
                                 row_start[:, None, None], 0), axis=0)
    dest_km = start_of + rank_t[:TOP_K]
    block_e = jnp.minimum(
        jnp.sum(blk_end[None, :] <= jnp.arange(n_blocks, dtype=jnp.int32)[:, None], axis=1),
        N_EXPERTS - 1).astype(jnp.int32)
    nb_used = blk_end[-1:].astype(jnp.int32)
    later_with_rows = jnp.where((e_ids[None, :] > e_ids[:, None]) & (counts[None, :] > 0),
                                e_ids[None, :], N_EXPERTS)
    next_e = jnp.min(later_with_rows, axis=1)
    next_e = jnp.where(next_e == N_EXPERTS, -1, next_e).astype(jnp.int32)
    blocks = jnp.arange(n_blocks, dtype=jnp.int32)
    of_block = block_e[:, None] == e_ids[None, :]
    rows_before = (blocks - jnp.sum(jnp.where(of_block, blk_end - blocks_per_e, 0), axis=1)) * bm
    block_rows = jnp.clip(jnp.sum(jnp.where(of_block, counts, 0), axis=1) - rows_before, 0, bm)
    block_rows = jnp.where(blocks < nb_used[0], block_rows, 0).astype(jnp.int32)

    tile = (ROW_WORDS_TILES, V7X_LANES)
    xs3 = _dispatch_rows(x1_rows.reshape(t_tok, *tile), dest_km, n_slots)
    ys_rows = _experts(block_e, nb_used, next_e, block_rows,
                       xs3.reshape(n_slots * ROW_WORDS_TILES, V7X_LANES),
                       layer, w_gu, b_gu, w_down, b_down)
    yk3 = _gather_rows(ys_rows.reshape(n_slots, *tile), dest_km.reshape(n_assign))
    return yk3.reshape(TOP_K, t_tok * ROW_WORDS_TILES, V7X_LANES), gates_pad


def kernel(x, w_in, b_in, conv_a, w_out_a, w_pool, scale_pool, conv_c, conv_c_b, ln_c_g, ln_c_b,
           w_out_c, b_out_c, w_o, ln1_g, ln1_b, w_router, b_router, w_gu, b_gu, w_down, b_down,
           ln2_g, ln2_b):
    bsz, seq_len, d = x.shape
    assert d == D_MODEL
    t_tok = bsz * seq_len
    depth = w_in.shape[0]

    def row(v):
        return v[:, None, :]

    wr_t = jnp.transpose(w_router, (0, 2, 1)).astype(BF16)
    mixer_consts = (w_in.astype(BF16), row(b_in), conv_a, w_out_a.astype(BF16),
                    w_pool.astype(BF16), row(scale_pool), conv_c, row(conv_c_b), row(ln_c_g),
                    row(ln_c_b), w_out_c.astype(BF16), row(b_out_c), w_o.astype(BF16),
                    row(ln1_g), row(ln1_b), wr_t)
    br_t = jnp.broadcast_to(b_router[:, :, None], (depth, N_EXPERTS, TR_ROUTE))
    n_le = depth * N_EXPERTS
    w_gu_f = w_gu.reshape(n_le, D_MODEL, 2 * D_FF)
    b_gu_d = jnp.concatenate([b_gu[..., 0::2], b_gu[..., 1::2]], axis=-1).reshape(n_le, 1, 2 * D_FF)
    w_down_f = w_down.reshape(n_le, D_FF, D_MODEL)
    b_down_d = b_down.reshape(n_le, 1, D_MODEL)
    ln2_g_r, ln2_b_r = row(ln2_g), row(ln2_b)

    assert bsz % TOKEN_GROUPS == 0
    per_group = t_tok // TOKEN_GROUPS
    x2 = x.reshape(t_tok, d)
    state = [None] * TOKEN_GROUPS
    for layer in range(depth):
        for g in range(TOKEN_GROUPS):
            if layer == 0:
                x1, x1_rows, logits_t = _mixer((x2,), layer, mixer_consts, seq_len=seq_len,
                                               n_tok=per_group, row0=g * per_group)
            else:
                x1, x1_rows, logits_t = _mixer(state[g], layer, mixer_consts, seq_len=seq_len,
                                               n_tok=per_group,
                                               combine_params=(ln2_g_r, ln2_b_r, layer - 1))
            yk4, gates = _moe_rows(x1_rows, logits_t, layer, br_t, w_gu_f, b_gu_d, w_down_f,
                                   b_down_d)
            state[g] = (x1, yk4, gates)
    out = None
    for g, (x1, yk4, gates) in enumerate(state):
        out = _combine(x1, yk4, gates, depth - 1, ln2_g_r, ln2_b_r, out_rows=t_tok,
                       row0=g * per_group, out_so_far=out)
    return out.reshape(bsz, seq_len, d)
```

```python
import functools

import jax
import jax.numpy as jnp
from jax import lax
from jax.experimental import pallas as pl
from jax.experimental.pallas import tpu as pltpu
from jax.experimental.pallas import tpu_sc as plsc

D_MODEL = 1024
DEPTH = 4
CONV_A_WIDTH = 3
POOL_WINDOWS = (2, 4, 8, 16)
POOL_GROUP_DIM = D_MODEL // len(POOL_WINDOWS)
CONV_C_WIDTH = 31
N_EXPERTS = 32
TOP_K = 4
D_FF = D_MODEL
SWIGLU_LIMIT = 7.0
SWIGLU_ALPHA = 1.702
LN_EPS = 1e-5
DEEPNORM_ALPHA = (2.0 * DEPTH) ** 0.25

V7X_LANES = 128
V7X_SUBLANES = 8
V7X_VMEM_LIMIT_BYTES = 56 * 1024 * 1024
V7X_MXU_DIM = 256
V7X_SC_CORES = 2
V7X_SC_SUBCORES = 16
V7X_SC_WORKERS = V7X_SC_CORES * V7X_SC_SUBCORES

TM_MIX = 256
HALO_A = 8
HALO_P = 16
HALO_C = 32
CONV_LANE_CHUNK = 128
TAP_ROW_BLOCK = 64
CONV_C_ROW_BLOCK = 64
MIX_COL_ORDER = (4, 5, 1, 2, 0, 6, 3, 7, 8)
TR_ROUTE = 512
GATE_LANES = V7X_LANES
BM_EXPERT = 256
TM_COMB = 256
DEINTERLEAVE_COLS = V7X_MXU_DIM
ROW_WORDS_TILES = D_MODEL // 2 // V7X_LANES
SC_ROWS = 32
SLOT_TABLE_LANES = 512
TOKEN_GROUPS = 2
NEG_BIG = -3.0e38

F32 = jnp.float32
BF16 = jnp.bfloat16


def _layer_norm(x, g, b):
    mu = jnp.mean(x, axis=-1, keepdims=True)
    xc = x - mu
    var = jnp.mean(xc * xc, axis=-1, keepdims=True)
    return xc * lax.rsqrt(var + LN_EPS) * g + b


def _store_token_rows(ref, val):
    rows = val.shape[0]
    words = pltpu.pack_elementwise([val[:, :D_MODEL // 2], val[:, D_MODEL // 2:]],
                                   packed_dtype=BF16)
    for g in range(ROW_WORDS_TILES):
        ref[pl.ds(g, rows, stride=ROW_WORDS_TILES), :] = words[:, g * V7X_LANES:(g + 1) * V7X_LANES]


def _load_token_rows(ref, rows):
    words = jnp.concatenate([ref[pl.ds(g, rows, stride=ROW_WORDS_TILES), :]
                             for g in range(ROW_WORDS_TILES)], axis=1)
    halves = [pltpu.unpack_elementwise(words, index=i, packed_dtype=BF16, unpacked_dtype=F32)
              for i in range(2)]
    return jnp.concatenate(halves, axis=1)


def _combine_rows(x1, yk_ref, gates, ln_g, ln_b):
    rows = x1.shape[0]
    m = jnp.zeros((rows, D_MODEL), F32)
    for k in range(TOP_K):
        m = m + gates[:, k:k + 1] * _load_token_rows(yk_ref.at[k], rows)
    return _layer_norm(DEEPNORM_ALPHA * x1 + m, ln_g, ln_b)


def _realign(ext_ref, shift_ref, halo, rows, lane0, lanes, max_shift):
    n = halo + rows - V7X_SUBLANES
    for b in range(1, min(max_shift, V7X_SUBLANES - 1) + 1):
        shift_ref[b - 1, 0:n, lane0:lane0 + lanes] = (
            ext_ref[V7X_SUBLANES - b:V7X_SUBLANES - b + n, lane0:lane0 + lanes])


def _causal_taps(ext_ref, shift_ref, halo, rows, lane0, lanes, weights):
    outs = []
    for r0 in range(0, rows, TAP_ROW_BLOCK):
        acc = None
        for j, w in enumerate(weights):
            a, b = divmod(j, V7X_SUBLANES)
            if b == 0:
                start = halo - V7X_SUBLANES * a + r0
                term = ext_ref[start:start + TAP_ROW_BLOCK, lane0:lane0 + lanes]
            else:
                start = halo - V7X_SUBLANES * (a + 1) + r0
                term = shift_ref[b - 1, start:start + TAP_ROW_BLOCK, lane0:lane0 + lanes]
            if w is not None:
                term = term * w
            acc = term if acc is None else acc + term
        outs.append(acc)
    return jnp.concatenate(outs, axis=0)


N_MIXER_CONSTS = 16


def _mixer_kernel(*refs, tiles_per_seq, fused_combine):
    n_in = 5 if fused_combine else 1
    inputs, refs = refs[:n_in], refs[n_in:]
    (w_in_ref, b_in_ref, conv_a_ref, w_out_a_ref, w_pool_ref, scale_pool_ref, conv_c_ref,
     conv_c_b_ref, ln_c_g_ref, ln_c_b_ref, w_out_c_ref, b_out_c_ref, w_o_ref, ln1_g_ref,
     ln1_b_ref, wr_ref) = refs[:N_MIXER_CONSTS]
    (y_ref, y_rows_ref, logits_ref,
     x_f32, xb_ref, ext_a, ext_p, ext_c, shift_ref, v_ref) = refs[N_MIXER_CONSTS:]
    tm = y_ref.shape[0]
    tile_in_seq = pl.program_id(0) % tiles_per_seq

    @pl.when(tile_in_seq == 0)
    def _():
        ext_a[0:HALO_A, :] = jnp.zeros((HALO_A, D_MODEL), F32)
        ext_p[0:HALO_P, :] = jnp.zeros((HALO_P, D_MODEL), F32)
        ext_c[0:HALO_C, :] = jnp.zeros((HALO_C, D_MODEL), F32)

    if fused_combine:
        x_prev_ref, yk_ref, gate_ref, ln2_g_ref, ln2_b_ref = inputs
        x_f32[...] = _combine_rows(x_prev_ref[...], yk_ref, gate_ref[...], ln2_g_ref[0],
                                   ln2_b_ref[0])
    else:
        x_f32[...] = inputs[0][...]
    xb_ref[...] = x_f32[...].astype(BF16)

    def proj(slot):
        lo = MIX_COL_ORDER[slot] * D_MODEL
        return (jnp.dot(xb_ref[...], w_in_ref[0, :, lo:lo + D_MODEL], preferred_element_type=F32)
                + b_in_ref[0, :, lo:lo + D_MODEL])

    ext_c[HALO_C:HALO_C + tm, :] = proj(0) * jax.nn.sigmoid(proj(1))
    ext_a[HALO_A:HALO_A + tm, :] = proj(2) * proj(3)
    _realign(ext_c, shift_ref, HALO_C, tm, 0, D_MODEL, CONV_C_WIDTH - 1)
    taps_of_copy = {}
    for j in range(CONV_C_WIDTH):
        a, b = divmod(j, V7X_SUBLANES)
        taps_of_copy.setdefault(b, []).append((a, j))

    def conv_block(i, carry):
        r0 = i * CONV_C_ROW_BLOCK
        for c0 in range(0, D_MODEL, CONV_LANE_CHUNK):
            lanes = slice(c0, c0 + CONV_LANE_CHUNK)
            acc = conv_c_b_ref[0, :, lanes]
            for b, taps in taps_of_copy.items():
                a_max = max(a for a, _ in taps)
                first = HALO_C - V7X_SUBLANES * (a_max + (1 if b else 0))
                start = pl.multiple_of(r0 + first, V7X_SUBLANES)
                span = CONV_C_ROW_BLOCK + V7X_SUBLANES * a_max
                if b == 0:
                    win = ext_c[pl.ds(start, span), lanes]
                else:
                    win = shift_ref[b - 1, pl.ds(start, span), lanes]
                for a, j in taps:
                    off = V7X_SUBLANES * (a_max - a)
                    w = conv_c_ref[0, CONV_C_WIDTH - 1 - j:CONV_C_WIDTH - j, lanes]
                    acc = acc + win[off:off + CONV_C_ROW_BLOCK] * w
            v_ref[pl.ds(pl.multiple_of(r0, CONV_C_ROW_BLOCK), CONV_C_ROW_BLOCK), lanes] = acc
        return carry

    lax.fori_loop(0, tm // CONV_C_ROW_BLOCK, conv_block, 0)
    ext_c[0:HALO_C, :] = ext_c[tm:tm + HALO_C, :]

    b_a = proj(4)
    gate_a = jax.nn.sigmoid(proj(5))
    parts = []
    for c0 in range(0, D_MODEL, CONV_LANE_CHUNK):
        w = [conv_a_ref[0, CONV_A_WIDTH - 1 - j:CONV_A_WIDTH - j, c0:c0 + CONV_LANE_CHUNK]
             for j in range(CONV_A_WIDTH)]
        _realign(ext_a, shift_ref, HALO_A, tm, c0, CONV_LANE_CHUNK, CONV_A_WIDTH - 1)
        parts.append(_causal_taps(ext_a, shift_ref, HALO_A, tm, c0, CONV_LANE_CHUNK, w))
    u_a = jnp.concatenate(parts, axis=1)
    ext_a[0:HALO_A, :] = ext_a[tm:tm + HALO_A, :]
    y_a = jnp.dot((b_a * u_a).astype(BF16), w_out_a_ref[0], preferred_element_type=F32)
    merged = gate_a * y_a

    v = _layer_norm(v_ref[...], ln_c_g_ref[0], ln_c_b_ref[0])
    v = v * jax.nn.sigmoid(v)
    y_c = jnp.dot(v.astype(BF16), w_out_c_ref[0], preferred_element_type=F32) + b_out_c_ref[0]
    merged = merged + jax.nn.sigmoid(proj(8)) * y_c

    p_in = proj(6)
    ext_p[HALO_P:HALO_P + tm, :] = p_in
    pos = tile_in_seq * tm + lax.broadcasted_iota(jnp.int32, (tm, POOL_GROUP_DIM), 0)
    parts = []
    for g, win in enumerate(POOL_WINDOWS):
        lo = g * POOL_GROUP_DIM
        sub = []
        for c0 in range(lo, lo + POOL_GROUP_DIM, CONV_LANE_CHUNK):
            _realign(ext_p, shift_ref, HALO_P, tm, c0, CONV_LANE_CHUNK, win - 1)
            sub.append(_causal_taps(ext_p, shift_ref, HALO_P, tm, c0, CONV_LANE_CHUNK,
                                    [None] * win))
        wsum = jnp.concatenate(sub, axis=1)
        cnt = jnp.minimum(pos + 1, win).astype(F32)
        pooled = wsum / cnt - p_in[:, lo:lo + POOL_GROUP_DIM]
        parts.append(jnp.dot(pooled.astype(BF16), w_pool_ref[0, g], preferred_element_type=F32))
    ext_p[0:HALO_P, :] = ext_p[tm:tm + HALO_P, :]
    y_b = jnp.concatenate(parts, axis=1) * scale_pool_ref[0]
    merged = merged + jax.nn.sigmoid(proj(7)) * y_b

    h = jnp.dot(merged.astype(BF16), w_o_ref[0], preferred_element_type=F32)
    y = _layer_norm(DEEPNORM_ALPHA * x_f32[...] + h, ln1_g_ref[0], ln1_b_ref[0])
    y_ref[...] = y
    _store_token_rows(y_rows_ref, y)
    logits_ref[...] = lax.dot_general(wr_ref[0], y.astype(BF16), (((1,), (1,)), ((), ())),
                                      preferred_element_type=F32)


def _layer_block(arr, layer, buffered_once=False):
    tail = (0,) * (arr.ndim - 1)
    mode = {"pipeline_mode": pl.Buffered(1)} if buffered_once else {}
    return pl.BlockSpec((1,) + arr.shape[1:], lambda *_: (layer,) + tail, **mode)


def _mixer(inputs, layer, consts, *, seq_len, n_tok, row0=0, combine_params=None):
    tm = TM_MIX
    assert seq_len % tm == 0 and n_tok % seq_len == 0 and row0 % tm == 0
    assert len(consts) == N_MIXER_CONSTS
    tile0 = row0 // tm
    tok_spec = pl.BlockSpec((tm, D_MODEL), lambda i: (i + tile0, 0))
    if combine_params is None:
        in_specs, args = [tok_spec], list(inputs)
    else:
        ln_g, ln_b, prev_layer = combine_params
        in_specs = [tok_spec,
                    pl.BlockSpec((TOP_K, tm * ROW_WORDS_TILES, V7X_LANES), lambda i: (0, i, 0)),
                    pl.BlockSpec((tm, GATE_LANES), lambda i: (i, 0)),
                    _layer_block(ln_g, prev_layer), _layer_block(ln_b, prev_layer)]
        args = list(inputs) + [ln_g, ln_b]
    return pl.pallas_call(
        functools.partial(_mixer_kernel, tiles_per_seq=seq_len // tm,
                          fused_combine=combine_params is not None),
        out_shape=(jax.ShapeDtypeStruct((n_tok, D_MODEL), F32),
                   jax.ShapeDtypeStruct((n_tok * ROW_WORDS_TILES, V7X_LANES), jnp.int32),
                   jax.ShapeDtypeStruct((N_EXPERTS, n_tok), F32)),
        grid=(n_tok // tm,),
        in_specs=in_specs + [_layer_block(c, layer, buffered_once=True) for c in consts],
        out_specs=(pl.BlockSpec((tm, D_MODEL), lambda i: (i, 0)),
                   pl.BlockSpec((tm * ROW_WORDS_TILES, V7X_LANES), lambda i: (i, 0)),
                   pl.BlockSpec((N_EXPERTS, tm), lambda i: (0, i))),
        scratch_shapes=[pltpu.VMEM((tm, D_MODEL), F32),
                        pltpu.VMEM((tm, D_MODEL), BF16),
                        pltpu.VMEM((HALO_A + tm, D_MODEL), F32),
                        pltpu.VMEM((HALO_P + tm, D_MODEL), F32),
                        pltpu.VMEM((HALO_C + tm, D_MODEL), F32),
                        pltpu.VMEM((V7X_SUBLANES - 1, HALO_C + tm - V7X_SUBLANES, D_MODEL), F32),
                        pltpu.VMEM((tm, D_MODEL), F32)],
        compiler_params=pltpu.CompilerParams(dimension_semantics=("arbitrary",),
                                             vmem_limit_bytes=V7X_VMEM_LIMIT_BYTES),
        name="mixer",
    )(*args, *consts)


def _router_kernel(logits_ref, br_ref, idx_ref, gate_ref, rank_ref, counts_ref, counts_row_ref,
                   run_ref, run_row_ref):
    tr = logits_ref.shape[1]

    @pl.when(pl.program_id(0) == 0)
    def _():
        run_ref[...] = jnp.zeros(run_ref.shape, F32)
        run_row_ref[...] = jnp.zeros(run_row_ref.shape, F32)

    expert = lax.broadcasted_iota(jnp.int32, (N_EXPERTS, tr), 0)
    work = logits_ref[...] + br_ref[0]
    vals, idxs = [], []
    for _ in range(TOP_K):
        m = jnp.max(work, axis=0, keepdims=True)
        idx = jnp.min(jnp.where(work == m, expert, N_EXPERTS), axis=0, keepdims=True)
        vals.append(m)
        idxs.append(idx)
        work = jnp.where(expert == idx, NEG_BIG, work)
    exps = [jnp.exp(v - vals[0]) for v in vals]
    denom = exps[0] + exps[1] + exps[2] + exps[3]

    sel = jnp.zeros((N_EXPERTS, tr), F32)
    for k in range(TOP_K):
        sel = sel + jnp.where(expert == idxs[k], 1.0, 0.0)
    earlier = lax.broadcasted_iota(jnp.int32, (tr, tr), 0)
    later = lax.broadcasted_iota(jnp.int32, (tr, tr), 1)
    before = jnp.where(earlier < later, 1.0, 0.0).astype(BF16)
    prior = jnp.dot(sel.astype(BF16), before, preferred_element_type=F32) + run_ref[:, 0:1]

    row = lax.broadcasted_iota(jnp.int32, (V7X_SUBLANES, tr), 0)
    idx_out = jnp.zeros((V7X_SUBLANES, tr), jnp.int32)
    gate_out = jnp.zeros((V7X_SUBLANES, tr), F32)
    rank_out = jnp.zeros((V7X_SUBLANES, tr), F32)
    for k in range(TOP_K):
        rank_k = jnp.sum(jnp.where(expert == idxs[k], prior, 0.0), axis=0, keepdims=True)
        idx_out = jnp.where(row == k, idxs[k], idx_out)
        gate_out = jnp.where(row == k, exps[k] / denom, gate_out)
        rank_out = jnp.where(row == k, rank_k, rank_out)
    idx_ref[...] = idx_out
    rank_ref[...] = rank_out.astype(jnp.int32)
    gate_rows = jnp.concatenate(
        [gate_out, jnp.zeros((GATE_LANES - V7X_SUBLANES, tr), F32)], axis=0)
    gate_ref[...] = gate_rows.T

    run_ref[...] = run_ref[...] + jnp.sum(sel, axis=1, keepdims=True)
    counts_ref[...] = run_ref[...]
    per_tile = lax.dot_general(jnp.ones((V7X_SUBLANES, tr), BF16), sel.astype(BF16),
                               (((1,), (1,)), ((), ())), preferred_element_type=F32)
    run_row_ref[:, 0:N_EXPERTS] = run_row_ref[:, 0:N_EXPERTS] + per_tile
    counts_row_ref[...] = run_row_ref[...]


def _router(logits_t, layer, br_t):
    t_tok = logits_t.shape[1]
    tr = TR_ROUTE
    tok_spec = pl.BlockSpec((V7X_SUBLANES, tr), lambda i: (0, i))
    return pl.pallas_call(
        _router_kernel,
        out_shape=(jax.ShapeDtypeStruct((V7X_SUBLANES, t_tok), jnp.int32),
                   jax.ShapeDtypeStruct((t_tok, GATE_LANES), F32),
                   jax.ShapeDtypeStruct((V7X_SUBLANES, t_tok), jnp.int32),
                   jax.ShapeDtypeStruct((N_EXPERTS, V7X_LANES), F32),
                   jax.ShapeDtypeStruct((V7X_SUBLANES, V7X_LANES), F32)),
        grid=(t_tok // tr,),
        in_specs=[pl.BlockSpec((N_EXPERTS, tr), lambda i: (0, i)), _layer_block(br_t, layer)],
        out_specs=(tok_spec, pl.BlockSpec((tr, GATE_LANES), lambda i: (i, 0)), tok_spec,
                   pl.BlockSpec((N_EXPERTS, V7X_LANES), lambda i: (0, 0)),
                   pl.BlockSpec((V7X_SUBLANES, V7X_LANES), lambda i: (0, 0))),
        scratch_shapes=[pltpu.VMEM((N_EXPERTS, V7X_LANES), F32),
                        pltpu.VMEM((V7X_SUBLANES, V7X_LANES), F32)],
        compiler_params=pltpu.CompilerParams(dimension_semantics=("arbitrary",)),
        name="router",
    )(logits_t, br_t)


def _slots_kernel(idx_ref, rank_ref, counts_ref, counts_row_ref, dest_ref, table_ref, *,
                  n_blocks):
    bm = BM_EXPERT
    shift = bm.bit_length() - 1
    assert bm == 1 << shift
    width = table_ref.shape[1]
    cnt_col = counts_ref[:, 0:1].astype(jnp.int32)
    cnt_row = counts_row_ref[0:1, :].astype(jnp.int32)
    nblk_col = lax.shift_right_logical(cnt_col + (bm - 1), shift)
    nblk_row = lax.shift_right_logical(cnt_row + (bm - 1), shift)
    e_sub = lax.broadcasted_iota(jnp.int32, (N_EXPERTS, V7X_LANES), 0)
    e_lane = lax.broadcasted_iota(jnp.int32, (N_EXPERTS, V7X_LANES), 1)
    blk_start = jnp.sum(jnp.where(e_lane < e_sub, nblk_row.astype(F32), 0.0), axis=1,
                        keepdims=True).astype(jnp.int32)
    blk_end = blk_start + nblk_col
    row_start = blk_start * bm

    dest = rank_ref[...]
    idx = idx_ref[...]
    for e in range(N_EXPERTS):
        dest = dest + jnp.where(idx == e, row_start[e:e + 1, :], 0)
    dest_ref[...] = dest

    b_lane = lax.broadcasted_iota(jnp.int32, (N_EXPERTS, width), 1)
    e_of = lax.broadcasted_iota(jnp.int32, (N_EXPERTS, width), 0)
    block_e = jnp.minimum(jnp.sum(jnp.where(blk_end <= b_lane, 1, 0), axis=0, keepdims=True),
                          N_EXPERTS - 1)
    mine = e_of == block_e
    cnt_of = jnp.sum(jnp.where(mine, cnt_col, 0), axis=0, keepdims=True)
    start_of = jnp.sum(jnp.where(mine, blk_start, 0), axis=0, keepdims=True)
    nb_used = jnp.sum(nblk_col, axis=0, keepdims=True)
    b_row = b_lane[0:1, :]
    block_rows = jnp.clip(cnt_of - (b_row - start_of) * bm, 0, bm)
    block_rows = jnp.where((b_row < nb_used) & (b_row < n_blocks), block_rows, 0)
    later = jnp.where((e_sub > e_lane) & (cnt_col > 0), e_sub, N_EXPERTS)
    next_e = jnp.min(later, axis=0, keepdims=True)
    next_e = jnp.where(next_e == N_EXPERTS, -1, next_e)

    row = lax.broadcasted_iota(jnp.int32, (V7X_SUBLANES, width), 0)
    lane = lax.broadcasted_iota(jnp.int32, (V7X_SUBLANES, width), 1)
    next_wide = jnp.concatenate(
        [next_e, jnp.full((1, width - V7X_LANES), -1, jnp.int32)], axis=1)
    table = jnp.where(row == 0, block_e, 0)
    table = jnp.where(row == 1, block_rows, table)
    table = jnp.where(row == 2, next_wide, table)
    table = jnp.where((row == 3) & (lane == 0), nb_used, table)
    table_ref[...] = table


def _slots(idx_t, rank_t, counts_col, counts_row, n_blocks):
    assert n_blocks <= SLOT_TABLE_LANES
    return pl.pallas_call(
        functools.partial(_slots_kernel, n_blocks=n_blocks),
        out_shape=(jax.ShapeDtypeStruct(idx_t.shape, jnp.int32),
                   jax.ShapeDtypeStruct((V7X_SUBLANES, SLOT_TABLE_LANES), jnp.int32)),
        name="slots",
    )(idx_t, rank_t, counts_col, counts_row)


def _sc_mesh():
    return plsc.VectorSubcoreMesh(core_axis_name="c", subcore_axis_name="s")


def _sc_worker_id():
    return lax.axis_index("s") * V7X_SC_CORES + lax.axis_index("c")


def _dispatch_rows(x3, dest_km, n_slots):
    t_tok = x3.shape[0]
    tok_per_worker = t_tok // V7X_SC_WORKERS
    assert tok_per_worker % SC_ROWS == 0

    def body(x_hbm, dest_hbm, out_hbm, rows_v, idx_v, sem):
        base = _sc_worker_id() * tok_per_worker

        @pl.loop(0, tok_per_worker // SC_ROWS)
        def _(step):
            t0 = pl.multiple_of(base + step * SC_ROWS, SC_ROWS)
            pltpu.sync_copy(x_hbm.at[pl.ds(t0, SC_ROWS)], rows_v)
            for k in range(TOP_K):
                pltpu.sync_copy(dest_hbm.at[k, pl.ds(t0, SC_ROWS)], idx_v.at[k])
            for k in range(TOP_K):
                pltpu.async_copy(rows_v, out_hbm.at[idx_v.at[k]], sem).wait()

    return pl.kernel(
        body, mesh=_sc_mesh(),
        out_type=jax.ShapeDtypeStruct((n_slots,) + x3.shape[1:], x3.dtype),
        scratch_types=[pltpu.VMEM((SC_ROWS,) + x3.shape[1:], x3.dtype),
                       pltpu.VMEM((TOP_K, SC_ROWS), jnp.int32),
                       pltpu.SemaphoreType.DMA],
    )(x3, dest_km)


def _gather_rows(ys3, dest_km):
    t_tok = dest_km.shape[1]
    workers_per_k = V7X_SC_WORKERS // TOP_K
    tok_per_worker = t_tok // workers_per_k
    assert V7X_SC_WORKERS % TOP_K == 0 and tok_per_worker % SC_ROWS == 0

    def body(ys_hbm, src_hbm, out_hbm, rows_v, idx_v, sem):
        wid = _sc_worker_id()
        k = wid // workers_per_k
        base = (wid % workers_per_k) * tok_per_worker

        @pl.loop(0, tok_per_worker // SC_ROWS)
        def _(step):
            t0 = pl.multiple_of(base + step * SC_ROWS, SC_ROWS)
            pltpu.sync_copy(src_hbm.at[k, pl.ds(t0, SC_ROWS)], idx_v)
            pltpu.async_copy(ys_hbm.at[idx_v], rows_v, sem).wait()
            pltpu.sync_copy(rows_v, out_hbm.at[pl.ds(pl.multiple_of(k * t_tok + t0, SC_ROWS),
                                                     SC_ROWS)])

    return pl.kernel(
        body, mesh=_sc_mesh(),
        out_type=jax.ShapeDtypeStruct((TOP_K * t_tok,) + ys3.shape[1:], ys3.dtype),
        scratch_types=[pltpu.VMEM((SC_ROWS,) + ys3.shape[1:], ys3.dtype),
                       pltpu.VMEM((SC_ROWS,), jnp.int32),
                       pltpu.SemaphoreType.DMA],
    )(ys3, dest_km)


def _expert_kernel(table_ref, xs_ref, w_gu_hbm, b_gu_ref, w_down_hbm, b_down_ref, ys_ref,
                   w_gu_f, w_down_f, w_gu_b, w_down_b, sem_gu, sem_down, slot_ref, *, layer):
    bm = BM_EXPERT
    b = pl.program_id(0)
    e = table_ref[0, b]
    n_rows = table_ref[1, b]
    live = n_rows > 0
    new_expert = jnp.logical_or(b == 0, e != table_ref[0, jnp.maximum(b - 1, 0)])

    def weight_copies(expert, slot):
        idx = layer * N_EXPERTS + expert
        return (pltpu.make_async_copy(w_gu_hbm.at[idx], w_gu_f.at[slot], sem_gu.at[slot]),
                pltpu.make_async_copy(w_down_hbm.at[idx], w_down_f.at[slot], sem_down.at[slot]))

    @pl.when(b == 0)
    def _():
        slot_ref[0] = 0
        for copy in weight_copies(e, 0):
            copy.start()

    @pl.when(jnp.logical_and(new_expert, live))
    def _():
        slot = slot_ref[0]
        nxt = table_ref[2, e]

        @pl.when(nxt >= 0)
        def _():
            for copy in weight_copies(nxt, 1 - slot):
                copy.start()

        for copy in weight_copies(e, slot):
            copy.wait()
        slot_ref[0] = 1 - slot

        cb = DEINTERLEAVE_COLS
        src = lax.broadcasted_iota(jnp.int32, (cb, cb), 0)
        dst = lax.broadcasted_iota(jnp.int32, (cb, cb), 1)
        pick = jnp.where(dst < cb // 2, 2 * dst, 2 * (dst - cb // 2) + 1)
        sel = jnp.where(src == pick, 1.0, 0.0).astype(BF16)
        for c in range(2 * D_FF // cb):
            blk = jnp.dot(w_gu_f[slot, :, c * cb:(c + 1) * cb].astype(BF16), sel,
                          preferred_element_type=F32).astype(BF16)
            lo = c * (cb // 2)
            w_gu_b[:, lo:lo + cb // 2] = blk[:, :cb // 2]
            w_gu_b[:, D_FF + lo:D_FF + lo + cb // 2] = blk[:, cb // 2:]
        w_down_b[...] = w_down_f[slot].astype(BF16)

    def expert_rows(rows):
        xb = _load_token_rows(xs_ref, rows).astype(BF16)
        h = jnp.dot(xb, w_gu_b[...], preferred_element_type=F32) + b_gu_ref[e]
        gate = jnp.minimum(h[:, :D_FF], SWIGLU_LIMIT)
        up = jnp.clip(h[:, D_FF:], -SWIGLU_LIMIT, SWIGLU_LIMIT)
        act = (up + 1.0) * (gate * jax.nn.sigmoid(SWIGLU_ALPHA * gate))
        y = jnp.dot(act.astype(BF16), w_down_b[...], preferred_element_type=F32) + b_down_ref[e]
        _store_token_rows(ys_ref, y)

    @pl.when(n_rows > bm // 2)
    def _():
        expert_rows(bm)

    @pl.when(jnp.logical_and(live, n_rows <= bm // 2))
    def _():
        expert_rows(bm // 2)
        ys_ref[bm // 2 * ROW_WORDS_TILES:, :] = jnp.zeros(
            (bm // 2 * ROW_WORDS_TILES, V7X_LANES), ys_ref.dtype)

    @pl.when(jnp.logical_not(live))
    def _():
        ys_ref[...] = jnp.zeros(ys_ref.shape, ys_ref.dtype)


def _experts(table, xs_rows, layer, w_gu, b_gu, w_down, b_down):
    bm = BM_EXPERT
    n_blocks = xs_rows.shape[0] // (bm * ROW_WORDS_TILES)

    def x_map(b, tbl):
        return (jnp.minimum(b, tbl[3, 0] - 1), 0)

    def layer_biases(width):
        return pl.BlockSpec((N_EXPERTS, 1, width), lambda b, tbl: (layer, 0, 0))

    row_block = (bm * ROW_WORDS_TILES, V7X_LANES)
    grid_spec = pltpu.PrefetchScalarGridSpec(
        num_scalar_prefetch=1,
        grid=(n_blocks,),
        in_specs=[pl.BlockSpec(row_block, x_map),
                  pl.BlockSpec(memory_space=pl.ANY),
                  layer_biases(2 * D_FF),
                  pl.BlockSpec(memory_space=pl.ANY),
                  layer_biases(D_MODEL)],
        out_specs=pl.BlockSpec(row_block, lambda b, tbl: (b, 0)),
        scratch_shapes=[pltpu.VMEM((2, D_MODEL, 2 * D_FF), F32),
                        pltpu.VMEM((2, D_FF, D_MODEL), F32),
                        pltpu.VMEM((D_MODEL, 2 * D_FF), BF16),
                        pltpu.VMEM((D_FF, D_MODEL), BF16),
                        pltpu.SemaphoreType.DMA((2,)),
                        pltpu.SemaphoreType.DMA((2,)),
                        pltpu.SMEM((1,), jnp.int32)],
    )
    return pl.pallas_call(
        functools.partial(_expert_kernel, layer=layer),
        out_shape=jax.ShapeDtypeStruct(xs_rows.shape, xs_rows.dtype),
        grid_spec=grid_spec,
        compiler_params=pltpu.CompilerParams(dimension_semantics=("arbitrary",),
                                             vmem_limit_bytes=V7X_VMEM_LIMIT_BYTES),
        name="experts",
    )(table, xs_rows, w_gu, b_gu, w_down, b_down)


def _combine_kernel(x_ref, yk_ref, gate_ref, g_ref, b_ref, *out_refs):
    o_ref = out_refs[-1]
    o_ref[...] = _combine_rows(x_ref[...], yk_ref, gate_ref[...], g_ref[0], b_ref[0])


def _combine(x1, yk4, gates_pad, layer, ln_g, ln_b, *, out_rows, row0, out_so_far=None):
    t_tok = x1.shape[0]
    tm = TM_COMB
    assert row0 % tm == 0
    tile0 = row0 // tm
    in_specs = [pl.BlockSpec((tm, D_MODEL), lambda i: (i, 0)),
                pl.BlockSpec((TOP_K, tm * ROW_WORDS_TILES, V7X_LANES), lambda i: (0, i, 0)),
                pl.BlockSpec((tm, GATE_LANES), lambda i: (i, 0)),
                _layer_block(ln_g, layer), _layer_block(ln_b, layer)]
    args = [x1, yk4, gates_pad, ln_g, ln_b]
    aliases = {}
    if out_so_far is not None:
        in_specs.append(pl.BlockSpec(memory_space=pl.ANY))
        args.append(out_so_far)
        aliases = {len(args) - 1: 0}
    return pl.pallas_call(
        _combine_kernel,
        out_shape=jax.ShapeDtypeStruct((out_rows, D_MODEL), F32),
        grid=(t_tok // tm,),
        in_specs=in_specs,
        out_specs=pl.BlockSpec((tm, D_MODEL), lambda i: (i + tile0, 0)),
        input_output_aliases=aliases,
        compiler_params=pltpu.CompilerParams(dimension_semantics=("arbitrary",)),
        name="combine",
    )(*args)


def _moe_rows(x1_rows, logits_t, layer, br_t, w_gu, b_gu, w_down, b_down):
    t_tok = logits_t.shape[1]
    n_assign = t_tok * TOP_K
    bm = BM_EXPERT
    n_blocks = n_assign // bm + N_EXPERTS
    n_slots = n_blocks * bm

    idx_t, gates_pad, rank_t, counts_col, counts_row = _router(logits_t, layer, br_t)
    dest_km, table = _slots(idx_t, rank_t, counts_col, counts_row, n_blocks)

    tile = (ROW_WORDS_TILES, V7X_LANES)
    xs3 = _dispatch_rows(x1_rows.reshape(t_tok, *tile), dest_km, n_slots)
    ys_rows = _experts(table, xs3.reshape(n_slots * ROW_WORDS_TILES, V7X_LANES),
                       layer, w_gu, b_gu, w_down, b_down)
    yk3 = _gather_rows(ys_rows.reshape(n_slots, *tile), dest_km)
    return yk3.reshape(TOP_K, t_tok * ROW_WORDS_TILES, V7X_LANES), gates_pad


def kernel(x, w_in, b_in, conv_a, w_out_a, w_pool, scale_pool, conv_c, conv_c_b, ln_c_g, ln_c_b,
           w_out_c, b_out_c, w_o, ln1_g, ln1_b, w_router, b_router, w_gu, b_gu, w_down, b_down,
           ln2_g, ln2_b):
    bsz, seq_len, d = x.shape
    assert d == D_MODEL
    t_tok = bsz * seq_len
    depth = w_in.shape[0]

    def row(v):
        return v[:, None, :]

    wr_t = jnp.transpose(w_router, (0, 2, 1)).astype(BF16)
    mixer_consts = (w_in.astype(BF16), row(b_in), conv_a, w_out_a.astype(BF16),
                    w_pool.astype(BF16), row(scale_pool), conv_c, row(conv_c_b), row(ln_c_g),
                    row(ln_c_b), w_out_c.astype(BF16), row(b_out_c), w_o.astype(BF16),
                    row(ln1_g), row(ln1_b), wr_t)
    br_t = jnp.broadcast_to(b_router[:, :, None], (depth, N_EXPERTS, TR_ROUTE))
    n_le = depth * N_EXPERTS
    w_gu_f = w_gu.reshape(n_le, D_MODEL, 2 * D_FF)
    b_gu_d = jnp.concatenate([b_gu[..., 0::2], b_gu[..., 1::2]], axis=-1).reshape(n_le, 1, 2 * D_FF)
    w_down_f = w_down.reshape(n_le, D_FF, D_MODEL)
    b_down_d = b_down.reshape(n_le, 1, D_MODEL)
    ln2_g_r, ln2_b_r = row(ln2_g), row(ln2_b)

    assert bsz % TOKEN_GROUPS == 0
    per_group = t_tok // TOKEN_GROUPS
    x2 = x.reshape(t_tok, d)
    state = [None] * TOKEN_GROUPS
    for layer in range(depth):
        for g in range(TOKEN_GROUPS):
            if layer == 0:
                x1, x1_rows, logits_t = _mixer((x2,), layer, mixer_consts, seq_len=seq_len,
                                               n_tok=per_group, row0=g * per_group)
            else:
                x1, x1_rows, logits_t = _mixer(state[g], layer, mixer_consts, seq_len=seq_len,
                                               n_tok=per_group,
                                               combine_params=(ln2_g_r, ln2_b_r, layer - 1))
            yk4, gates = _moe_rows(x1_rows, logits_t, layer, br_t, w_gu_f, b_gu_d, w_down_f,
                                   b_down_d)
            state[g] = (x1, yk4, gates)
    out = None
    for g, (x1, yk4, gates) in enumerate(state):
        out = _combine(x1, yk4, gates, depth - 1, ln2_g_r, ln2_b_r, out_rows=t_tok,
                       row0=g * per_group, out_so_far=out)
    return out.reshape(bsz, seq_len, d)
```

```python
import functools

import jax
import jax.numpy as jnp
from jax import lax
from jax.experimental import pallas as pl
from jax.experimental.pallas import tpu as pltpu
from jax.experimental.pallas import tpu_sc as plsc

D_MODEL = 1024
DEPTH = 4
CONV_A_WIDTH = 3
POOL_WINDOWS = (2, 4, 8, 16)
POOL_GROUP_DIM = D_MODEL // len(POOL_WINDOWS)
CONV_C_WIDTH = 31
N_EXPERTS = 32
TOP_K = 4
D_FF = D_MODEL
SWIGLU_LIMIT = 7.0
SWIGLU_ALPHA = 1.702
LN_EPS = 1e-5
DEEPNORM_ALPHA = (2.0 * DEPTH) ** 0.25

V7X_LANES = 128
V7X_SUBLANES = 8
V7X_VMEM_LIMIT_BYTES = 56 * 1024 * 1024
V7X_MXU_DIM = 256
V7X_SC_CORES = 2
V7X_SC_SUBCORES = 16
V7X_SC_WORKERS = V7X_SC_CORES * V7X_SC_SUBCORES

TM_MIX = 256
HALO_A = 8
HALO_P = 16
HALO_C = 32
CONV_LANE_CHUNK = 128
TAP_ROW_BLOCK = 64
CONV_C_ROW_BLOCK = 64
MIX_COL_ORDER = (4, 5, 1, 2, 0, 6, 3, 7, 8)
TR_ROUTE = 512
GATE_LANES = V7X_LANES
BM_EXPERT = 512
EXPERT_TAIL_DIVISORS = (4, 2)
TM_COMB = 256
DEINTERLEAVE_COLS = V7X_MXU_DIM
ROW_WORDS_TILES = D_MODEL // 2 // V7X_LANES
SC_ROWS = 32
SLOT_TABLE_LANES = 512
TOKEN_GROUPS = 2
NEG_BIG = -3.0e38

F32 = jnp.float32
BF16 = jnp.bfloat16


def _layer_norm(x, g, b):
    mu = jnp.mean(x, axis=-1, keepdims=True)
    xc = x - mu
    var = jnp.mean(xc * xc, axis=-1, keepdims=True)
    return xc * lax.rsqrt(var + LN_EPS) * g + b


def _store_token_rows(ref, val):
    rows = val.shape[0]
    words = pltpu.pack_elementwise([val[:, :D_MODEL // 2], val[:, D_MODEL // 2:]],
                                   packed_dtype=BF16)
    for g in range(ROW_WORDS_TILES):
        ref[pl.ds(g, rows, stride=ROW_WORDS_TILES), :] = words[:, g * V7X_LANES:(g + 1) * V7X_LANES]


def _load_token_rows(ref, rows):
    words = jnp.concatenate([ref[pl.ds(g, rows, stride=ROW_WORDS_TILES), :]
                             for g in range(ROW_WORDS_TILES)], axis=1)
    halves = [pltpu.unpack_elementwise(words, index=i, packed_dtype=BF16, unpacked_dtype=F32)
              for i in range(2)]
    return jnp.concatenate(halves, axis=1)


def _combine_rows(x1, yk_ref, gates, ln_g, ln_b):
    rows = x1.shape[0]
    m = jnp.zeros((rows, D_MODEL), F32)
    for k in range(TOP_K):
        m = m + gates[:, k:k + 1] * _load_token_rows(yk_ref.at[k], rows)
    return _layer_norm(DEEPNORM_ALPHA * x1 + m, ln_g, ln_b)


def _realign(ext_ref, shift_ref, halo, rows, lane0, lanes, max_shift):
    n = halo + rows - V7X_SUBLANES
    for b in range(1, min(max_shift, V7X_SUBLANES - 1) + 1):
        shift_ref[b - 1, 0:n, lane0:lane0 + lanes] = (
            ext_ref[V7X_SUBLANES - b:V7X_SUBLANES - b + n, lane0:lane0 + lanes])


def _causal_taps(ext_ref, shift_ref, halo, rows, lane0, lanes, weights):
    outs = []
    for r0 in range(0, rows, TAP_ROW_BLOCK):
        acc = None
        for j, w in enumerate(weights):
            a, b = divmod(j, V7X_SUBLANES)
            if b == 0:
                start = halo - V7X_SUBLANES * a + r0
                term = ext_ref[start:start + TAP_ROW_BLOCK, lane0:lane0 + lanes]
            else:
                start = halo - V7X_SUBLANES * (a + 1) + r0
                term = shift_ref[b - 1, start:start + TAP_ROW_BLOCK, lane0:lane0 + lanes]
            if w is not None:
                term = term * w
            acc = term if acc is None else acc + term
        outs.append(acc)
    return jnp.concatenate(outs, axis=0)


N_MIXER_CONSTS = 16


def _mixer_kernel(*refs, tiles_per_seq, fused_combine):
    n_in = 5 if fused_combine else 1
    inputs, refs = refs[:n_in], refs[n_in:]
    (w_in_ref, b_in_ref, conv_a_ref, w_out_a_ref, w_pool_ref, scale_pool_ref, conv_c_ref,
     conv_c_b_ref, ln_c_g_ref, ln_c_b_ref, w_out_c_ref, b_out_c_ref, w_o_ref, ln1_g_ref,
     ln1_b_ref, wr_ref) = refs[:N_MIXER_CONSTS]
    (y_ref, y_rows_ref, logits_ref,
     x_f32, xb_ref, ext_a, ext_p, ext_c, shift_ref, v_ref) = refs[N_MIXER_CONSTS:]
    tm = y_ref.shape[0]
    tile_in_seq = pl.program_id(0) % tiles_per_seq

    @pl.when(tile_in_seq == 0)
    def _():
        ext_a[0:HALO_A, :] = jnp.zeros((HALO_A, D_MODEL), F32)
        ext_p[0:HALO_P, :] = jnp.zeros((HALO_P, D_MODEL), F32)
        ext_c[0:HALO_C, :] = jnp.zeros((HALO_C, D_MODEL), F32)

    if fused_combine:
        x_prev_ref, yk_ref, gate_ref, ln2_g_ref, ln2_b_ref = inputs
        x_f32[...] = _combine_rows(x_prev_ref[...], yk_ref, gate_ref[...], ln2_g_ref[0],
                                   ln2_b_ref[0])
    else:
        x_f32[...] = inputs[0][...]
    xb_ref[...] = x_f32[...].astype(BF16)

    def proj(slot):
        lo = MIX_COL_ORDER[slot] * D_MODEL
        return (jnp.dot(xb_ref[...], w_in_ref[0, :, lo:lo + D_MODEL], preferred_element_type=F32)
                + b_in_ref[0, :, lo:lo + D_MODEL])

    ext_c[HALO_C:HALO_C + tm, :] = proj(0) * jax.nn.sigmoid(proj(1))
    ext_a[HALO_A:HALO_A + tm, :] = proj(2) * proj(3)
    _realign(ext_c, shift_ref, HALO_C, tm, 0, D_MODEL, CONV_C_WIDTH - 1)
    taps_of_copy = {}
    for j in range(CONV_C_WIDTH):
        a, b = divmod(j, V7X_SUBLANES)
        taps_of_copy.setdefault(b, []).append((a, j))

    def conv_block(i, carry):
        r0 = i * CONV_C_ROW_BLOCK
        for c0 in range(0, D_MODEL, CONV_LANE_CHUNK):
            lanes = slice(c0, c0 + CONV_LANE_CHUNK)
            acc = conv_c_b_ref[0, :, lanes]
            for b, taps in taps_of_copy.items():
                a_max = max(a for a, _ in taps)
                first = HALO_C - V7X_SUBLANES * (a_max + (1 if b else 0))
                start = pl.multiple_of(r0 + first, V7X_SUBLANES)
                span = CONV_C_ROW_BLOCK + V7X_SUBLANES * a_max
                if b == 0:
                    win = ext_c[pl.ds(start, span), lanes]
                else:
                    win = shift_ref[b - 1, pl.ds(start, span), lanes]
                for a, j in taps:
                    off = V7X_SUBLANES * (a_max - a)
                    w = conv_c_ref[0, CONV_C_WIDTH - 1 - j:CONV_C_WIDTH - j, lanes]
                    acc = acc + win[off:off + CONV_C_ROW_BLOCK] * w
            v_ref[pl.ds(pl.multiple_of(r0, CONV_C_ROW_BLOCK), CONV_C_ROW_BLOCK), lanes] = acc
        return carry

    lax.fori_loop(0, tm // CONV_C_ROW_BLOCK, conv_block, 0)
    ext_c[0:HALO_C, :] = ext_c[tm:tm + HALO_C, :]

    b_a = proj(4)
    gate_a = jax.nn.sigmoid(proj(5))
    parts = []
    for c0 in range(0, D_MODEL, CONV_LANE_CHUNK):
        w = [conv_a_ref[0, CONV_A_WIDTH - 1 - j:CONV_A_WIDTH - j, c0:c0 + CONV_LANE_CHUNK]
             for j in range(CONV_A_WIDTH)]
        _realign(ext_a, shift_ref, HALO_A, tm, c0, CONV_LANE_CHUNK, CONV_A_WIDTH - 1)
        parts.append(_causal_taps(ext_a, shift_ref, HALO_A, tm, c0, CONV_LANE_CHUNK, w))
    u_a = jnp.concatenate(parts, axis=1)
    ext_a[0:HALO_A, :] = ext_a[tm:tm + HALO_A, :]
    y_a = jnp.dot((b_a * u_a).astype(BF16), w_out_a_ref[0], preferred_element_type=F32)
    merged = gate_a * y_a

    v = _layer_norm(v_ref[...], ln_c_g_ref[0], ln_c_b_ref[0])
    v = v * jax.nn.sigmoid(v)
    y_c = jnp.dot(v.astype(BF16), w_out_c_ref[0], preferred_element_type=F32) + b_out_c_ref[0]
    merged = merged + jax.nn.sigmoid(proj(8)) * y_c

    p_in = proj(6)
    ext_p[HALO_P:HALO_P + tm, :] = p_in
    pos = tile_in_seq * tm + lax.broadcasted_iota(jnp.int32, (tm, POOL_GROUP_DIM), 0)
    parts = []
    for g, win in enumerate(POOL_WINDOWS):
        lo = g * POOL_GROUP_DIM
        sub = []
        for c0 in range(lo, lo + POOL_GROUP_DIM, CONV_LANE_CHUNK):
            _realign(ext_p, shift_ref, HALO_P, tm, c0, CONV_LANE_CHUNK, win - 1)
            sub.append(_causal_taps(ext_p, shift_ref, HALO_P, tm, c0, CONV_LANE_CHUNK,
                                    [None] * win))
        wsum = jnp.concatenate(sub, axis=1)
        cnt = jnp.minimum(pos + 1, win).astype(F32)
        pooled = wsum / cnt - p_in[:, lo:lo + POOL_GROUP_DIM]
        parts.append(jnp.dot(pooled.astype(BF16), w_pool_ref[0, g], preferred_element_type=F32))
    ext_p[0:HALO_P, :] = ext_p[tm:tm + HALO_P, :]
    y_b = jnp.concatenate(parts, axis=1) * scale_pool_ref[0]
    merged = merged + jax.nn.sigmoid(proj(7)) * y_b

    h = jnp.dot(merged.astype(BF16), w_o_ref[0], preferred_element_type=F32)
    y = _layer_norm(DEEPNORM_ALPHA * x_f32[...] + h, ln1_g_ref[0], ln1_b_ref[0])
    y_ref[...] = y
    _store_token_rows(y_rows_ref, y)
    logits_ref[...] = lax.dot_general(wr_ref[0], y.astype(BF16), (((1,), (1,)), ((), ())),
                                      preferred_element_type=F32)


def _layer_block(arr, layer, buffered_once=False):
    tail = (0,) * (arr.ndim - 1)
    mode = {"pipeline_mode": pl.Buffered(1)} if buffered_once else {}
    return pl.BlockSpec((1,) + arr.shape[1:], lambda *_: (layer,) + tail, **mode)


def _mixer(inputs, layer, consts, *, seq_len, n_tok, row0=0, combine_params=None):
    tm = TM_MIX
    assert seq_len % tm == 0 and n_tok % seq_len == 0 and row0 % tm == 0
    assert len(consts) == N_MIXER_CONSTS
    tile0 = row0 // tm
    tok_spec = pl.BlockSpec((tm, D_MODEL), lambda i: (i + tile0, 0))
    if combine_params is None:
        in_specs, args = [tok_spec], list(inputs)
    else:
        ln_g, ln_b, prev_layer = combine_params
        in_specs = [tok_spec,
                    pl.BlockSpec((TOP_K, tm * ROW_WORDS_TILES, V7X_LANES), lambda i: (0, i, 0)),
                    pl.BlockSpec((tm, GATE_LANES), lambda i: (i, 0)),
                    _layer_block(ln_g, prev_layer), _layer_block(ln_b, prev_layer)]
        args = list(inputs) + [ln_g, ln_b]
    return pl.pallas_call(
        functools.partial(_mixer_kernel, tiles_per_seq=seq_len // tm,
                          fused_combine=combine_params is not None),
        out_shape=(jax.ShapeDtypeStruct((n_tok, D_MODEL), F32),
                   jax.ShapeDtypeStruct((n_tok * ROW_WORDS_TILES, V7X_LANES), jnp.int32),
                   jax.ShapeDtypeStruct((N_EXPERTS, n_tok), F32)),
        grid=(n_tok // tm,),
        in_specs=in_specs + [_layer_block(c, layer, buffered_once=True) for c in consts],
        out_specs=(pl.BlockSpec((tm, D_MODEL), lambda i: (i, 0)),
                   pl.BlockSpec((tm * ROW_WORDS_TILES, V7X_LANES), lambda i: (i, 0)),
                   pl.BlockSpec((N_EXPERTS, tm), lambda i: (0, i))),
        scratch_shapes=[pltpu.VMEM((tm, D_MODEL), F32),
                        pltpu.VMEM((tm, D_MODEL), BF16),
                        pltpu.VMEM((HALO_A + tm, D_MODEL), F32),
                        pltpu.VMEM((HALO_P + tm, D_MODEL), F32),
                        pltpu.VMEM((HALO_C + tm, D_MODEL), F32),
                        pltpu.VMEM((V7X_SUBLANES - 1, HALO_C + tm - V7X_SUBLANES, D_MODEL), F32),
                        pltpu.VMEM((tm, D_MODEL), F32)],
        compiler_params=pltpu.CompilerParams(dimension_semantics=("arbitrary",),
                                             vmem_limit_bytes=V7X_VMEM_LIMIT_BYTES),
        name="mixer",
    )(*args, *consts)


def _router_kernel(logits_ref, br_ref, idx_ref, gate_ref, rank_ref, counts_ref, counts_row_ref,
                   run_ref, run_row_ref):
    tr = logits_ref.shape[1]

    @pl.when(pl.program_id(0) == 0)
    def _():
        run_ref[...] = jnp.zeros(run_ref.shape, F32)
        run_row_ref[...] = jnp.zeros(run_row_ref.shape, F32)

    expert = lax.broadcasted_iota(jnp.int32, (N_EXPERTS, tr), 0)
    work = logits_ref[...] + br_ref[0]
    vals, idxs = [], []
    for _ in range(TOP_K):
        m = jnp.max(work, axis=0, keepdims=True)
        idx = jnp.min(jnp.where(work == m, expert, N_EXPERTS), axis=0, keepdims=True)
        vals.append(m)
        idxs.append(idx)
        work = jnp.where(expert == idx, NEG_BIG, work)
    exps = [jnp.exp(v - vals[0]) for v in vals]
    denom = exps[0] + exps[1] + exps[2] + exps[3]

    sel = jnp.zeros((N_EXPERTS, tr), F32)
    for k in range(TOP_K):
        sel = sel + jnp.where(expert == idxs[k], 1.0, 0.0)
    earlier = lax.broadcasted_iota(jnp.int32, (tr, tr), 0)
    later = lax.broadcasted_iota(jnp.int32, (tr, tr), 1)
    before = jnp.where(earlier < later, 1.0, 0.0).astype(BF16)
    prior = jnp.dot(sel.astype(BF16), before, preferred_element_type=F32) + run_ref[:, 0:1]

    row = lax.broadcasted_iota(jnp.int32, (V7X_SUBLANES, tr), 0)
    idx_out = jnp.zeros((V7X_SUBLANES, tr), jnp.int32)
    gate_out = jnp.zeros((V7X_SUBLANES, tr), F32)
    rank_out = jnp.zeros((V7X_SUBLANES, tr), F32)
    for k in range(TOP_K):
        rank_k = jnp.sum(jnp.where(expert == idxs[k], prior, 0.0), axis=0, keepdims=True)
        idx_out = jnp.where(row == k, idxs[k], idx_out)
        gate_out = jnp.where(row == k, exps[k] / denom, gate_out)
        rank_out = jnp.where(row == k, rank_k, rank_out)
    idx_ref[...] = idx_out
    rank_ref[...] = rank_out.astype(jnp.int32)
    gate_rows = jnp.concatenate(
        [gate_out, jnp.zeros((GATE_LANES - V7X_SUBLANES, tr), F32)], axis=0)
    gate_ref[...] = gate_rows.T

    run_ref[...] = run_ref[...] + jnp.sum(sel, axis=1, keepdims=True)
    counts_ref[...] = run_ref[...]
    per_tile = lax.dot_general(jnp.ones((V7X_SUBLANES, tr), BF16), sel.astype(BF16),
                               (((1,), (1,)), ((), ())), preferred_element_type=F32)
    run_row_ref[:, 0:N_EXPERTS] = run_row_ref[:, 0:N_EXPERTS] + per_tile
    counts_row_ref[...] = run_row_ref[...]


def _router(logits_t, layer, br_t):
    t_tok = logits_t.shape[1]
    tr = TR_ROUTE
    tok_spec = pl.BlockSpec((V7X_SUBLANES, tr), lambda i: (0, i))
    return pl.pallas_call(
        _router_kernel,
        out_shape=(jax.ShapeDtypeStruct((V7X_SUBLANES, t_tok), jnp.int32),
                   jax.ShapeDtypeStruct((t_tok, GATE_LANES), F32),
                   jax.ShapeDtypeStruct((V7X_SUBLANES, t_tok), jnp.int32),
                   jax.ShapeDtypeStruct((N_EXPERTS, V7X_LANES), F32),
                   jax.ShapeDtypeStruct((V7X_SUBLANES, V7X_LANES), F32)),
        grid=(t_tok // tr,),
        in_specs=[pl.BlockSpec((N_EXPERTS, tr), lambda i: (0, i)), _layer_block(br_t, layer)],
        out_specs=(tok_spec, pl.BlockSpec((tr, GATE_LANES), lambda i: (i, 0)), tok_spec,
                   pl.BlockSpec((N_EXPERTS, V7X_LANES), lambda i: (0, 0)),
                   pl.BlockSpec((V7X_SUBLANES, V7X_LANES), lambda i: (0, 0))),
        scratch_shapes=[pltpu.VMEM((N_EXPERTS, V7X_LANES), F32),
                        pltpu.VMEM((V7X_SUBLANES, V7X_LANES), F32)],
        compiler_params=pltpu.CompilerParams(dimension_semantics=("arbitrary",)),
        name="router",
    )(logits_t, br_t)


def _slots_kernel(idx_ref, rank_ref, counts_ref, counts_row_ref, dest_ref, table_ref, *,
                  n_blocks):
    bm = BM_EXPERT
    shift = bm.bit_length() - 1
    assert bm == 1 << shift
    width = table_ref.shape[1]
    cnt_col = counts_ref[:, 0:1].astype(jnp.int32)
    cnt_row = counts_row_ref[0:1, :].astype(jnp.int32)
    nblk_col = lax.shift_right_logical(cnt_col + (bm - 1), shift)
    nblk_row = lax.shift_right_logical(cnt_row + (bm - 1), shift)
    e_sub = lax.broadcasted_iota(jnp.int32, (N_EXPERTS, V7X_LANES), 0)
    e_lane = lax.broadcasted_iota(jnp.int32, (N_EXPERTS, V7X_LANES), 1)
    blk_start = jnp.sum(jnp.where(e_lane < e_sub, nblk_row.astype(F32), 0.0), axis=1,
                        keepdims=True).astype(jnp.int32)
    blk_end = blk_start + nblk_col
    row_start = blk_start * bm

    dest = rank_ref[...]
    idx = idx_ref[...]
    for e in range(N_EXPERTS):
        dest = dest + jnp.where(idx == e, row_start[e:e + 1, :], 0)
    dest_ref[...] = dest

    b_lane = lax.broadcasted_iota(jnp.int32, (N_EXPERTS, width), 1)
    e_of = lax.broadcasted_iota(jnp.int32, (N_EXPERTS, width), 0)
    block_e = jnp.minimum(jnp.sum(jnp.where(blk_end <= b_lane, 1, 0), axis=0, keepdims=True),
                          N_EXPERTS - 1)
    mine = e_of == block_e
    cnt_of = jnp.sum(jnp.where(mine, cnt_col, 0), axis=0, keepdims=True)
    start_of = jnp.sum(jnp.where(mine, blk_start, 0), axis=0, keepdims=True)
    nb_used = jnp.sum(nblk_col, axis=0, keepdims=True)
    b_row = b_lane[0:1, :]
    block_rows = jnp.clip(cnt_of - (b_row - start_of) * bm, 0, bm)
    block_rows = jnp.where((b_row < nb_used) & (b_row < n_blocks), block_rows, 0)
    later = jnp.where((e_sub > e_lane) & (cnt_col > 0), e_sub, N_EXPERTS)
    next_e = jnp.min(later, axis=0, keepdims=True)
    next_e = jnp.where(next_e == N_EXPERTS, -1, next_e)

    row = lax.broadcasted_iota(jnp.int32, (V7X_SUBLANES, width), 0)
    lane = lax.broadcasted_iota(jnp.int32, (V7X_SUBLANES, width), 1)
    next_wide = jnp.concatenate(
        [next_e, jnp.full((1, width - V7X_LANES), -1, jnp.int32)], axis=1)
    table = jnp.where(row == 0, block_e, 0)
    table = jnp.where(row == 1, block_rows, table)
    table = jnp.where(row == 2, next_wide, table)
    table = jnp.where((row == 3) & (lane == 0), nb_used, table)
    table_ref[...] = table


def _slots(idx_t, rank_t, counts_col, counts_row, n_blocks):
    assert n_blocks <= SLOT_TABLE_LANES
    return pl.pallas_call(
        functools.partial(_slots_kernel, n_blocks=n_blocks),
        out_shape=(jax.ShapeDtypeStruct(idx_t.shape, jnp.int32),
                   jax.ShapeDtypeStruct((V7X_SUBLANES, SLOT_TABLE_LANES), jnp.int32)),
        name="slots",
    )(idx_t, rank_t, counts_col, counts_row)


def _sc_mesh():
    return plsc.VectorSubcoreMesh(core_axis_name="c", subcore_axis_name="s")


def _sc_worker_id():
    return lax.axis_index("s") * V7X_SC_CORES + lax.axis_index("c")


def _dispatch_rows(x3, dest_km, n_slots):
    t_tok = x3.shape[0]
    tok_per_worker = t_tok // V7X_SC_WORKERS
    assert tok_per_worker % SC_ROWS == 0

    def body(x_hbm, dest_hbm, out_hbm, rows_v, idx_v, sem):
        base = _sc_worker_id() * tok_per_worker

        @pl.loop(0, tok_per_worker // SC_ROWS)
        def _(step):
            t0 = pl.multiple_of(base + step * SC_ROWS, SC_ROWS)
            pltpu.sync_copy(x_hbm.at[pl.ds(t0, SC_ROWS)], rows_v)
            for k in range(TOP_K):
                pltpu.sync_copy(dest_hbm.at[k, pl.ds(t0, SC_ROWS)], idx_v.at[k])
            for k in range(TOP_K):
                pltpu.async_copy(rows_v, out_hbm.at[idx_v.at[k]], sem).wait()

    return pl.kernel(
        body, mesh=_sc_mesh(),
        out_type=jax.ShapeDtypeStruct((n_slots,) + x3.shape[1:], x3.dtype),
        scratch_types=[pltpu.VMEM((SC_ROWS,) + x3.shape[1:], x3.dtype),
                       pltpu.VMEM((TOP_K, SC_ROWS), jnp.int32),
                       pltpu.SemaphoreType.DMA],
    )(x3, dest_km)


def _gather_rows(ys3, dest_km):
    t_tok = dest_km.shape[1]
    workers_per_k = V7X_SC_WORKERS // TOP_K
    tok_per_worker = t_tok // workers_per_k
    assert V7X_SC_WORKERS % TOP_K == 0 and tok_per_worker % SC_ROWS == 0

    def body(ys_hbm, src_hbm, out_hbm, rows_v, idx_v, sem):
        wid = _sc_worker_id()
        k = wid // workers_per_k
        base = (wid % workers_per_k) * tok_per_worker

        @pl.loop(0, tok_per_worker // SC_ROWS)
        def _(step):
            t0 = pl.multiple_of(base + step * SC_ROWS, SC_ROWS)
            pltpu.sync_copy(src_hbm.at[k, pl.ds(t0, SC_ROWS)], idx_v)
            pltpu.async_copy(ys_hbm.at[idx_v], rows_v, sem).wait()
            pltpu.sync_copy(rows_v, out_hbm.at[pl.ds(pl.multiple_of(k * t_tok + t0, SC_ROWS),
                                                     SC_ROWS)])

    return pl.kernel(
        body, mesh=_sc_mesh(),
        out_type=jax.ShapeDtypeStruct((TOP_K * t_tok,) + ys3.shape[1:], ys3.dtype),
        scratch_types=[pltpu.VMEM((SC_ROWS,) + ys3.shape[1:], ys3.dtype),
                       pltpu.VMEM((SC_ROWS,), jnp.int32),
                       pltpu.SemaphoreType.DMA],
    )(ys3, dest_km)


def _expert_kernel(table_ref, xs_ref, w_gu_hbm, b_gu_ref, w_down_hbm, b_down_ref, ys_ref,
                   w_gu_f, w_down_f, w_gu_b, w_down_b, sem_gu, sem_down, slot_ref, *, layer):
    bm = BM_EXPERT
    b = pl.program_id(0)
    e = table_ref[0, b]
    n_rows = table_ref[1, b]
    live = n_rows > 0
    new_expert = jnp.logical_or(b == 0, e != table_ref[0, jnp.maximum(b - 1, 0)])

    def weight_copies(expert, slot):
        idx = layer * N_EXPERTS + expert
        return (pltpu.make_async_copy(w_gu_hbm.at[idx], w_gu_f.at[slot], sem_gu.at[slot]),
                pltpu.make_async_copy(w_down_hbm.at[idx], w_down_f.at[slot], sem_down.at[slot]))

    @pl.when(b == 0)
    def _():
        slot_ref[0] = 0
        for copy in weight_copies(e, 0):
            copy.start()

    @pl.when(jnp.logical_and(new_expert, live))
    def _():
        slot = slot_ref[0]
        nxt = table_ref[2, e]

        @pl.when(nxt >= 0)
        def _():
            for copy in weight_copies(nxt, 1 - slot):
                copy.start()

        for copy in weight_copies(e, slot):
            copy.wait()
        slot_ref[0] = 1 - slot

        cb = DEINTERLEAVE_COLS
        src = lax.broadcasted_iota(jnp.int32, (cb, cb), 0)
        dst = lax.broadcasted_iota(jnp.int32, (cb, cb), 1)
        pick = jnp.where(dst < cb // 2, 2 * dst, 2 * (dst - cb // 2) + 1)
        sel = jnp.where(src == pick, 1.0, 0.0).astype(BF16)
        for c in range(2 * D_FF // cb):
            blk = jnp.dot(w_gu_f[slot, :, c * cb:(c + 1) * cb].astype(BF16), sel,
                          preferred_element_type=F32).astype(BF16)
            lo = c * (cb // 2)
            w_gu_b[:, lo:lo + cb // 2] = blk[:, :cb // 2]
            w_gu_b[:, D_FF + lo:D_FF + lo + cb // 2] = blk[:, cb // 2:]
        w_down_b[...] = w_down_f[slot].astype(BF16)

    def expert_rows(rows):
        xb = _load_token_rows(xs_ref, rows).astype(BF16)
        h = jnp.dot(xb, w_gu_b[...], preferred_element_type=F32) + b_gu_ref[e]
        gate = jnp.minimum(h[:, :D_FF], SWIGLU_LIMIT)
        up = jnp.clip(h[:, D_FF:], -SWIGLU_LIMIT, SWIGLU_LIMIT)
        act = (up + 1.0) * (gate * jax.nn.sigmoid(SWIGLU_ALPHA * gate))
        y = jnp.dot(act.astype(BF16), w_down_b[...], preferred_element_type=F32) + b_down_ref[e]
        _store_token_rows(ys_ref, y)

    sizes = [bm // d for d in EXPERT_TAIL_DIVISORS] + [bm]
    for lo, rows in zip([0] + sizes[:-1], sizes):
        @pl.when(jnp.logical_and(n_rows > lo, n_rows <= rows))
        def _(rows=rows):
            expert_rows(rows)
            if rows < bm:
                ys_ref[rows * ROW_WORDS_TILES:, :] = jnp.zeros(
                    ((bm - rows) * ROW_WORDS_TILES, V7X_LANES), ys_ref.dtype)

    @pl.when(jnp.logical_not(live))
    def _():
        ys_ref[...] = jnp.zeros(ys_ref.shape, ys_ref.dtype)


def _experts(table, xs_rows, layer, w_gu, b_gu, w_down, b_down):
    bm = BM_EXPERT
    n_blocks = xs_rows.shape[0] // (bm * ROW_WORDS_TILES)

    def x_map(b, tbl):
        return (jnp.minimum(b, tbl[3, 0] - 1), 0)

    def layer_biases(width):
        return pl.BlockSpec((N_EXPERTS, 1, width), lambda b, tbl: (layer, 0, 0))

    row_block = (bm * ROW_WORDS_TILES, V7X_LANES)
    grid_spec = pltpu.PrefetchScalarGridSpec(
        num_scalar_prefetch=1,
        grid=(n_blocks,),
        in_specs=[pl.BlockSpec(row_block, x_map),
                  pl.BlockSpec(memory_space=pl.ANY),
                  layer_biases(2 * D_FF),
                  pl.BlockSpec(memory_space=pl.ANY),
                  layer_biases(D_MODEL)],
        out_specs=pl.BlockSpec(row_block, lambda b, tbl: (b, 0)),
        scratch_shapes=[pltpu.VMEM((2, D_MODEL, 2 * D_FF), F32),
                        pltpu.VMEM((2, D_FF, D_MODEL), F32),
                        pltpu.VMEM((D_MODEL, 2 * D_FF), BF16),
                        pltpu.VMEM((D_FF, D_MODEL), BF16),
                        pltpu.SemaphoreType.DMA((2,)),
                        pltpu.SemaphoreType.DMA((2,)),
                        pltpu.SMEM((1,), jnp.int32)],
    )
    return pl.pallas_call(
        functools.partial(_expert_kernel, layer=layer),
        out_shape=jax.ShapeDtypeStruct(xs_rows.shape, xs_rows.dtype),
        grid_spec=grid_spec,
        compiler_params=pltpu.CompilerParams(dimension_semantics=("arbitrary",),
                                             vmem_limit_bytes=V7X_VMEM_LIMIT_BYTES),
        name="experts",
    )(table, xs_rows, w_gu, b_gu, w_down, b_down)


def _combine_kernel(x_ref, yk_ref, gate_ref, g_ref, b_ref, *out_refs):
    o_ref = out_refs[-1]
    o_ref[...] = _combine_rows(x_ref[...], yk_ref, gate_ref[...], g_ref[0], b_ref[0])


def _combine(x1, yk4, gates_pad, layer, ln_g, ln_b, *, out_rows, row0, out_so_far=None):
    t_tok = x1.shape[0]
    tm = TM_COMB
    assert row0 % tm == 0
    tile0 = row0 // tm
    in_specs = [pl.BlockSpec((tm, D_MODEL), lambda i: (i, 0)),
                pl.BlockSpec((TOP_K, tm * ROW_WORDS_TILES, V7X_LANES), lambda i: (0, i, 0)),
                pl.BlockSpec((tm, GATE_LANES), lambda i: (i, 0)),
                _layer_block(ln_g, layer), _layer_block(ln_b, layer)]
    args = [x1, yk4, gates_pad, ln_g, ln_b]
    aliases = {}
    if out_so_far is not None:
        in_specs.append(pl.BlockSpec(memory_space=pl.ANY))
        args.append(out_so_far)
        aliases = {len(args) - 1: 0}
    return pl.pallas_call(
        _combine_kernel,
        out_shape=jax.ShapeDtypeStruct((out_rows, D_MODEL), F32),
        grid=(t_tok // tm,),
        in_specs=in_specs,
        out_specs=pl.BlockSpec((tm, D_MODEL), lambda i: (i + tile0, 0)),
        input_output_aliases=aliases,
        compiler_params=pltpu.CompilerParams(dimension_semantics=("arbitrary",)),
        name="combine",
    )(*args)


def _moe_rows(x1_rows, logits_t, layer, br_t, w_gu, b_gu, w_down, b_down):
    t_tok = logits_t.shape[1]
    n_assign = t_tok * TOP_K
    bm = BM_EXPERT
    n_blocks = n_assign // bm + N_EXPERTS
    n_slots = n_blocks * bm

    idx_t, gates_pad, rank_t, counts_col, counts_row = _router(logits_t, layer, br_t)
    dest_km, table = _slots(idx_t, rank_t, counts_col, counts_row, n_blocks)

    tile = (ROW_WORDS_TILES, V7X_LANES)
    xs3 = _dispatch_rows(x1_rows.reshape(t_tok, *tile), dest_km, n_slots)
    ys_rows = _experts(table, xs3.reshape(n_slots * ROW_WORDS_TILES, V7X_LANES),
                       layer, w_gu, b_gu, w_down, b_down)
    yk3 = _gather_rows(ys_rows.reshape(n_slots, *tile), dest_km)
    return yk3.reshape(TOP_K, t_tok * ROW_WORDS_TILES, V7X_LANES), gates_pad


def kernel(x, w_in, b_in, conv_a, w_out_a, w_pool, scale_pool, conv_c, conv_c_b, ln_c_g, ln_c_b,
           w_out_c, b_out_c, w_o, ln1_g, ln1_b, w_router, b_router, w_gu, b_gu, w_down, b_down,
           ln2_g, ln2_b):
    bsz, seq_len, d = x.shape
    assert d == D_MODEL
    t_tok = bsz * seq_len
    depth = w_in.shape[0]

    def row(v):
        return v[:, None, :]

    wr_t = jnp.transpose(w_router, (0, 2, 1)).astype(BF16)
    mixer_consts = (w_in.astype(BF16), row(b_in), conv_a, w_out_a.astype(BF16),
                    w_pool.astype(BF16), row(scale_pool), conv_c, row(conv_c_b), row(ln_c_g),
                    row(ln_c_b), w_out_c.astype(BF16), row(b_out_c), w_o.astype(BF16),
                    row(ln1_g), row(ln1_b), wr_t)
    br_t = jnp.broadcast_to(b_router[:, :, None], (depth, N_EXPERTS, TR_ROUTE))
    n_le = depth * N_EXPERTS
    w_gu_f = w_gu.reshape(n_le, D_MODEL, 2 * D_FF)
    b_gu_d = jnp.concatenate([b_gu[..., 0::2], b_gu[..., 1::2]], axis=-1).reshape(n_le, 1, 2 * D_FF)
    w_down_f = w_down.reshape(n_le, D_FF, D_MODEL)
    b_down_d = b_down.reshape(n_le, 1, D_MODEL)
    ln2_g_r, ln2_b_r = row(ln2_g), row(ln2_b)

    assert bsz % TOKEN_GROUPS == 0
    per_group = t_tok // TOKEN_GROUPS
    x2 = x.reshape(t_tok, d)
    state = [None] * TOKEN_GROUPS
    for layer in range(depth):
        for g in range(TOKEN_GROUPS):
            if layer == 0:
                x1, x1_rows, logits_t = _mixer((x2,), layer, mixer_consts, seq_len=seq_len,
                                               n_tok=per_group, row0=g * per_group)
            else:
                x1, x1_rows, logits_t = _mixer(state[g], layer, mixer_consts, seq_len=seq_len,
                                               n_tok=per_group,
                                               combine_params=(ln2_g_r, ln2_b_r, layer - 1))
            yk4, gates = _moe_rows(x1_rows, logits_t, layer, br_t, w_gu_f, b_gu_d, w_down_f,
                                   b_down_d)
            state[g] = (x1, yk4, gates)
    out = None
    for g, (x1, yk4, gates) in enumerate(state):
        out = _combine(x1, yk4, gates, depth - 1, ln2_g_r, ln2_b_r, out_rows=t_tok,
                       row0=g * per_group, out_so_far=out)
    return out.reshape(bsz, seq_len, d)
```

```python
import functools

import jax
import jax.numpy as jnp
from jax import lax
from jax.experimental import pallas as pl
from jax.experimental.pallas import tpu as pltpu
from jax.experimental.pallas import tpu_sc as plsc

D_MODEL = 1024
DEPTH = 4
CONV_A_WIDTH = 3
POOL_WINDOWS = (2, 4, 8, 16)
POOL_GROUP_DIM = D_MODEL // len(POOL_WINDOWS)
CONV_C_WIDTH = 31
N_EXPERTS = 32
TOP_K = 4
D_FF = D_MODEL
SWIGLU_LIMIT = 7.0
SWIGLU_ALPHA = 1.702
LN_EPS = 1e-5
DEEPNORM_ALPHA = (2.0 * DEPTH) ** 0.25

V7X_LANES = 128
V7X_SUBLANES = 8
V7X_VMEM_LIMIT_BYTES = 56 * 1024 * 1024
V7X_MXU_DIM = 256
V7X_SC_CORES = 2
V7X_SC_SUBCORES = 16
V7X_SC_WORKERS = V7X_SC_CORES * V7X_SC_SUBCORES

TM_MIX = 256
HALO_A = 8
HALO_P = 16
HALO_C = 32
CONV_LANE_CHUNK = 128
TAP_ROW_BLOCK = 64
CONV_C_ROW_BLOCK = 64
MIX_COL_ORDER = (4, 5, 1, 2, 0, 6, 3, 7, 8)
TR_ROUTE = 512
GATE_LANES = V7X_LANES
BM_EXPERT = 1024
EXPERT_TAIL_DIVISORS = (8, 4, 2)
TM_COMB = 256
DEINTERLEAVE_COLS = V7X_MXU_DIM
ROW_WORDS_TILES = D_MODEL // 2 // V7X_LANES
SC_ROWS = 32
SLOT_TABLE_LANES = 512
TOKEN_GROUPS = 2
NEG_BIG = -3.0e38

F32 = jnp.float32
BF16 = jnp.bfloat16


def _layer_norm(x, g, b):
    mu = jnp.mean(x, axis=-1, keepdims=True)
    xc = x - mu
    var = jnp.mean(xc * xc, axis=-1, keepdims=True)
    return xc * lax.rsqrt(var + LN_EPS) * g + b


def _store_token_rows(ref, val):
    rows = val.shape[0]
    words = pltpu.pack_elementwise([val[:, :D_MODEL // 2], val[:, D_MODEL // 2:]],
                                   packed_dtype=BF16)
    for g in range(ROW_WORDS_TILES):
        ref[pl.ds(g, rows, stride=ROW_WORDS_TILES), :] = words[:, g * V7X_LANES:(g + 1) * V7X_LANES]


def _load_token_rows(ref, rows):
    words = jnp.concatenate([ref[pl.ds(g, rows, stride=ROW_WORDS_TILES), :]
                             for g in range(ROW_WORDS_TILES)], axis=1)
    halves = [pltpu.unpack_elementwise(words, index=i, packed_dtype=BF16, unpacked_dtype=F32)
              for i in range(2)]
    return jnp.concatenate(halves, axis=1)


def _combine_rows(x1, yk_ref, gates, ln_g, ln_b):
    rows = x1.shape[0]
    m = jnp.zeros((rows, D_MODEL), F32)
    for k in range(TOP_K):
        m = m + gates[:, k:k + 1] * _load_token_rows(yk_ref.at[k], rows)
    return _layer_norm(DEEPNORM_ALPHA * x1 + m, ln_g, ln_b)


def _realign(ext_ref, shift_ref, halo, rows, lane0, lanes, max_shift):
    n = halo + rows - V7X_SUBLANES
    for b in range(1, min(max_shift, V7X_SUBLANES - 1) + 1):
        shift_ref[b - 1, 0:n, lane0:lane0 + lanes] = (
            ext_ref[V7X_SUBLANES - b:V7X_SUBLANES - b + n, lane0:lane0 + lanes])


def _causal_taps(ext_ref, shift_ref, halo, rows, lane0, lanes, weights):
    outs = []
    for r0 in range(0, rows, TAP_ROW_BLOCK):
        acc = None
        for j, w in enumerate(weights):
            a, b = divmod(j, V7X_SUBLANES)
            if b == 0:
                start = halo - V7X_SUBLANES * a + r0
                term = ext_ref[start:start + TAP_ROW_BLOCK, lane0:lane0 + lanes]
            else:
                start = halo - V7X_SUBLANES * (a + 1) + r0
                term = shift_ref[b - 1, start:start + TAP_ROW_BLOCK, lane0:lane0 + lanes]
            if w is not None:
                term = term * w
            acc = term if acc is None else acc + term
        outs.append(acc)
    return jnp.concatenate(outs, axis=0)


N_MIXER_CONSTS = 16


def _mixer_kernel(*refs, tiles_per_seq, fused_combine):
    n_in = 5 if fused_combine else 1
    inputs, refs = refs[:n_in], refs[n_in:]
    (w_in_ref, b_in_ref, conv_a_ref, w_out_a_ref, w_pool_ref, scale_pool_ref, conv_c_ref,
     conv_c_b_ref, ln_c_g_ref, ln_c_b_ref, w_out_c_ref, b_out_c_ref, w_o_ref, ln1_g_ref,
     ln1_b_ref, wr_ref) = refs[:N_MIXER_CONSTS]
    (y_ref, y_rows_ref, logits_ref,
     x_f32, xb_ref, ext_a, ext_p, ext_c, shift_ref, v_ref) = refs[N_MIXER_CONSTS:]
    tm = y_ref.shape[0]
    tile_in_seq = pl.program_id(0) % tiles_per_seq

    @pl.when(tile_in_seq == 0)
    def _():
        ext_a[0:HALO_A, :] = jnp.zeros((HALO_A, D_MODEL), F32)
        ext_p[0:HALO_P, :] = jnp.zeros((HALO_P, D_MODEL), F32)
        ext_c[0:HALO_C, :] = jnp.zeros((HALO_C, D_MODEL), F32)

    if fused_combine:
        x_prev_ref, yk_ref, gate_ref, ln2_g_ref, ln2_b_ref = inputs
        x_f32[...] = _combine_rows(x_prev_ref[...], yk_ref, gate_ref[...], ln2_g_ref[0],
                                   ln2_b_ref[0])
    else:
        x_f32[...] = inputs[0][...]
    xb_ref[...] = x_f32[...].astype(BF16)

    def proj(slot):
        lo = MIX_COL_ORDER[slot] * D_MODEL
        return (jnp.dot(xb_ref[...], w_in_ref[0, :, lo:lo + D_MODEL], preferred_element_type=F32)
                + b_in_ref[0, :, lo:lo + D_MODEL])

    ext_c[HALO_C:HALO_C + tm, :] = proj(0) * jax.nn.sigmoid(proj(1))
    ext_a[HALO_A:HALO_A + tm, :] = proj(2) * proj(3)
    _realign(ext_c, shift_ref, HALO_C, tm, 0, D_MODEL, CONV_C_WIDTH - 1)
    taps_of_copy = {}
    for j in range(CONV_C_WIDTH):
        a, b = divmod(j, V7X_SUBLANES)
        taps_of_copy.setdefault(b, []).append((a, j))

    def conv_block(i, carry):
        r0 = i * CONV_C_ROW_BLOCK
        for c0 in range(0, D_MODEL, CONV_LANE_CHUNK):
            lanes = slice(c0, c0 + CONV_LANE_CHUNK)
            acc = conv_c_b_ref[0, :, lanes]
            for b, taps in taps_of_copy.items():
                a_max = max(a for a, _ in taps)
                first = HALO_C - V7X_SUBLANES * (a_max + (1 if b else 0))
                start = pl.multiple_of(r0 + first, V7X_SUBLANES)
                span = CONV_C_ROW_BLOCK + V7X_SUBLANES * a_max
                if b == 0:
                    win = ext_c[pl.ds(start, span), lanes]
                else:
                    win = shift_ref[b - 1, pl.ds(start, span), lanes]
                for a, j in taps:
                    off = V7X_SUBLANES * (a_max - a)
                    w = conv_c_ref[0, CONV_C_WIDTH - 1 - j:CONV_C_WIDTH - j, lanes]
                    acc = acc + win[off:off + CONV_C_ROW_BLOCK] * w
            v_ref[pl.ds(pl.multiple_of(r0, CONV_C_ROW_BLOCK), CONV_C_ROW_BLOCK), lanes] = acc
        return carry

    lax.fori_loop(0, tm // CONV_C_ROW_BLOCK, conv_block, 0)
    ext_c[0:HALO_C, :] = ext_c[tm:tm + HALO_C, :]

    b_a = proj(4)
    gate_a = jax.nn.sigmoid(proj(5))
    parts = []
    for c0 in range(0, D_MODEL, CONV_LANE_CHUNK):
        w = [conv_a_ref[0, CONV_A_WIDTH - 1 - j:CONV_A_WIDTH - j, c0:c0 + CONV_LANE_CHUNK]
             for j in range(CONV_A_WIDTH)]
        _realign(ext_a, shift_ref, HALO_A, tm, c0, CONV_LANE_CHUNK, CONV_A_WIDTH - 1)
        parts.append(_causal_taps(ext_a, shift_ref, HALO_A, tm, c0, CONV_LANE_CHUNK, w))
    u_a = jnp.concatenate(parts, axis=1)
    ext_a[0:HALO_A, :] = ext_a[tm:tm + HALO_A, :]
    y_a = jnp.dot((b_a * u_a).astype(BF16), w_out_a_ref[0], preferred_element_type=F32)
    merged = gate_a * y_a

    v = _layer_norm(v_ref[...], ln_c_g_ref[0], ln_c_b_ref[0])
    v = v * jax.nn.sigmoid(v)
    y_c = jnp.dot(v.astype(BF16), w_out_c_ref[0], preferred_element_type=F32) + b_out_c_ref[0]
    merged = merged + jax.nn.sigmoid(proj(8)) * y_c

    p_in = proj(6)
    ext_p[HALO_P:HALO_P + tm, :] = p_in
    pos = tile_in_seq * tm + lax.broadcasted_iota(jnp.int32, (tm, POOL_GROUP_DIM), 0)
    parts = []
    for g, win in enumerate(POOL_WINDOWS):
        lo = g * POOL_GROUP_DIM
        sub = []
        for c0 in range(lo, lo + POOL_GROUP_DIM, CONV_LANE_CHUNK):
            _realign(ext_p, shift_ref, HALO_P, tm, c0, CONV_LANE_CHUNK, win - 1)
            sub.append(_causal_taps(ext_p, shift_ref, HALO_P, tm, c0, CONV_LANE_CHUNK,
                                    [None] * win))
        wsum = jnp.concatenate(sub, axis=1)
        cnt = jnp.minimum(pos + 1, win).astype(F32)
        pooled = wsum / cnt - p_in[:, lo:lo + POOL_GROUP_DIM]
        parts.append(jnp.dot(pooled.astype(BF16), w_pool_ref[0, g], preferred_element_type=F32))
    ext_p[0:HALO_P, :] = ext_p[tm:tm + HALO_P, :]
    y_b = jnp.concatenate(parts, axis=1) * scale_pool_ref[0]
    merged = merged + jax.nn.sigmoid(proj(7)) * y_b

    h = jnp.dot(merged.astype(BF16), w_o_ref[0], preferred_element_type=F32)
    y = _layer_norm(DEEPNORM_ALPHA * x_f32[...] + h, ln1_g_ref[0], ln1_b_ref[0])
    y_ref[...] = y
    _store_token_rows(y_rows_ref, y)
    logits_ref[...] = lax.dot_general(wr_ref[0], y.astype(BF16), (((1,), (1,)), ((), ())),
                                      preferred_element_type=F32)


def _layer_block(arr, layer, buffered_once=False):
    tail = (0,) * (arr.ndim - 1)
    mode = {"pipeline_mode": pl.Buffered(1)} if buffered_once else {}
    return pl.BlockSpec((1,) + arr.shape[1:], lambda *_: (layer,) + tail, **mode)


def _mixer(inputs, layer, consts, *, seq_len, n_tok, row0=0, combine_params=None):
    tm = TM_MIX
    assert seq_len % tm == 0 and n_tok % seq_len == 0 and row0 % tm == 0
    assert len(consts) == N_MIXER_CONSTS
    tile0 = row0 // tm
    tok_spec = pl.BlockSpec((tm, D_MODEL), lambda i: (i + tile0, 0))
    if combine_params is None:
        in_specs, args = [tok_spec], list(inputs)
    else:
        ln_g, ln_b, prev_layer = combine_params
        in_specs = [tok_spec,
                    pl.BlockSpec((TOP_K, tm * ROW_WORDS_TILES, V7X_LANES), lambda i: (0, i, 0)),
                    pl.BlockSpec((tm, GATE_LANES), lambda i: (i, 0)),
                    _layer_block(ln_g, prev_layer), _layer_block(ln_b, prev_layer)]
        args = list(inputs) + [ln_g, ln_b]
    return pl.pallas_call(
        functools.partial(_mixer_kernel, tiles_per_seq=seq_len // tm,
                          fused_combine=combine_params is not None),
        out_shape=(jax.ShapeDtypeStruct((n_tok, D_MODEL), F32),
                   jax.ShapeDtypeStruct((n_tok * ROW_WORDS_TILES, V7X_LANES), jnp.int32),
                   jax.ShapeDtypeStruct((N_EXPERTS, n_tok), F32)),
        grid=(n_tok // tm,),
        in_specs=in_specs + [_layer_block(c, layer, buffered_once=True) for c in consts],
        out_specs=(pl.BlockSpec((tm, D_MODEL), lambda i: (i, 0)),
                   pl.BlockSpec((tm * ROW_WORDS_TILES, V7X_LANES), lambda i: (i, 0)),
                   pl.BlockSpec((N_EXPERTS, tm), lambda i: (0, i))),
        scratch_shapes=[pltpu.VMEM((tm, D_MODEL), F32),
                        pltpu.VMEM((tm, D_MODEL), BF16),
                        pltpu.VMEM((HALO_A + tm, D_MODEL), F32),
                        pltpu.VMEM((HALO_P + tm, D_MODEL), F32),
                        pltpu.VMEM((HALO_C + tm, D_MODEL), F32),
                        pltpu.VMEM((V7X_SUBLANES - 1, HALO_C + tm - V7X_SUBLANES, D_MODEL), F32),
                        pltpu.VMEM((tm, D_MODEL), F32)],
        compiler_params=pltpu.CompilerParams(dimension_semantics=("arbitrary",),
                                             vmem_limit_bytes=V7X_VMEM_LIMIT_BYTES),
        name="mixer",
    )(*args, *consts)


def _router_kernel(logits_ref, br_ref, idx_ref, gate_ref, rank_ref, counts_ref, counts_row_ref,
                   run_ref, run_row_ref):
    tr = logits_ref.shape[1]

    @pl.when(pl.program_id(0) == 0)
    def _():
        run_ref[...] = jnp.zeros(run_ref.shape, F32)
        run_row_ref[...] = jnp.zeros(run_row_ref.shape, F32)

    expert = lax.broadcasted_iota(jnp.int32, (N_EXPERTS, tr), 0)
    work = logits_ref[...] + br_ref[0]
    vals, idxs = [], []
    for _ in range(TOP_K):
        m = jnp.max(work, axis=0, keepdims=True)
        idx = jnp.min(jnp.where(work == m, expert, N_EXPERTS), axis=0, keepdims=True)
        vals.append(m)
        idxs.append(idx)
        work = jnp.where(expert == idx, NEG_BIG, work)
    exps = [jnp.exp(v - vals[0]) for v in vals]
    denom = exps[0] + exps[1] + exps[2] + exps[3]

    sel = jnp.zeros((N_EXPERTS, tr), F32)
    for k in range(TOP_K):
        sel = sel + jnp.where(expert == idxs[k], 1.0, 0.0)
    earlier = lax.broadcasted_iota(jnp.int32, (tr, tr), 0)
    later = lax.broadcasted_iota(jnp.int32, (tr, tr), 1)
    before = jnp.where(earlier < later, 1.0, 0.0).astype(BF16)
    prior = jnp.dot(sel.astype(BF16), before, preferred_element_type=F32) + run_ref[:, 0:1]

    row = lax.broadcasted_iota(jnp.int32, (V7X_SUBLANES, tr), 0)
    idx_out = jnp.zeros((V7X_SUBLANES, tr), jnp.int32)
    gate_out = jnp.zeros((V7X_SUBLANES, tr), F32)
    rank_out = jnp.zeros((V7X_SUBLANES, tr), F32)
    for k in range(TOP_K):
        rank_k = jnp.sum(jnp.where(expert == idxs[k], prior, 0.0), axis=0, keepdims=True)
        idx_out = jnp.where(row == k, idxs[k], idx_out)
        gate_out = jnp.where(row == k, exps[k] / denom, gate_out)
        rank_out = jnp.where(row == k, rank_k, rank_out)
    idx_ref[...] = idx_out
    rank_ref[...] = rank_out.astype(jnp.int32)
    gate_rows = jnp.concatenate(
        [gate_out, jnp.zeros((GATE_LANES - V7X_SUBLANES, tr), F32)], axis=0)
    gate_ref[...] = gate_rows.T

    run_ref[...] = run_ref[...] + jnp.sum(sel, axis=1, keepdims=True)
    counts_ref[...] = run_ref[...]
    per_tile = lax.dot_general(jnp.ones((V7X_SUBLANES, tr), BF16), sel.astype(BF16),
                               (((1,), (1,)), ((), ())), preferred_element_type=F32)
    run_row_ref[:, 0:N_EXPERTS] = run_row_ref[:, 0:N_EXPERTS] + per_tile
    counts_row_ref[...] = run_row_ref[...]


def _router(logits_t, layer, br_t):
    t_tok = logits_t.shape[1]
    tr = TR_ROUTE
    tok_spec = pl.BlockSpec((V7X_SUBLANES, tr), lambda i: (0, i))
    return pl.pallas_call(
        _router_kernel,
        out_shape=(jax.ShapeDtypeStruct((V7X_SUBLANES, t_tok), jnp.int32),
                   jax.ShapeDtypeStruct((t_tok, GATE_LANES), F32),
                   jax.ShapeDtypeStruct((V7X_SUBLANES, t_tok), jnp.int32),
                   jax.ShapeDtypeStruct((N_EXPERTS, V7X_LANES), F32),
                   jax.ShapeDtypeStruct((V7X_SUBLANES, V7X_LANES), F32)),
        grid=(t_tok // tr,),
        in_specs=[pl.BlockSpec((N_EXPERTS, tr), lambda i: (0, i)), _layer_block(br_t, layer)],
        out_specs=(tok_spec, pl.BlockSpec((tr, GATE_LANES), lambda i: (i, 0)), tok_spec,
                   pl.BlockSpec((N_EXPERTS, V7X_LANES), lambda i: (0, 0)),
                   pl.BlockSpec((V7X_SUBLANES, V7X_LANES), lambda i: (0, 0))),
        scratch_shapes=[pltpu.VMEM((N_EXPERTS, V7X_LANES), F32),
                        pltpu.VMEM((V7X_SUBLANES, V7X_LANES), F32)],
        compiler_params=pltpu.CompilerParams(dimension_semantics=("arbitrary",)),
        name="router",
    )(logits_t, br_t)


def _slots_kernel(idx_ref, rank_ref, counts_ref, counts_row_ref, dest_ref, table_ref, *,
                  n_blocks):
    bm = BM_EXPERT
    shift = bm.bit_length() - 1
    assert bm == 1 << shift
    width = table_ref.shape[1]
    cnt_col = counts_ref[:, 0:1].astype(jnp.int32)
    cnt_row = counts_row_ref[0:1, :].astype(jnp.int32)
    nblk_col = lax.shift_right_logical(cnt_col + (bm - 1), shift)
    nblk_row = lax.shift_right_logical(cnt_row + (bm - 1), shift)
    e_sub = lax.broadcasted_iota(jnp.int32, (N_EXPERTS, V7X_LANES), 0)
    e_lane = lax.broadcasted_iota(jnp.int32, (N_EXPERTS, V7X_LANES), 1)
    blk_start = jnp.sum(jnp.where(e_lane < e_sub, nblk_row.astype(F32), 0.0), axis=1,
                        keepdims=True).astype(jnp.int32)
    blk_end = blk_start + nblk_col
    row_start = blk_start * bm

    dest = rank_ref[...]
    idx = idx_ref[...]
    for e in range(N_EXPERTS):
        dest = dest + jnp.where(idx == e, row_start[e:e + 1, :], 0)
    dest_ref[...] = dest

    b_lane = lax.broadcasted_iota(jnp.int32, (N_EXPERTS, width), 1)
    e_of = lax.broadcasted_iota(jnp.int32, (N_EXPERTS, width), 0)
    block_e = jnp.minimum(jnp.sum(jnp.where(blk_end <= b_lane, 1, 0), axis=0, keepdims=True),
                          N_EXPERTS - 1)
    mine = e_of == block_e
    cnt_of = jnp.sum(jnp.where(mine, cnt_col, 0), axis=0, keepdims=True)
    start_of = jnp.sum(jnp.where(mine, blk_start, 0), axis=0, keepdims=True)
    nb_used = jnp.sum(nblk_col, axis=0, keepdims=True)
    b_row = b_lane[0:1, :]
    block_rows = jnp.clip(cnt_of - (b_row - start_of) * bm, 0, bm)
    block_rows = jnp.where((b_row < nb_used) & (b_row < n_blocks), block_rows, 0)
    later = jnp.where((e_sub > e_lane) & (cnt_col > 0), e_sub, N_EXPERTS)
    next_e = jnp.min(later, axis=0, keepdims=True)
    next_e = jnp.where(next_e == N_EXPERTS, -1, next_e)

    row = lax.broadcasted_iota(jnp.int32, (V7X_SUBLANES, width), 0)
    lane = lax.broadcasted_iota(jnp.int32, (V7X_SUBLANES, width), 1)
    next_wide = jnp.concatenate(
        [next_e, jnp.full((1, width - V7X_LANES), -1, jnp.int32)], axis=1)
    table = jnp.where(row == 0, block_e, 0)
    table = jnp.where(row == 1, block_rows, table)
    table = jnp.where(row == 2, next_wide, table)
    table = jnp.where((row == 3) & (lane == 0), nb_used, table)
    table_ref[...] = table


def _slots(idx_t, rank_t, counts_col, counts_row, n_blocks):
    assert n_blocks <= SLOT_TABLE_LANES
    return pl.pallas_call(
        functools.partial(_slots_kernel, n_blocks=n_blocks),
        out_shape=(jax.ShapeDtypeStruct(idx_t.shape, jnp.int32),
                   jax.ShapeDtypeStruct((V7X_SUBLANES, SLOT_TABLE_LANES), jnp.int32)),
        name="slots",
    )(idx_t, rank_t, counts_col, counts_row)


def _sc_mesh():
    return plsc.VectorSubcoreMesh(core_axis_name="c", subcore_axis_name="s")


def _sc_worker_id():
    return lax.axis_index("s") * V7X_SC_CORES + lax.axis_index("c")


def _dispatch_rows(x3, dest_km, n_slots):
    t_tok = x3.shape[0]
    tok_per_worker = t_tok // V7X_SC_WORKERS
    assert tok_per_worker % SC_ROWS == 0

    def body(x_hbm, dest_hbm, out_hbm, rows_v, idx_v, sem):
        base = _sc_worker_id() * tok_per_worker

        @pl.loop(0, tok_per_worker // SC_ROWS)
        def _(step):
            t0 = pl.multiple_of(base + step * SC_ROWS, SC_ROWS)
            pltpu.sync_copy(x_hbm.at[pl.ds(t0, SC_ROWS)], rows_v)
            for k in range(TOP_K):
                pltpu.sync_copy(dest_hbm.at[k, pl.ds(t0, SC_ROWS)], idx_v.at[k])
            for k in range(TOP_K):
                pltpu.async_copy(rows_v, out_hbm.at[idx_v.at[k]], sem).wait()

    return pl.kernel(
        body, mesh=_sc_mesh(),
        out_type=jax.ShapeDtypeStruct((n_slots,) + x3.shape[1:], x3.dtype),
        scratch_types=[pltpu.VMEM((SC_ROWS,) + x3.shape[1:], x3.dtype),
                       pltpu.VMEM((TOP_K, SC_ROWS), jnp.int32),
                       pltpu.SemaphoreType.DMA],
    )(x3, dest_km)


def _gather_rows(ys3, dest_km):
    t_tok = dest_km.shape[1]
    workers_per_k = V7X_SC_WORKERS // TOP_K
    tok_per_worker = t_tok // workers_per_k
    assert V7X_SC_WORKERS % TOP_K == 0 and tok_per_worker % SC_ROWS == 0

    def body(ys_hbm, src_hbm, out_hbm, rows_v, idx_v, sem):
        wid = _sc_worker_id()
        k = wid // workers_per_k
        base = (wid % workers_per_k) * tok_per_worker

        @pl.loop(0, tok_per_worker // SC_ROWS)
        def _(step):
            t0 = pl.multiple_of(base + step * SC_ROWS, SC_ROWS)
            pltpu.sync_copy(src_hbm.at[k, pl.ds(t0, SC_ROWS)], idx_v)
            pltpu.async_copy(ys_hbm.at[idx_v], rows_v, sem).wait()
            pltpu.sync_copy(rows_v, out_hbm.at[pl.ds(pl.multiple_of(k * t_tok + t0, SC_ROWS),
                                                     SC_ROWS)])

    return pl.kernel(
        body, mesh=_sc_mesh(),
        out_type=jax.ShapeDtypeStruct((TOP_K * t_tok,) + ys3.shape[1:], ys3.dtype),
        scratch_types=[pltpu.VMEM((SC_ROWS,) + ys3.shape[1:], ys3.dtype),
                       pltpu.VMEM((SC_ROWS,), jnp.int32),
                       pltpu.SemaphoreType.DMA],
    )(ys3, dest_km)


def _expert_kernel(table_ref, xs_ref, w_gu_hbm, b_gu_ref, w_down_hbm, b_down_ref, ys_ref,
                   w_gu_f, w_down_f, w_gu_b, w_down_b, sem_gu, sem_down, slot_ref, *, layer):
    bm = BM_EXPERT
    b = pl.program_id(0)
    e = table_ref[0, b]
    n_rows = table_ref[1, b]
    live = n_rows > 0
    new_expert = jnp.logical_or(b == 0, e != table_ref[0, jnp.maximum(b - 1, 0)])

    def weight_copies(expert, slot):
        idx = layer * N_EXPERTS + expert
        return (pltpu.make_async_copy(w_gu_hbm.at[idx], w_gu_f.at[slot], sem_gu.at[slot]),
                pltpu.make_async_copy(w_down_hbm.at[idx], w_down_f.at[slot], sem_down.at[slot]))

    @pl.when(b == 0)
    def _():
        slot_ref[0] = 0
        for copy in weight_copies(e, 0):
            copy.start()

    @pl.when(jnp.logical_and(new_expert, live))
    def _():
        slot = slot_ref[0]
        nxt = table_ref[2, e]

        @pl.when(nxt >= 0)
        def _():
            for copy in weight_copies(nxt, 1 - slot):
                copy.start()

        for copy in weight_copies(e, slot):
            copy.wait()
        slot_ref[0] = 1 - slot

        cb = DEINTERLEAVE_COLS
        src = lax.broadcasted_iota(jnp.int32, (cb, cb), 0)
        dst = lax.broadcasted_iota(jnp.int32, (cb, cb), 1)
        pick = jnp.where(dst < cb // 2, 2 * dst, 2 * (dst - cb // 2) + 1)
        sel = jnp.where(src == pick, 1.0, 0.0).astype(BF16)
        for c in range(2 * D_FF // cb):
            blk = jnp.dot(w_gu_f[slot, :, c * cb:(c + 1) * cb].astype(BF16), sel,
                          preferred_element_type=F32).astype(BF16)
            lo = c * (cb // 2)
            w_gu_b[:, lo:lo + cb // 2] = blk[:, :cb // 2]
            w_gu_b[:, D_FF + lo:D_FF + lo + cb // 2] = blk[:, cb // 2:]
        w_down_b[...] = w_down_f[slot].astype(BF16)

    def expert_rows(rows):
        xb = _load_token_rows(xs_ref, rows).astype(BF16)
        h = jnp.dot(xb, w_gu_b[...], preferred_element_type=F32) + b_gu_ref[e]
        gate = jnp.minimum(h[:, :D_FF], SWIGLU_LIMIT)
        up = jnp.clip(h[:, D_FF:], -SWIGLU_LIMIT, SWIGLU_LIMIT)
        act = (up + 1.0) * (gate * jax.nn.sigmoid(SWIGLU_ALPHA * gate))
        y = jnp.dot(act.astype(BF16), w_down_b[...], preferred_element_type=F32) + b_down_ref[e]
        _store_token_rows(ys_ref, y)

    sizes = [bm // d for d in EXPERT_TAIL_DIVISORS] + [bm]
    for lo, rows in zip([0] + sizes[:-1], sizes):
        @pl.when(jnp.logical_and(n_rows > lo, n_rows <= rows))
        def _(rows=rows):
            expert_rows(rows)


def _experts(table, xs_rows, layer, w_gu, b_gu, w_down, b_down):
    bm = BM_EXPERT
    n_blocks = xs_rows.shape[0] // (bm * ROW_WORDS_TILES)

    def row_map(b, tbl):
        return (jnp.minimum(b, tbl[3, 0] - 1), 0)

    def layer_biases(width):
        return pl.BlockSpec((N_EXPERTS, 1, width), lambda b, tbl: (layer, 0, 0))

    row_block = (bm * ROW_WORDS_TILES, V7X_LANES)
    grid_spec = pltpu.PrefetchScalarGridSpec(
        num_scalar_prefetch=1,
        grid=(n_blocks,),
        in_specs=[pl.BlockSpec(row_block, row_map),
                  pl.BlockSpec(memory_space=pl.ANY),
                  layer_biases(2 * D_FF),
                  pl.BlockSpec(memory_space=pl.ANY),
                  layer_biases(D_MODEL)],
        out_specs=pl.BlockSpec(row_block, row_map),
        scratch_shapes=[pltpu.VMEM((2, D_MODEL, 2 * D_FF), F32),
                        pltpu.VMEM((2, D_FF, D_MODEL), F32),
                        pltpu.VMEM((D_MODEL, 2 * D_FF), BF16),
                        pltpu.VMEM((D_FF, D_MODEL), BF16),
                        pltpu.SemaphoreType.DMA((2,)),
                        pltpu.SemaphoreType.DMA((2,)),
                        pltpu.SMEM((1,), jnp.int32)],
    )
    return pl.pallas_call(
        functools.partial(_expert_kernel, layer=layer),
        out_shape=jax.ShapeDtypeStruct(xs_rows.shape, xs_rows.dtype),
        grid_spec=grid_spec,
        compiler_params=pltpu.CompilerParams(dimension_semantics=("arbitrary",),
                                             vmem_limit_bytes=V7X_VMEM_LIMIT_BYTES),
        name="experts",
    )(table, xs_rows, w_gu, b_gu, w_down, b_down)


def _combine_kernel(x_ref, yk_ref, gate_ref, g_ref, b_ref, *out_refs):
    o_ref = out_refs[-1]
    o_ref[...] = _combine_rows(x_ref[...], yk_ref, gate_ref[...], g_ref[0], b_ref[0])


def _combine(x1, yk4, gates_pad, layer, ln_g, ln_b, *, out_rows, row0, out_so_far=None):
    t_tok = x1.shape[0]
    tm = TM_COMB
    assert row0 % tm == 0
    tile0 = row0 // tm
    in_specs = [pl.BlockSpec((tm, D_MODEL), lambda i: (i, 0)),
                pl.BlockSpec((TOP_K, tm * ROW_WORDS_TILES, V7X_LANES), lambda i: (0, i, 0)),
                pl.BlockSpec((tm, GATE_LANES), lambda i: (i, 0)),
                _layer_block(ln_g, layer), _layer_block(ln_b, layer)]
    args = [x1, yk4, gates_pad, ln_g, ln_b]
    aliases = {}
    if out_so_far is not None:
        in_specs.append(pl.BlockSpec(memory_space=pl.ANY))
        args.append(out_so_far)
        aliases = {len(args) - 1: 0}
    return pl.pallas_call(
        _combine_kernel,
        out_shape=jax.ShapeDtypeStruct((out_rows, D_MODEL), F32),
        grid=(t_tok // tm,),
        in_specs=in_specs,
        out_specs=pl.BlockSpec((tm, D_MODEL), lambda i: (i + tile0, 0)),
        input_output_aliases=aliases,
        compiler_params=pltpu.CompilerParams(dimension_semantics=("arbitrary",)),
        name="combine",
    )(*args)


def _moe_rows(x1_rows, logits_t, layer, br_t, w_gu, b_gu, w_down, b_down):
    t_tok = logits_t.shape[1]
    n_assign = t_tok * TOP_K
    bm = BM_EXPERT
    n_blocks = n_assign // bm + N_EXPERTS
    n_slots = n_blocks * bm

    idx_t, gates_pad, rank_t, counts_col, counts_row = _router(logits_t, layer, br_t)
    dest_km, table = _slots(idx_t, rank_t, counts_col, counts_row, n_blocks)

    tile = (ROW_WORDS_TILES, V7X_LANES)
    xs3 = _dispatch_rows(x1_rows.reshape(t_tok, *tile), dest_km, n_slots)
    ys_rows = _experts(table, xs3.reshape(n_slots * ROW_WORDS_TILES, V7X_LANES),
                       layer, w_gu, b_gu, w_down, b_down)
    yk3 = _gather_rows(ys_rows.reshape(n_slots, *tile), dest_km)
    return yk3.reshape(TOP_K, t_tok * ROW_WORDS_TILES, V7X_LANES), gates_pad


def kernel(x, w_in, b_in, conv_a, w_out_a, w_pool, scale_pool, conv_c, conv_c_b, ln_c_g, ln_c_b,
           w_out_c, b_out_c, w_o, ln1_g, ln1_b, w_router, b_router, w_gu, b_gu, w_down, b_down,
           ln2_g, ln2_b):
    bsz, seq_len, d = x.shape
    assert d == D_MODEL
    t_tok = bsz * seq_len
    depth = w_in.shape[0]

    def row(v):
        return v[:, None, :]

    wr_t = jnp.transpose(w_router, (0, 2, 1)).astype(BF16)
    mixer_consts = (w_in.astype(BF16), row(b_in), conv_a, w_out_a.astype(BF16),
                    w_pool.astype(BF16), row(scale_pool), conv_c, row(conv_c_b), row(ln_c_g),
                    row(ln_c_b), w_out_c.astype(BF16), row(b_out_c), w_o.astype(BF16),
                    row(ln1_g), row(ln1_b), wr_t)
    br_t = jnp.broadcast_to(b_router[:, :, None], (depth, N_EXPERTS, TR_ROUTE))
    n_le = depth * N_EXPERTS
    w_gu_f = w_gu.reshape(n_le, D_MODEL, 2 * D_FF)
    b_gu_d = jnp.concatenate([b_gu[..., 0::2], b_gu[..., 1::2]], axis=-1).reshape(n_le, 1, 2 * D_FF)
    w_down_f = w_down.reshape(n_le, D_FF, D_MODEL)
    b_down_d = b_down.reshape(n_le, 1, D_MODEL)
    ln2_g_r, ln2_b_r = row(ln2_g), row(ln2_b)

    assert bsz % TOKEN_GROUPS == 0
    per_group = t_tok // TOKEN_GROUPS
    x2 = x.reshape(t_tok, d)
    state = [None] * TOKEN_GROUPS
    for layer in range(depth):
        for g in range(TOKEN_GROUPS):
            if layer == 0:
                x1, x1_rows, logits_t = _mixer((x2,), layer, mixer_consts, seq_len=seq_len,
                                               n_tok=per_group, row0=g * per_group)
            else:
                x1, x1_rows, logits_t = _mixer(state[g], layer, mixer_consts, seq_len=seq_len,
                                               n_tok=per_group,
                                               combine_params=(ln2_g_r, ln2_b_r, layer - 1))
            yk4, gates = _moe_rows(x1_rows, logits_t, layer, br_t, w_gu_f, b_gu_d, w_down_f,
                                   b_down_d)
            state[g] = (x1, yk4, gates)
    out = None
    for g, (x1, yk4, gates) in enumerate(state):
        out = _combine(x1, yk4, gates, depth - 1, ln2_g_r, ln2_b_r, out_rows=t_tok,
                       row0=g * per_group, out_so_far=out)
    return out.reshape(bsz, seq_len, d)
```

```python
import functools

import jax
import jax.numpy as jnp
from jax import lax
from jax.experimental import pallas as pl
from jax.experimental.pallas import tpu as pltpu
from jax.experimental.pallas import tpu_sc as plsc

D_MODEL = 1024
DEPTH = 4
CONV_A_WIDTH = 3
POOL_WINDOWS = (2, 4, 8, 16)
POOL_GROUP_DIM = D_MODEL // len(POOL_WINDOWS)
CONV_C_WIDTH = 31
N_EXPERTS = 32
TOP_K = 4
D_FF = D_MODEL
SWIGLU_LIMIT = 7.0
SWIGLU_ALPHA = 1.702
LN_EPS = 1e-5
DEEPNORM_ALPHA = (2.0 * DEPTH) ** 0.25

V7X_LANES = 128
V7X_SUBLANES = 8
V7X_VMEM_LIMIT_BYTES = 56 * 1024 * 1024
V7X_MXU_DIM = 256
V7X_SC_CORES = 2
V7X_SC_SUBCORES = 16
V7X_SC_WORKERS = V7X_SC_CORES * V7X_SC_SUBCORES

TM_MIX = 256
HALO_A = 8
HALO_P = 16
HALO_C = 32
CONV_LANE_CHUNK = 128
TAP_ROW_BLOCK = 64
CONV_C_ROW_BLOCK = 64
MIX_COL_ORDER = (4, 5, 1, 2, 0, 6, 3, 7, 8)
TR_ROUTE = 512
GATE_LANES = V7X_LANES
BM_EXPERT = 1024
EXPERT_TAIL_ROWS = (128, 256, 384, 512, 768)
TM_COMB = 512
DEINTERLEAVE_COLS = V7X_MXU_DIM
ROW_WORDS_TILES = D_MODEL // 2 // V7X_LANES
SC_ROWS = 32
SLOT_TABLE_LANES = 512
TOKEN_GROUPS = 2
NEG_BIG = -3.0e38

F32 = jnp.float32
BF16 = jnp.bfloat16


def _layer_norm(x, g, b):
    mu = jnp.mean(x, axis=-1, keepdims=True)
    xc = x - mu
    var = jnp.mean(xc * xc, axis=-1, keepdims=True)
    return xc * lax.rsqrt(var + LN_EPS) * g + b


def _store_token_rows(ref, val):
    rows = val.shape[0]
    words = pltpu.pack_elementwise([val[:, :D_MODEL // 2], val[:, D_MODEL // 2:]],
                                   packed_dtype=BF16)
    for g in range(ROW_WORDS_TILES):
        ref[pl.ds(g, rows, stride=ROW_WORDS_TILES), :] = words[:, g * V7X_LANES:(g + 1) * V7X_LANES]


def _load_token_rows(ref, rows):
    words = jnp.concatenate([ref[pl.ds(g, rows, stride=ROW_WORDS_TILES), :]
                             for g in range(ROW_WORDS_TILES)], axis=1)
    halves = [pltpu.unpack_elementwise(words, index=i, packed_dtype=BF16, unpacked_dtype=F32)
              for i in range(2)]
    return jnp.concatenate(halves, axis=1)


def _combine_rows(x1, yk_ref, gates, ln_g, ln_b):
    rows = x1.shape[0]
    m = jnp.zeros((rows, D_MODEL), F32)
    for k in range(TOP_K):
        m = m + gates[:, k:k + 1] * _load_token_rows(yk_ref.at[k], rows)
    return _layer_norm(DEEPNORM_ALPHA * x1 + m, ln_g, ln_b)


def _realign(ext_ref, shift_ref, halo, rows, lane0, lanes, max_shift):
    n = halo + rows - V7X_SUBLANES
    for b in range(1, min(max_shift, V7X_SUBLANES - 1) + 1):
        shift_ref[b - 1, 0:n, lane0:lane0 + lanes] = (
            ext_ref[V7X_SUBLANES - b:V7X_SUBLANES - b + n, lane0:lane0 + lanes])


def _causal_taps(ext_ref, shift_ref, halo, rows, lane0, lanes, weights):
    outs = []
    for r0 in range(0, rows, TAP_ROW_BLOCK):
        acc = None
        for j, w in enumerate(weights):
            a, b = divmod(j, V7X_SUBLANES)
            if b == 0:
                start = halo - V7X_SUBLANES * a + r0
                term = ext_ref[start:start + TAP_ROW_BLOCK, lane0:lane0 + lanes]
            else:
                start = halo - V7X_SUBLANES * (a + 1) + r0
                term = shift_ref[b - 1, start:start + TAP_ROW_BLOCK, lane0:lane0 + lanes]
            if w is not None:
                term = term * w
            acc = term if acc is None else acc + term
        outs.append(acc)
    return jnp.concatenate(outs, axis=0)


N_MIXER_CONSTS = 16


def _mixer_kernel(*refs, tiles_per_seq, fused_combine):
    n_in = 5 if fused_combine else 1
    inputs, refs = refs[:n_in], refs[n_in:]
    (w_in_ref, b_in_ref, conv_a_ref, w_out_a_ref, w_pool_ref, scale_pool_ref, conv_c_ref,
     conv_c_b_ref, ln_c_g_ref, ln_c_b_ref, w_out_c_ref, b_out_c_ref, w_o_ref, ln1_g_ref,
     ln1_b_ref, wr_ref) = refs[:N_MIXER_CONSTS]
    (y_ref, y_rows_ref, logits_ref,
     x_f32, xb_ref, ext_a, ext_p, ext_c, shift_ref, v_ref) = refs[N_MIXER_CONSTS:]
    tm = y_ref.shape[0]
    tile_in_seq = pl.program_id(0) % tiles_per_seq

    @pl.when(tile_in_seq == 0)
    def _():
        ext_a[0:HALO_A, :] = jnp.zeros((HALO_A, D_MODEL), F32)
        ext_p[0:HALO_P, :] = jnp.zeros((HALO_P, D_MODEL), F32)
        ext_c[0:HALO_C, :] = jnp.zeros((HALO_C, D_MODEL), F32)

    if fused_combine:
        x_prev_ref, yk_ref, gate_ref, ln2_g_ref, ln2_b_ref = inputs
        x_f32[...] = _combine_rows(x_prev_ref[...], yk_ref, gate_ref[...], ln2_g_ref[0],
                                   ln2_b_ref[0])
    else:
        x_f32[...] = inputs[0][...]
    xb_ref[...] = x_f32[...].astype(BF16)

    def proj(slot):
        lo = MIX_COL_ORDER[slot] * D_MODEL
        return (jnp.dot(xb_ref[...], w_in_ref[0, :, lo:lo + D_MODEL], preferred_element_type=F32)
                + b_in_ref[0, :, lo:lo + D_MODEL])

    ext_c[HALO_C:HALO_C + tm, :] = proj(0) * jax.nn.sigmoid(proj(1))
    ext_a[HALO_A:HALO_A + tm, :] = proj(2) * proj(3)
    _realign(ext_c, shift_ref, HALO_C, tm, 0, D_MODEL, CONV_C_WIDTH - 1)
    taps_of_copy = {}
    for j in range(CONV_C_WIDTH):
        a, b = divmod(j, V7X_SUBLANES)
        taps_of_copy.setdefault(b, []).append((a, j))

    def conv_block(i, carry):
        r0 = i * CONV_C_ROW_BLOCK
        for c0 in range(0, D_MODEL, CONV_LANE_CHUNK):
            lanes = slice(c0, c0 + CONV_LANE_CHUNK)
            acc = conv_c_b_ref[0, :, lanes]
            for b, taps in taps_of_copy.items():
                a_max = max(a for a, _ in taps)
                first = HALO_C - V7X_SUBLANES * (a_max + (1 if b else 0))
                start = pl.multiple_of(r0 + first, V7X_SUBLANES)
                span = CONV_C_ROW_BLOCK + V7X_SUBLANES * a_max
                if b == 0:
                    win = ext_c[pl.ds(start, span), lanes]
                else:
                    win = shift_ref[b - 1, pl.ds(start, span), lanes]
                for a, j in taps:
                    off = V7X_SUBLANES * (a_max - a)
                    w = conv_c_ref[0, CONV_C_WIDTH - 1 - j:CONV_C_WIDTH - j, lanes]
                    acc = acc + win[off:off + CONV_C_ROW_BLOCK] * w
            v_ref[pl.ds(pl.multiple_of(r0, CONV_C_ROW_BLOCK), CONV_C_ROW_BLOCK), lanes] = acc
        return carry

    lax.fori_loop(0, tm // CONV_C_ROW_BLOCK, conv_block, 0)
    ext_c[0:HALO_C, :] = ext_c[tm:tm + HALO_C, :]

    b_a = proj(4)
    gate_a = jax.nn.sigmoid(proj(5))
    parts = []
    for c0 in range(0, D_MODEL, CONV_LANE_CHUNK):
        w = [conv_a_ref[0, CONV_A_WIDTH - 1 - j:CONV_A_WIDTH - j, c0:c0 + CONV_LANE_CHUNK]
             for j in range(CONV_A_WIDTH)]
        _realign(ext_a, shift_ref, HALO_A, tm, c0, CONV_LANE_CHUNK, CONV_A_WIDTH - 1)
        parts.append(_causal_taps(ext_a, shift_ref, HALO_A, tm, c0, CONV_LANE_CHUNK, w))
    u_a = jnp.concatenate(parts, axis=1)
    ext_a[0:HALO_A, :] = ext_a[tm:tm + HALO_A, :]
    y_a = jnp.dot((b_a * u_a).astype(BF16), w_out_a_ref[0], preferred_element_type=F32)
    merged = gate_a * y_a

    v = _layer_norm(v_ref[...], ln_c_g_ref[0], ln_c_b_ref[0])
    v = v * jax.nn.sigmoid(v)
    y_c = jnp.dot(v.astype(BF16), w_out_c_ref[0], preferred_element_type=F32) + b_out_c_ref[0]
    merged = merged + jax.nn.sigmoid(proj(8)) * y_c

    p_in = proj(6)
    ext_p[HALO_P:HALO_P + tm, :] = p_in
    pos = tile_in_seq * tm + lax.broadcasted_iota(jnp.int32, (tm, POOL_GROUP_DIM), 0)
    parts = []
    for g, win in enumerate(POOL_WINDOWS):
        lo = g * POOL_GROUP_DIM
        sub = []
        for c0 in range(lo, lo + POOL_GROUP_DIM, CONV_LANE_CHUNK):
            _realign(ext_p, shift_ref, HALO_P, tm, c0, CONV_LANE_CHUNK, win - 1)
            sub.append(_causal_taps(ext_p, shift_ref, HALO_P, tm, c0, CONV_LANE_CHUNK,
                                    [None] * win))
        wsum = jnp.concatenate(sub, axis=1)
        cnt = jnp.minimum(pos + 1, win).astype(F32)
        pooled = wsum / cnt - p_in[:, lo:lo + POOL_GROUP_DIM]
        parts.append(jnp.dot(pooled.astype(BF16), w_pool_ref[0, g], preferred_element_type=F32))
    ext_p[0:HALO_P, :] = ext_p[tm:tm + HALO_P, :]
    y_b = jnp.concatenate(parts, axis=1) * scale_pool_ref[0]
    merged = merged + jax.nn.sigmoid(proj(7)) * y_b

    h = jnp.dot(merged.astype(BF16), w_o_ref[0], preferred_element_type=F32)
    y = _layer_norm(DEEPNORM_ALPHA * x_f32[...] + h, ln1_g_ref[0], ln1_b_ref[0])
    y_ref[...] = y
    _store_token_rows(y_rows_ref, y)
    logits_ref[...] = lax.dot_general(wr_ref[0], y.astype(BF16), (((1,), (1,)), ((), ())),
                                      preferred_element_type=F32)


def _layer_block(arr, layer, buffered_once=False):
    tail = (0,) * (arr.ndim - 1)
    mode = {"pipeline_mode": pl.Buffered(1)} if buffered_once else {}
    return pl.BlockSpec((1,) + arr.shape[1:], lambda *_: (layer,) + tail, **mode)


def _mixer(inputs, layer, consts, *, seq_len, n_tok, row0=0, combine_params=None):
    tm = TM_MIX
    assert seq_len % tm == 0 and n_tok % seq_len == 0 and row0 % tm == 0
    assert len(consts) == N_MIXER_CONSTS
    tile0 = row0 // tm
    tok_spec = pl.BlockSpec((tm, D_MODEL), lambda i: (i + tile0, 0))
    if combine_params is None:
        in_specs, args = [tok_spec], list(inputs)
    else:
        ln_g, ln_b, prev_layer = combine_params
        in_specs = [tok_spec,
                    pl.BlockSpec((TOP_K, tm * ROW_WORDS_TILES, V7X_LANES), lambda i: (0, i, 0)),
                    pl.BlockSpec((tm, GATE_LANES), lambda i: (i, 0)),
                    _layer_block(ln_g, prev_layer), _layer_block(ln_b, prev_layer)]
        args = list(inputs) + [ln_g, ln_b]
    return pl.pallas_call(
        functools.partial(_mixer_kernel, tiles_per_seq=seq_len // tm,
                          fused_combine=combine_params is not None),
        out_shape=(jax.ShapeDtypeStruct((n_tok, D_MODEL), F32),
                   jax.ShapeDtypeStruct((n_tok * ROW_WORDS_TILES, V7X_LANES), jnp.int32),
                   jax.ShapeDtypeStruct((N_EXPERTS, n_tok), F32)),
        grid=(n_tok // tm,),
        in_specs=in_specs + [_layer_block(c, layer, buffered_once=True) for c in consts],
        out_specs=(pl.BlockSpec((tm, D_MODEL), lambda i: (i, 0)),
                   pl.BlockSpec((tm * ROW_WORDS_TILES, V7X_LANES), lambda i: (i, 0)),
                   pl.BlockSpec((N_EXPERTS, tm), lambda i: (0, i))),
        scratch_shapes=[pltpu.VMEM((tm, D_MODEL), F32),
                        pltpu.VMEM((tm, D_MODEL), BF16),
                        pltpu.VMEM((HALO_A + tm, D_MODEL), F32),
                        pltpu.VMEM((HALO_P + tm, D_MODEL), F32),
                        pltpu.VMEM((HALO_C + tm, D_MODEL), F32),
                        pltpu.VMEM((V7X_SUBLANES - 1, HALO_C + tm - V7X_SUBLANES, D_MODEL), F32),
                        pltpu.VMEM((tm, D_MODEL), F32)],
        compiler_params=pltpu.CompilerParams(dimension_semantics=("arbitrary",),
                                             vmem_limit_bytes=V7X_VMEM_LIMIT_BYTES),
        name="mixer",
    )(*args, *consts)


def _router_kernel(logits_ref, br_ref, idx_ref, gate_ref, rank_ref, counts_ref, counts_row_ref,
                   run_ref, run_row_ref):
    tr = logits_ref.shape[1]

    @pl.when(pl.program_id(0) == 0)
    def _():
        run_ref[...] = jnp.zeros(run_ref.shape, F32)
        run_row_ref[...] = jnp.zeros(run_row_ref.shape, F32)

    expert = lax.broadcasted_iota(jnp.int32, (N_EXPERTS, tr), 0)
    work = logits_ref[...] + br_ref[0]
    vals, idxs = [], []
    for _ in range(TOP_K):
        m = jnp.max(work, axis=0, keepdims=True)
        idx = jnp.min(jnp.where(work == m, expert, N_EXPERTS), axis=0, keepdims=True)
        vals.append(m)
        idxs.append(idx)
        work = jnp.where(expert == idx, NEG_BIG, work)
    exps = [jnp.exp(v - vals[0]) for v in vals]
    denom = exps[0] + exps[1] + exps[2] + exps[3]

    sel = jnp.zeros((N_EXPERTS, tr), F32)
    for k in range(TOP_K):
        sel = sel + jnp.where(expert == idxs[k], 1.0, 0.0)
    earlier = lax.broadcasted_iota(jnp.int32, (tr, tr), 0)
    later = lax.broadcasted_iota(jnp.int32, (tr, tr), 1)
    before = jnp.where(earlier < later, 1.0, 0.0).astype(BF16)
    prior = jnp.dot(sel.astype(BF16), before, preferred_element_type=F32) + run_ref[:, 0:1]

    row = lax.broadcasted_iota(jnp.int32, (V7X_SUBLANES, tr), 0)
    idx_out = jnp.zeros((V7X_SUBLANES, tr), jnp.int32)
    gate_out = jnp.zeros((V7X_SUBLANES, tr), F32)
    rank_out = jnp.zeros((V7X_SUBLANES, tr), F32)
    for k in range(TOP_K):
        rank_k = jnp.sum(jnp.where(expert == idxs[k], prior, 0.0), axis=0, keepdims=True)
        idx_out = jnp.where(row == k, idxs[k], idx_out)
        gate_out = jnp.where(row == k, exps[k] / denom, gate_out)
        rank_out = jnp.where(row == k, rank_k, rank_out)
    idx_ref[...] = idx_out
    rank_ref[...] = rank_out.astype(jnp.int32)
    gate_rows = jnp.concatenate(
        [gate_out, jnp.zeros((GATE_LANES - V7X_SUBLANES, tr), F32)], axis=0)
    gate_ref[...] = gate_rows.T

    run_ref[...] = run_ref[...] + jnp.sum(sel, axis=1, keepdims=True)
    counts_ref[...] = run_ref[...]
    per_tile = lax.dot_general(jnp.ones((V7X_SUBLANES, tr), BF16), sel.astype(BF16),
                               (((1,), (1,)), ((), ())), preferred_element_type=F32)
    run_row_ref[:, 0:N_EXPERTS] = run_row_ref[:, 0:N_EXPERTS] + per_tile
    counts_row_ref[...] = run_row_ref[...]


def _router(logits_t, layer, br_t):
    t_tok = logits_t.shape[1]
    tr = TR_ROUTE
    tok_spec = pl.BlockSpec((V7X_SUBLANES, tr), lambda i: (0, i))
    return pl.pallas_call(
        _router_kernel,
        out_shape=(jax.ShapeDtypeStruct((V7X_SUBLANES, t_tok), jnp.int32),
                   jax.ShapeDtypeStruct((t_tok, GATE_LANES), F32),
                   jax.ShapeDtypeStruct((V7X_SUBLANES, t_tok), jnp.int32),
                   jax.ShapeDtypeStruct((N_EXPERTS, V7X_LANES), F32),
                   jax.ShapeDtypeStruct((V7X_SUBLANES, V7X_LANES), F32)),
        grid=(t_tok // tr,),
        in_specs=[pl.BlockSpec((N_EXPERTS, tr), lambda i: (0, i)), _layer_block(br_t, layer)],
        out_specs=(tok_spec, pl.BlockSpec((tr, GATE_LANES), lambda i: (i, 0)), tok_spec,
                   pl.BlockSpec((N_EXPERTS, V7X_LANES), lambda i: (0, 0)),
                   pl.BlockSpec((V7X_SUBLANES, V7X_LANES), lambda i: (0, 0))),
        scratch_shapes=[pltpu.VMEM((N_EXPERTS, V7X_LANES), F32),
                        pltpu.VMEM((V7X_SUBLANES, V7X_LANES), F32)],
        compiler_params=pltpu.CompilerParams(dimension_semantics=("arbitrary",)),
        name="router",
    )(logits_t, br_t)


def _slots_kernel(idx_ref, rank_ref, counts_ref, counts_row_ref, dest_ref, table_ref, *,
                  n_blocks):
    bm = BM_EXPERT
    shift = bm.bit_length() - 1
    assert bm == 1 << shift
    width = table_ref.shape[1]
    cnt_col = counts_ref[:, 0:1].astype(jnp.int32)
    cnt_row = counts_row_ref[0:1, :].astype(jnp.int32)
    nblk_col = lax.shift_right_logical(cnt_col + (bm - 1), shift)
    nblk_row = lax.shift_right_logical(cnt_row + (bm - 1), shift)
    e_sub = lax.broadcasted_iota(jnp.int32, (N_EXPERTS, V7X_LANES), 0)
    e_lane = lax.broadcasted_iota(jnp.int32, (N_EXPERTS, V7X_LANES), 1)
    blk_start = jnp.sum(jnp.where(e_lane < e_sub, nblk_row.astype(F32), 0.0), axis=1,
                        keepdims=True).astype(jnp.int32)
    blk_end = blk_start + nblk_col
    row_start = blk_start * bm

    dest = rank_ref[...]
    idx = idx_ref[...]
    for e in range(N_EXPERTS):
        dest = dest + jnp.where(idx == e, row_start[e:e + 1, :], 0)
    dest_ref[...] = dest

    b_lane = lax.broadcasted_iota(jnp.int32, (N_EXPERTS, width), 1)
    e_of = lax.broadcasted_iota(jnp.int32, (N_EXPERTS, width), 0)
    block_e = jnp.minimum(jnp.sum(jnp.where(blk_end <= b_lane, 1, 0), axis=0, keepdims=True),
                          N_EXPERTS - 1)
    mine = e_of == block_e
    cnt_of = jnp.sum(jnp.where(mine, cnt_col, 0), axis=0, keepdims=True)
    start_of = jnp.sum(jnp.where(mine, blk_start, 0), axis=0, keepdims=True)
    nb_used = jnp.sum(nblk_col, axis=0, keepdims=True)
    b_row = b_lane[0:1, :]
    block_rows = jnp.clip(cnt_of - (b_row - start_of) * bm, 0, bm)
    block_rows = jnp.where((b_row < nb_used) & (b_row < n_blocks), block_rows, 0)
    later = jnp.where((e_sub > e_lane) & (cnt_col > 0), e_sub, N_EXPERTS)
    next_e = jnp.min(later, axis=0, keepdims=True)
    next_e = jnp.where(next_e == N_EXPERTS, -1, next_e)

    row = lax.broadcasted_iota(jnp.int32, (V7X_SUBLANES, width), 0)
    lane = lax.broadcasted_iota(jnp.int32, (V7X_SUBLANES, width), 1)
    next_wide = jnp.concatenate(
        [next_e, jnp.full((1, width - V7X_LANES), -1, jnp.int32)], axis=1)
    table = jnp.where(row == 0, block_e, 0)
    table = jnp.where(row == 1, block_rows, table)
    table = jnp.where(row == 2, next_wide, table)
    table = jnp.where((row == 3) & (lane == 0), nb_used, table)
    table_ref[...] = table


def _slots(idx_t, rank_t, counts_col, counts_row, n_blocks):
    assert n_blocks <= SLOT_TABLE_LANES
    return pl.pallas_call(
        functools.partial(_slots_kernel, n_blocks=n_blocks),
        out_shape=(jax.ShapeDtypeStruct(idx_t.shape, jnp.int32),
                   jax.ShapeDtypeStruct((V7X_SUBLANES, SLOT_TABLE_LANES), jnp.int32)),
        name="slots",
    )(idx_t, rank_t, counts_col, counts_row)


def _sc_mesh():
    return plsc.VectorSubcoreMesh(core_axis_name="c", subcore_axis_name="s")


def _sc_worker_id():
    return lax.axis_index("s") * V7X_SC_CORES + lax.axis_index("c")


def _dispatch_rows(x3, dest_km, n_slots):
    t_tok = x3.shape[0]
    tok_per_worker = t_tok // V7X_SC_WORKERS
    assert tok_per_worker % SC_ROWS == 0

    def body(x_hbm, dest_hbm, out_hbm, rows_v, idx_v, sem):
        base = _sc_worker_id() * tok_per_worker

        @pl.loop(0, tok_per_worker // SC_ROWS)
        def _(step):
            t0 = pl.multiple_of(base + step * SC_ROWS, SC_ROWS)
            pltpu.sync_copy(x_hbm.at[pl.ds(t0, SC_ROWS)], rows_v)
            for k in range(TOP_K):
                pltpu.sync_copy(dest_hbm.at[k, pl.ds(t0, SC_ROWS)], idx_v.at[k])
            for k in range(TOP_K):
                pltpu.async_copy(rows_v, out_hbm.at[idx_v.at[k]], sem).wait()

    return pl.kernel(
        body, mesh=_sc_mesh(),
        out_type=jax.ShapeDtypeStruct((n_slots,) + x3.shape[1:], x3.dtype),
        scratch_types=[pltpu.VMEM((SC_ROWS,) + x3.shape[1:], x3.dtype),
                       pltpu.VMEM((TOP_K, SC_ROWS), jnp.int32),
                       pltpu.SemaphoreType.DMA],
    )(x3, dest_km)


def _gather_rows(ys3, dest_km):
    t_tok = dest_km.shape[1]
    workers_per_k = V7X_SC_WORKERS // TOP_K
    tok_per_worker = t_tok // workers_per_k
    assert V7X_SC_WORKERS % TOP_K == 0 and tok_per_worker % SC_ROWS == 0

    def body(ys_hbm, src_hbm, out_hbm, rows_v, idx_v, sem):
        wid = _sc_worker_id()
        k = wid // workers_per_k
        base = (wid % workers_per_k) * tok_per_worker

        @pl.loop(0, tok_per_worker // SC_ROWS)
        def _(step):
            t0 = pl.multiple_of(base + step * SC_ROWS, SC_ROWS)
            pltpu.sync_copy(src_hbm.at[k, pl.ds(t0, SC_ROWS)], idx_v)
            pltpu.async_copy(ys_hbm.at[idx_v], rows_v, sem).wait()
            pltpu.sync_copy(rows_v, out_hbm.at[pl.ds(pl.multiple_of(k * t_tok + t0, SC_ROWS),
                                                     SC_ROWS)])

    return pl.kernel(
        body, mesh=_sc_mesh(),
        out_type=jax.ShapeDtypeStruct((TOP_K * t_tok,) + ys3.shape[1:], ys3.dtype),
        scratch_types=[pltpu.VMEM((SC_ROWS,) + ys3.shape[1:], ys3.dtype),
                       pltpu.VMEM((SC_ROWS,), jnp.int32),
                       pltpu.SemaphoreType.DMA],
    )(ys3, dest_km)


def _expert_kernel(table_ref, xs_ref, w_gu_hbm, b_gu_ref, w_down_hbm, b_down_ref, ys_ref,
                   w_gu_f, w_down_f, w_gu_b, w_down_b, sem_gu, sem_down, slot_ref, *, layer):
    bm = BM_EXPERT
    b = pl.program_id(0)
    e = table_ref[0, b]
    n_rows = table_ref[1, b]
    live = n_rows > 0
    new_expert = jnp.logical_or(b == 0, e != table_ref[0, jnp.maximum(b - 1, 0)])

    def weight_copies(expert, slot):
        idx = layer * N_EXPERTS + expert
        return (pltpu.make_async_copy(w_gu_hbm.at[idx], w_gu_f.at[slot], sem_gu.at[slot]),
                pltpu.make_async_copy(w_down_hbm.at[idx], w_down_f.at[slot], sem_down.at[slot]))

    @pl.when(b == 0)
    def _():
        slot_ref[0] = 0
        for copy in weight_copies(e, 0):
            copy.start()

    @pl.when(jnp.logical_and(new_expert, live))
    def _():
        slot = slot_ref[0]
        nxt = table_ref[2, e]

        @pl.when(nxt >= 0)
        def _():
            for copy in weight_copies(nxt, 1 - slot):
                copy.start()

        for copy in weight_copies(e, slot):
            copy.wait()
        slot_ref[0] = 1 - slot

        cb = DEINTERLEAVE_COLS
        src = lax.broadcasted_iota(jnp.int32, (cb, cb), 0)
        dst = lax.broadcasted_iota(jnp.int32, (cb, cb), 1)
        pick = jnp.where(dst < cb // 2, 2 * dst, 2 * (dst - cb // 2) + 1)
        sel = jnp.where(src == pick, 1.0, 0.0).astype(BF16)
        for c in range(2 * D_FF // cb):
            blk = jnp.dot(w_gu_f[slot, :, c * cb:(c + 1) * cb].astype(BF16), sel,
                          preferred_element_type=F32).astype(BF16)
            lo = c * (cb // 2)
            w_gu_b[:, lo:lo + cb // 2] = blk[:, :cb // 2]
            w_gu_b[:, D_FF + lo:D_FF + lo + cb // 2] = blk[:, cb // 2:]
        w_down_b[...] = w_down_f[slot].astype(BF16)

    def expert_rows(rows):
        xb = _load_token_rows(xs_ref, rows).astype(BF16)
        h = jnp.dot(xb, w_gu_b[...], preferred_element_type=F32) + b_gu_ref[e]
        gate = jnp.minimum(h[:, :D_FF], SWIGLU_LIMIT)
        up = jnp.clip(h[:, D_FF:], -SWIGLU_LIMIT, SWIGLU_LIMIT)
        act = (up + 1.0) * (gate * jax.nn.sigmoid(SWIGLU_ALPHA * gate))
        y = jnp.dot(act.astype(BF16), w_down_b[...], preferred_element_type=F32) + b_down_ref[e]
        _store_token_rows(ys_ref, y)

    sizes = list(EXPERT_TAIL_ROWS) + [bm]
    for lo, rows in zip([0] + sizes[:-1], sizes):
        @pl.when(jnp.logical_and(n_rows > lo, n_rows <= rows))
        def _(rows=rows):
            expert_rows(rows)


def _experts(table, xs_rows, layer, w_gu, b_gu, w_down, b_down):
    bm = BM_EXPERT
    n_blocks = xs_rows.shape[0] // (bm * ROW_WORDS_TILES)

    def row_map(b, tbl):
        return (jnp.minimum(b, tbl[3, 0] - 1), 0)

    def layer_biases(width):
        return pl.BlockSpec((N_EXPERTS, 1, width), lambda b, tbl: (layer, 0, 0))

    row_block = (bm * ROW_WORDS_TILES, V7X_LANES)
    grid_spec = pltpu.PrefetchScalarGridSpec(
        num_scalar_prefetch=1,
        grid=(n_blocks,),
        in_specs=[pl.BlockSpec(row_block, row_map),
                  pl.BlockSpec(memory_space=pl.ANY),
                  layer_biases(2 * D_FF),
                  pl.BlockSpec(memory_space=pl.ANY),
                  layer_biases(D_MODEL)],
        out_specs=pl.BlockSpec(row_block, row_map),
        scratch_shapes=[pltpu.VMEM((2, D_MODEL, 2 * D_FF), F32),
                        pltpu.VMEM((2, D_FF, D_MODEL), F32),
                        pltpu.VMEM((D_MODEL, 2 * D_FF), BF16),
                        pltpu.VMEM((D_FF, D_MODEL), BF16),
                        pltpu.SemaphoreType.DMA((2,)),
                        pltpu.SemaphoreType.DMA((2,)),
                        pltpu.SMEM((1,), jnp.int32)],
    )
    return pl.pallas_call(
        functools.partial(_expert_kernel, layer=layer),
        out_shape=jax.ShapeDtypeStruct(xs_rows.shape, xs_rows.dtype),
        grid_spec=grid_spec,
        compiler_params=pltpu.CompilerParams(dimension_semantics=("arbitrary",),
                                             vmem_limit_bytes=V7X_VMEM_LIMIT_BYTES),
        name="experts",
    )(table, xs_rows, w_gu, b_gu, w_down, b_down)


def _combine_kernel(x_ref, yk_ref, gate_ref, g_ref, b_ref, *out_refs):
    o_ref = out_refs[-1]
    o_ref[...] = _combine_rows(x_ref[...], yk_ref, gate_ref[...], g_ref[0], b_ref[0])


def _combine(x1, yk4, gates_pad, layer, ln_g, ln_b, *, out_rows, row0, out_so_far=None):
    t_tok = x1.shape[0]
    tm = TM_COMB
    assert row0 % tm == 0
    tile0 = row0 // tm
    in_specs = [pl.BlockSpec((tm, D_MODEL), lambda i: (i, 0)),
                pl.BlockSpec((TOP_K, tm * ROW_WORDS_TILES, V7X_LANES), lambda i: (0, i, 0)),
                pl.BlockSpec((tm, GATE_LANES), lambda i: (i, 0)),
                _layer_block(ln_g, layer), _layer_block(ln_b, layer)]
    args = [x1, yk4, gates_pad, ln_g, ln_b]
    aliases = {}
    if out_so_far is not None:
        in_specs.append(pl.BlockSpec(memory_space=pl.ANY))
        args.append(out_so_far)
        aliases = {len(args) - 1: 0}
    return pl.pallas_call(
        _combine_kernel,
        out_shape=jax.ShapeDtypeStruct((out_rows, D_MODEL), F32),
        grid=(t_tok // tm,),
        in_specs=in_specs,
        out_specs=pl.BlockSpec((tm, D_MODEL), lambda i: (i + tile0, 0)),
        input_output_aliases=aliases,
        compiler_params=pltpu.CompilerParams(dimension_semantics=("arbitrary",)),
        name="combine",
    )(*args)


def _moe_rows(x1_rows, logits_t, layer, br_t, w_gu, b_gu, w_down, b_down):
    t_tok = logits_t.shape[1]
    n_assign = t_tok * TOP_K
    bm = BM_EXPERT
    n_blocks = n_assign // bm + N_EXPERTS
    n_slots = n_blocks * bm

    idx_t, gates_pad, rank_t, counts_col, counts_row = _router(logits_t, layer, br_t)
    dest_km, table = _slots(idx_t, rank_t, counts_col, counts_row, n_blocks)

    tile = (ROW_WORDS_TILES, V7X_LANES)
    xs3 = _dispatch_rows(x1_rows.reshape(t_tok, *tile), dest_km, n_slots)
    ys_rows = _experts(table, xs3.reshape(n_slots * ROW_WORDS_TILES, V7X_LANES),
                       layer, w_gu, b_gu, w_down, b_down)
    yk3 = _gather_rows(ys_rows.reshape(n_slots, *tile), dest_km)
    return yk3.reshape(TOP_K, t_tok * ROW_WORDS_TILES, V7X_LANES), gates_pad


def kernel(x, w_in, b_in, conv_a, w_out_a, w_pool, scale_pool, conv_c, conv_c_b, ln_c_g, ln_c_b,
           w_out_c, b_out_c, w_o, ln1_g, ln1_b, w_router, b_router, w_gu, b_gu, w_down, b_down,
           ln2_g, ln2_b):
    bsz, seq_len, d = x.shape
    assert d == D_MODEL
    t_tok = bsz * seq_len
    depth = w_in.shape[0]

    def row(v):
        return v[:, None, :]

    wr_t = jnp.transpose(w_router, (0, 2, 1)).astype(BF16)
    mixer_consts = (w_in.astype(BF16), row(b_in), conv_a, w_out_a.astype(BF16),
                    w_pool.astype(BF16), row(scale_pool), conv_c, row(conv_c_b), row(ln_c_g),
                    row(ln_c_b), w_out_c.astype(BF16), row(b_out_c), w_o.astype(BF16),
                    row(ln1_g), row(ln1_b), wr_t)
    br_t = jnp.broadcast_to(b_router[:, :, None], (depth, N_EXPERTS, TR_ROUTE))
    n_le = depth * N_EXPERTS
    w_gu_f = w_gu.reshape(n_le, D_MODEL, 2 * D_FF)
    b_gu_d = jnp.concatenate([b_gu[..., 0::2], b_gu[..., 1::2]], axis=-1).reshape(n_le, 1, 2 * D_FF)
    w_down_f = w_down.reshape(n_le, D_FF, D_MODEL)
    b_down_d = b_down.reshape(n_le, 1, D_MODEL)
    ln2_g_r, ln2_b_r = row(ln2_g), row(ln2_b)

    assert bsz % TOKEN_GROUPS == 0
    per_group = t_tok // TOKEN_GROUPS
    x2 = x.reshape(t_tok, d)
    state = [None] * TOKEN_GROUPS
    for layer in range(depth):
        for g in range(TOKEN_GROUPS):
            if layer == 0:
                x1, x1_rows, logits_t = _mixer((x2,), layer, mixer_consts, seq_len=seq_len,
                                               n_tok=per_group, row0=g * per_group)
            else:
                x1, x1_rows, logits_t = _mixer(state[g], layer, mixer_consts, seq_len=seq_len,
                                               n_tok=per_group,
                                               combine_params=(ln2_g_r, ln2_b_r, layer - 1))
            yk4, gates = _moe_rows(x1_rows, logits_t, layer, br_t, w_gu_f, b_gu_d, w_down_f,
                                   b_down_d)
            state[g] = (x1, yk4, gates)
    out = None
    for g, (x1, yk4, gates) in enumerate(state):
        out = _combine(x1, yk4, gates, depth - 1, ln2_g_r, ln2_b_r, out_rows=t_tok,
                       row0=g * per_group, out_so_far=out)
    return out.reshape(bsz, seq_len, d)
```

```python
import functools

import jax
import jax.numpy as jnp
from jax import lax
from jax.experimental import pallas as pl
from jax.experimental.pallas import tpu as pltpu
from jax.experimental.pallas import tpu_sc as plsc

D_MODEL = 1024
DEPTH = 4
CONV_A_WIDTH = 3
POOL_WINDOWS = (2, 4, 8, 16)
POOL_GROUP_DIM = D_MODEL // len(POOL_WINDOWS)
CONV_C_WIDTH = 31
N_EXPERTS = 32
TOP_K = 4
D_FF = D_MODEL
SWIGLU_LIMIT = 7.0
SWIGLU_ALPHA = 1.702
LN_EPS = 1e-5
DEEPNORM_ALPHA = (2.0 * DEPTH) ** 0.25

V7X_LANES = 128
V7X_SUBLANES = 8
V7X_VMEM_LIMIT_BYTES = 56 * 1024 * 1024
V7X_MXU_DIM = 256
V7X_SC_CORES = 2
V7X_SC_SUBCORES = 16
V7X_SC_WORKERS = V7X_SC_CORES * V7X_SC_SUBCORES

TM_MIX = 256
HALO_A = 8
HALO_P = 16
HALO_C = 32
CONV_LANE_CHUNK = 128
N_LANE_CHUNKS = D_MODEL // CONV_LANE_CHUNK
TAP_ROW_BLOCK = 64
CONV_C_STRIDE = TM_MIX // V7X_SUBLANES + 1
CONV_C_ROWS = V7X_SUBLANES * CONV_C_STRIDE
CONV_C_ACCUMULATORS = 11
MIX_COL_ORDER = (4, 5, 1, 2, 0, 6, 3, 7, 8)
TR_ROUTE = 512
GATE_LANES = V7X_LANES
BM_EXPERT = 1024
EXPERT_TAIL_ROWS = (128, 256, 384, 512, 768)
TM_COMB = 512
DEINTERLEAVE_COLS = V7X_MXU_DIM
ROW_WORDS_TILES = D_MODEL // 2 // V7X_LANES
SC_ROWS = 32
SLOT_TABLE_LANES = 512
TOKEN_GROUPS = 2
NEG_BIG = -3.0e38

F32 = jnp.float32
BF16 = jnp.bfloat16


def _layer_norm(x, g, b):
    mu = jnp.mean(x, axis=-1, keepdims=True)
    xc = x - mu
    var = jnp.mean(xc * xc, axis=-1, keepdims=True)
    return xc * lax.rsqrt(var + LN_EPS) * g + b


def _store_token_rows(ref, val):
    rows = val.shape[0]
    words = pltpu.pack_elementwise([val[:, :D_MODEL // 2], val[:, D_MODEL // 2:]],
                                   packed_dtype=BF16)
    for g in range(ROW_WORDS_TILES):
        ref[pl.ds(g, rows, stride=ROW_WORDS_TILES), :] = words[:, g * V7X_LANES:(g + 1) * V7X_LANES]


def _load_token_rows(ref, rows):
    words = jnp.concatenate([ref[pl.ds(g, rows, stride=ROW_WORDS_TILES), :]
                             for g in range(ROW_WORDS_TILES)], axis=1)
    halves = [pltpu.unpack_elementwise(words, index=i, packed_dtype=BF16, unpacked_dtype=F32)
              for i in range(2)]
    return jnp.concatenate(halves, axis=1)


def _combine_rows(x1, yk_ref, gates, ln_g, ln_b):
    rows = x1.shape[0]
    m = jnp.zeros((rows, D_MODEL), F32)
    for k in range(TOP_K):
        m = m + gates[:, k:k + 1] * _load_token_rows(yk_ref.at[k], rows)
    return _layer_norm(DEEPNORM_ALPHA * x1 + m, ln_g, ln_b)


def _realign(ext_ref, shift_ref, halo, rows, lane0, lanes, max_shift):
    n = halo + rows - V7X_SUBLANES
    for b in range(1, min(max_shift, V7X_SUBLANES - 1) + 1):
        shift_ref[b - 1, 0:n, lane0:lane0 + lanes] = (
            ext_ref[V7X_SUBLANES - b:V7X_SUBLANES - b + n, lane0:lane0 + lanes])


def _causal_taps(ext_ref, shift_ref, halo, rows, lane0, lanes, weights):
    outs = []
    for r0 in range(0, rows, TAP_ROW_BLOCK):
        acc = None
        for j, w in enumerate(weights):
            a, b = divmod(j, V7X_SUBLANES)
            if b == 0:
                start = halo - V7X_SUBLANES * a + r0
                term = ext_ref[start:start + TAP_ROW_BLOCK, lane0:lane0 + lanes]
            else:
                start = halo - V7X_SUBLANES * (a + 1) + r0
                term = shift_ref[b - 1, start:start + TAP_ROW_BLOCK, lane0:lane0 + lanes]
            if w is not None:
                term = term * w
            acc = term if acc is None else acc + term
        outs.append(acc)
    return jnp.concatenate(outs, axis=0)


N_MIXER_CONSTS = 16


def _mixer_kernel(*refs, tiles_per_seq, fused_combine):
    n_in = 5 if fused_combine else 1
    inputs, refs = refs[:n_in], refs[n_in:]
    (w_in_ref, b_in_ref, conv_a_ref, w_out_a_ref, w_pool_ref, scale_pool_ref, conv_c_ref,
     conv_c_b_ref, ln_c_g_ref, ln_c_b_ref, w_out_c_ref, b_out_c_ref, w_o_ref, ln1_g_ref,
     ln1_b_ref, wr_ref) = refs[:N_MIXER_CONSTS]
    (y_ref, y_rows_ref, logits_ref,
     x_f32, xb_ref, ext_a, ext_p, ext_c, shift_ref, v_ref) = refs[N_MIXER_CONSTS:]
    tm = y_ref.shape[0]
    tile_in_seq = pl.program_id(0) % tiles_per_seq

    @pl.when(tile_in_seq == 0)
    def _():
        ext_a[0:HALO_A, :] = jnp.zeros((HALO_A, D_MODEL), F32)
        ext_p[0:HALO_P, :] = jnp.zeros((HALO_P, D_MODEL), F32)
        ext_c[:, 0:HALO_C, :] = jnp.zeros((N_LANE_CHUNKS, HALO_C, CONV_LANE_CHUNK), F32)
        ext_c[:, HALO_C + tm:, :] = jnp.zeros(
            (N_LANE_CHUNKS, CONV_C_ROWS - tm, CONV_LANE_CHUNK), F32)

    if fused_combine:
        x_prev_ref, yk_ref, gate_ref, ln2_g_ref, ln2_b_ref = inputs
        x_f32[...] = _combine_rows(x_prev_ref[...], yk_ref, gate_ref[...], ln2_g_ref[0],
                                   ln2_b_ref[0])
    else:
        x_f32[...] = inputs[0][...]
    xb_ref[...] = x_f32[...].astype(BF16)

    def proj(slot):
        lo = MIX_COL_ORDER[slot] * D_MODEL
        return (jnp.dot(xb_ref[...], w_in_ref[0, :, lo:lo + D_MODEL], preferred_element_type=F32)
                + b_in_ref[0, :, lo:lo + D_MODEL])

    glu = proj(0) * jax.nn.sigmoid(proj(1))
    for c in range(N_LANE_CHUNKS):
        ext_c[c, HALO_C:HALO_C + tm, :] = glu[:, c * CONV_LANE_CHUNK:(c + 1) * CONV_LANE_CHUNK]
    ext_a[HALO_A:HALO_A + tm, :] = proj(2) * proj(3)
    stride = CONV_C_STRIDE
    for c in range(N_LANE_CHUNKS):
        lanes = slice(c * CONV_LANE_CHUNK, (c + 1) * CONV_LANE_CHUNK)
        w = [jnp.broadcast_to(conv_c_ref[0, CONV_C_WIDTH - 1 - j:CONV_C_WIDTH - j, lanes],
                              (V7X_SUBLANES, CONV_LANE_CHUNK)) for j in range(CONV_C_WIDTH)]
        bias = jnp.broadcast_to(conv_c_b_ref[0, :, lanes], (V7X_SUBLANES, CONV_LANE_CHUNK))
        for q0 in range(0, stride, CONV_C_ACCUMULATORS):
            qs = range(q0, min(q0 + CONV_C_ACCUMULATORS, stride))
            acc = {q: bias for q in qs}
            for s in range(qs[0] - (CONV_C_WIDTH - 1), qs[-1] + 1):
                win = ext_c[c, pl.ds(HALO_C + s, V7X_SUBLANES, stride=stride), :]
                for q in qs:
                    if 0 <= q - s < CONV_C_WIDTH:
                        acc[q] = acc[q] + win * w[q - s]
            for q in qs:
                v_ref[c, pl.ds(q, V7X_SUBLANES, stride=stride), :] = acc[q]
    ext_c[:, 0:HALO_C, :] = ext_c[:, tm:tm + HALO_C, :]

    b_a = proj(4)
    gate_a = jax.nn.sigmoid(proj(5))
    parts = []
    for c0 in range(0, D_MODEL, CONV_LANE_CHUNK):
        w = [conv_a_ref[0, CONV_A_WIDTH - 1 - j:CONV_A_WIDTH - j, c0:c0 + CONV_LANE_CHUNK]
             for j in range(CONV_A_WIDTH)]
        _realign(ext_a, shift_ref, HALO_A, tm, c0, CONV_LANE_CHUNK, CONV_A_WIDTH - 1)
        parts.append(_causal_taps(ext_a, shift_ref, HALO_A, tm, c0, CONV_LANE_CHUNK, w))
    u_a = jnp.concatenate(parts, axis=1)
    ext_a[0:HALO_A, :] = ext_a[tm:tm + HALO_A, :]
    y_a = jnp.dot((b_a * u_a).astype(BF16), w_out_a_ref[0], preferred_element_type=F32)
    merged = gate_a * y_a

    v = jnp.concatenate([v_ref[c, 0:tm, :] for c in range(N_LANE_CHUNKS)], axis=1)
    v = _layer_norm(v, ln_c_g_ref[0], ln_c_b_ref[0])
    v = v * jax.nn.sigmoid(v)
    y_c = jnp.dot(v.astype(BF16), w_out_c_ref[0], preferred_element_type=F32) + b_out_c_ref[0]
    merged = merged + jax.nn.sigmoid(proj(8)) * y_c

    p_in = proj(6)
    ext_p[HALO_P:HALO_P + tm, :] = p_in
    pos = tile_in_seq * tm + lax.broadcasted_iota(jnp.int32, (tm, POOL_GROUP_DIM), 0)
    parts = []
    for g, win in enumerate(POOL_WINDOWS):
        lo = g * POOL_GROUP_DIM
        sub = []
        for c0 in range(lo, lo + POOL_GROUP_DIM, CONV_LANE_CHUNK):
            _realign(ext_p, shift_ref, HALO_P, tm, c0, CONV_LANE_CHUNK, win - 1)
            sub.append(_causal_taps(ext_p, shift_ref, HALO_P, tm, c0, CONV_LANE_CHUNK,
                                    [None] * win))
        wsum = jnp.concatenate(sub, axis=1)
        cnt = jnp.minimum(pos + 1, win).astype(F32)
        pooled = wsum / cnt - p_in[:, lo:lo + POOL_GROUP_DIM]
        parts.append(jnp.dot(pooled.astype(BF16), w_pool_ref[0, g], preferred_element_type=F32))
    ext_p[0:HALO_P, :] = ext_p[tm:tm + HALO_P, :]
    y_b = jnp.concatenate(parts, axis=1) * scale_pool_ref[0]
    merged = merged + jax.nn.sigmoid(proj(7)) * y_b

    h = jnp.dot(merged.astype(BF16), w_o_ref[0], preferred_element_type=F32)
    y = _layer_norm(DEEPNORM_ALPHA * x_f32[...] + h, ln1_g_ref[0], ln1_b_ref[0])
    y_ref[...] = y
    _store_token_rows(y_rows_ref, y)
    logits_ref[...] = lax.dot_general(wr_ref[0], y.astype(BF16), (((1,), (1,)), ((), ())),
                                      preferred_element_type=F32)


def _layer_block(arr, layer, buffered_once=False):
    tail = (0,) * (arr.ndim - 1)
    mode = {"pipeline_mode": pl.Buffered(1)} if buffered_once else {}
    return pl.BlockSpec((1,) + arr.shape[1:], lambda *_: (layer,) + tail, **mode)


def _mixer(inputs, layer, consts, *, seq_len, n_tok, row0=0, combine_params=None):
    tm = TM_MIX
    assert seq_len % tm == 0 and n_tok % seq_len == 0 and row0 % tm == 0
    assert len(consts) == N_MIXER_CONSTS
    tile0 = row0 // tm
    tok_spec = pl.BlockSpec((tm, D_MODEL), lambda i: (i + tile0, 0))
    if combine_params is None:
        in_specs, args = [tok_spec], list(inputs)
    else:
        ln_g, ln_b, prev_layer = combine_params
        in_specs = [tok_spec,
                    pl.BlockSpec((TOP_K, tm * ROW_WORDS_TILES, V7X_LANES), lambda i: (0, i, 0)),
                    pl.BlockSpec((tm, GATE_LANES), lambda i: (i, 0)),
                    _layer_block(ln_g, prev_layer), _layer_block(ln_b, prev_layer)]
        args = list(inputs) + [ln_g, ln_b]
    return pl.pallas_call(
        functools.partial(_mixer_kernel, tiles_per_seq=seq_len // tm,
                          fused_combine=combine_params is not None),
        out_shape=(jax.ShapeDtypeStruct((n_tok, D_MODEL), F32),
                   jax.ShapeDtypeStruct((n_tok * ROW_WORDS_TILES, V7X_LANES), jnp.int32),
                   jax.ShapeDtypeStruct((N_EXPERTS, n_tok), F32)),
        grid=(n_tok // tm,),
        in_specs=in_specs + [_layer_block(c, layer, buffered_once=True) for c in consts],
        out_specs=(pl.BlockSpec((tm, D_MODEL), lambda i: (i, 0)),
                   pl.BlockSpec((tm * ROW_WORDS_TILES, V7X_LANES), lambda i: (i, 0)),
                   pl.BlockSpec((N_EXPERTS, tm), lambda i: (0, i))),
        scratch_shapes=[pltpu.VMEM((tm, D_MODEL), F32),
                        pltpu.VMEM((tm, D_MODEL), BF16),
                        pltpu.VMEM((HALO_A + tm, D_MODEL), F32),
                        pltpu.VMEM((HALO_P + tm, D_MODEL), F32),
                        pltpu.VMEM((N_LANE_CHUNKS, HALO_C + CONV_C_ROWS, CONV_LANE_CHUNK), F32),
                        pltpu.VMEM((V7X_SUBLANES - 1, HALO_C + tm - V7X_SUBLANES, D_MODEL), F32),
                        pltpu.VMEM((N_LANE_CHUNKS, CONV_C_ROWS, CONV_LANE_CHUNK), F32)],
        compiler_params=pltpu.CompilerParams(dimension_semantics=("arbitrary",),
                                             vmem_limit_bytes=V7X_VMEM_LIMIT_BYTES),
        name="mixer",
    )(*args, *consts)


def _router_kernel(logits_ref, br_ref, idx_ref, gate_ref, rank_ref, counts_ref, counts_row_ref,
                   run_ref, run_row_ref):
    tr = logits_ref.shape[1]

    @pl.when(pl.program_id(0) == 0)
    def _():
        run_ref[...] = jnp.zeros(run_ref.shape, F32)
        run_row_ref[...] = jnp.zeros(run_row_ref.shape, F32)

    expert = lax.broadcasted_iota(jnp.int32, (N_EXPERTS, tr), 0)
    work = logits_ref[...] + br_ref[0]
    vals, idxs = [], []
    for _ in range(TOP_K):
        m = jnp.max(work, axis=0, keepdims=True)
        idx = jnp.min(jnp.where(work == m, expert, N_EXPERTS), axis=0, keepdims=True)
        vals.append(m)
        idxs.append(idx)
        work = jnp.where(expert == idx, NEG_BIG, work)
    exps = [jnp.exp(v - vals[0]) for v in vals]
    denom = exps[0] + exps[1] + exps[2] + exps[3]

    sel = jnp.zeros((N_EXPERTS, tr), F32)
    for k in range(TOP_K):
        sel = sel + jnp.where(expert == idxs[k], 1.0, 0.0)
    earlier = lax.broadcasted_iota(jnp.int32, (tr, tr), 0)
    later = lax.broadcasted_iota(jnp.int32, (tr, tr), 1)
    before = jnp.where(earlier < later, 1.0, 0.0).astype(BF16)
    prior = jnp.dot(sel.astype(BF16), before, preferred_element_type=F32) + run_ref[:, 0:1]

    row = lax.broadcasted_iota(jnp.int32, (V7X_SUBLANES, tr), 0)
    idx_out = jnp.zeros((V7X_SUBLANES, tr), jnp.int32)
    gate_out = jnp.zeros((V7X_SUBLANES, tr), F32)
    rank_out = jnp.zeros((V7X_SUBLANES, tr), F32)
    for k in range(TOP_K):
        rank_k = jnp.sum(jnp.where(expert == idxs[k], prior, 0.0), axis=0, keepdims=True)
        idx_out = jnp.where(row == k, idxs[k], idx_out)
        gate_out = jnp.where(row == k, exps[k] / denom, gate_out)
        rank_out = jnp.where(row == k, rank_k, rank_out)
    idx_ref[...] = idx_out
    rank_ref[...] = rank_out.astype(jnp.int32)
    gate_rows = jnp.concatenate(
        [gate_out, jnp.zeros((GATE_LANES - V7X_SUBLANES, tr), F32)], axis=0)
    gate_ref[...] = gate_rows.T

    run_ref[...] = run_ref[...] + jnp.sum(sel, axis=1, keepdims=True)
    counts_ref[...] = run_ref[...]
    per_tile = lax.dot_general(jnp.ones((V7X_SUBLANES, tr), BF16), sel.astype(BF16),
                               (((1,), (1,)), ((), ())), preferred_element_type=F32)
    run_row_ref[:, 0:N_EXPERTS] = run_row_ref[:, 0:N_EXPERTS] + per_tile
    counts_row_ref[...] = run_row_ref[...]


def _router(logits_t, layer, br_t):
    t_tok = logits_t.shape[1]
    tr = TR_ROUTE
    tok_spec = pl.BlockSpec((V7X_SUBLANES, tr), lambda i: (0, i))
    return pl.pallas_call(
        _router_kernel,
        out_shape=(jax.ShapeDtypeStruct((V7X_SUBLANES, t_tok), jnp.int32),
                   jax.ShapeDtypeStruct((t_tok, GATE_LANES), F32),
                   jax.ShapeDtypeStruct((V7X_SUBLANES, t_tok), jnp.int32),
                   jax.ShapeDtypeStruct((N_EXPERTS, V7X_LANES), F32),
                   jax.ShapeDtypeStruct((V7X_SUBLANES, V7X_LANES), F32)),
        grid=(t_tok // tr,),
        in_specs=[pl.BlockSpec((N_EXPERTS, tr), lambda i: (0, i)), _layer_block(br_t, layer)],
        out_specs=(tok_spec, pl.BlockSpec((tr, GATE_LANES), lambda i: (i, 0)), tok_spec,
                   pl.BlockSpec((N_EXPERTS, V7X_LANES), lambda i: (0, 0)),
                   pl.BlockSpec((V7X_SUBLANES, V7X_LANES), lambda i: (0, 0))),
        scratch_shapes=[pltpu.VMEM((N_EXPERTS, V7X_LANES), F32),
                        pltpu.VMEM((V7X_SUBLANES, V7X_LANES), F32)],
        compiler_params=pltpu.CompilerParams(dimension_semantics=("arbitrary",)),
        name="router",
    )(logits_t, br_t)


def _slots_kernel(idx_ref, rank_ref, counts_ref, counts_row_ref, dest_ref, table_ref, *,
                  n_blocks):
    bm = BM_EXPERT
    shift = bm.bit_length() - 1
    assert bm == 1 << shift
    width = table_ref.shape[1]
    cnt_col = counts_ref[:, 0:1].astype(jnp.int32)
    cnt_row = counts_row_ref[0:1, :].astype(jnp.int32)
    nblk_col = lax.shift_right_logical(cnt_col + (bm - 1), shift)
    nblk_row = lax.shift_right_logical(cnt_row + (bm - 1), shift)
    e_sub = lax.broadcasted_iota(jnp.int32, (N_EXPERTS, V7X_LANES), 0)
    e_lane = lax.broadcasted_iota(jnp.int32, (N_EXPERTS, V7X_LANES), 1)
    blk_start = jnp.sum(jnp.where(e_lane < e_sub, nblk_row.astype(F32), 0.0), axis=1,
                        keepdims=True).astype(jnp.int32)
    blk_end = blk_start + nblk_col
    row_start = blk_start * bm

    dest = rank_ref[...]
    idx = idx_ref[...]
    for e in range(N_EXPERTS):
        dest = dest + jnp.where(idx == e, row_start[e:e + 1, :], 0)
    dest_ref[...] = dest

    b_lane = lax.broadcasted_iota(jnp.int32, (N_EXPERTS, width), 1)
    e_of = lax.broadcasted_iota(jnp.int32, (N_EXPERTS, width), 0)
    block_e = jnp.minimum(jnp.sum(jnp.where(blk_end <= b_lane, 1, 0), axis=0, keepdims=True),
                          N_EXPERTS - 1)
    mine = e_of == block_e
    cnt_of = jnp.sum(jnp.where(mine, cnt_col, 0), axis=0, keepdims=True)
    start_of = jnp.sum(jnp.where(mine, blk_start, 0), axis=0, keepdims=True)
    nb_used = jnp.sum(nblk_col, axis=0, keepdims=True)
    b_row = b_lane[0:1, :]
    block_rows = jnp.clip(cnt_of - (b_row - start_of) * bm, 0, bm)
    block_rows = jnp.where((b_row < nb_used) & (b_row < n_blocks), block_rows, 0)
    later = jnp.where((e_sub > e_lane) & (cnt_col > 0), e_sub, N_EXPERTS)
    next_e = jnp.min(later, axis=0, keepdims=True)
    next_e = jnp.where(next_e == N_EXPERTS, -1, next_e)

    row = lax.broadcasted_iota(jnp.int32, (V7X_SUBLANES, width), 0)
    lane = lax.broadcasted_iota(jnp.int32, (V7X_SUBLANES, width), 1)
    next_wide = jnp.concatenate(
        [next_e, jnp.full((1, width - V7X_LANES), -1, jnp.int32)], axis=1)
    table = jnp.where(row == 0, block_e, 0)
    table = jnp.where(row == 1, block_rows, table)
    table = jnp.where(row == 2, next_wide, table)
    table = jnp.where((row == 3) & (lane == 0), nb_used, table)
    table_ref[...] = table


def _slots(idx_t, rank_t, counts_col, counts_row, n_blocks):
    assert n_blocks <= SLOT_TABLE_LANES
    return pl.pallas_call(
        functools.partial(_slots_kernel, n_blocks=n_blocks),
        out_shape=(jax.ShapeDtypeStruct(idx_t.shape, jnp.int32),
                   jax.ShapeDtypeStruct((V7X_SUBLANES, SLOT_TABLE_LANES), jnp.int32)),
        name="slots",
    )(idx_t, rank_t, counts_col, counts_row)


def _sc_mesh():
    return plsc.VectorSubcoreMesh(core_axis_name="c", subcore_axis_name="s")


def _sc_worker_id():
    return lax.axis_index("s") * V7X_SC_CORES + lax.axis_index("c")


def _dispatch_rows(x3, dest_km, n_slots):
    t_tok = x3.shape[0]
    tok_per_worker = t_tok // V7X_SC_WORKERS
    assert tok_per_worker % SC_ROWS == 0

    def body(x_hbm, dest_hbm, out_hbm, rows_v, idx_v, sem):
        base = _sc_worker_id() * tok_per_worker

        @pl.loop(0, tok_per_worker // SC_ROWS)
        def _(step):
            t0 = pl.multiple_of(base + step * SC_ROWS, SC_ROWS)
            pltpu.sync_copy(x_hbm.at[pl.ds(t0, SC_ROWS)], rows_v)
            for k in range(TOP_K):
                pltpu.sync_copy(dest_hbm.at[k, pl.ds(t0, SC_ROWS)], idx_v.at[k])
            for k in range(TOP_K):
                pltpu.async_copy(rows_v, out_hbm.at[idx_v.at[k]], sem).wait()

    return pl.kernel(
        body, mesh=_sc_mesh(),
        out_type=jax.ShapeDtypeStruct((n_slots,) + x3.shape[1:], x3.dtype),
        scratch_types=[pltpu.VMEM((SC_ROWS,) + x3.shape[1:], x3.dtype),
                       pltpu.VMEM((TOP_K, SC_ROWS), jnp.int32),
                       pltpu.SemaphoreType.DMA],
    )(x3, dest_km)


def _gather_rows(ys3, dest_km):
    t_tok = dest_km.shape[1]
    workers_per_k = V7X_SC_WORKERS // TOP_K
    tok_per_worker = t_tok // workers_per_k
    assert V7X_SC_WORKERS % TOP_K == 0 and tok_per_worker % SC_ROWS == 0

    def body(ys_hbm, src_hbm, out_hbm, rows_v, idx_v, sem):
        wid = _sc_worker_id()
        k = wid // workers_per_k
        base = (wid % workers_per_k) * tok_per_worker

        @pl.loop(0, tok_per_worker // SC_ROWS)
        def _(step):
            t0 = pl.multiple_of(base + step * SC_ROWS, SC_ROWS)
            pltpu.sync_copy(src_hbm.at[k, pl.ds(t0, SC_ROWS)], idx_v)
            pltpu.async_copy(ys_hbm.at[idx_v], rows_v, sem).wait()
            pltpu.sync_copy(rows_v, out_hbm.at[pl.ds(pl.multiple_of(k * t_tok + t0, SC_ROWS),
                                                     SC_ROWS)])

    return pl.kernel(
        body, mesh=_sc_mesh(),
        out_type=jax.ShapeDtypeStruct((TOP_K * t_tok,) + ys3.shape[1:], ys3.dtype),
        scratch_types=[pltpu.VMEM((SC_ROWS,) + ys3.shape[1:], ys3.dtype),
                       pltpu.VMEM((SC_ROWS,), jnp.int32),
                       pltpu.SemaphoreType.DMA],
    )(ys3, dest_km)


def _expert_kernel(table_ref, xs_ref, w_gu_hbm, b_gu_ref, w_down_hbm, b_down_ref, ys_ref,
                   w_gu_f, w_down_f, w_gu_b, w_down_b, sem_gu, sem_down, slot_ref, *, layer):
    bm = BM_EXPERT
    b = pl.program_id(0)
    e = table_ref[0, b]
    n_rows = table_ref[1, b]
    live = n_rows > 0
    new_expert = jnp.logical_or(b == 0, e != table_ref[0, jnp.maximum(b - 1, 0)])

    def weight_copies(expert, slot):
        idx = layer * N_EXPERTS + expert
        return (pltpu.make_async_copy(w_gu_hbm.at[idx], w_gu_f.at[slot], sem_gu.at[slot]),
                pltpu.make_async_copy(w_down_hbm.at[idx], w_down_f.at[slot], sem_down.at[slot]))

    @pl.when(b == 0)
    def _():
        slot_ref[0] = 0
        for copy in weight_copies(e, 0):
            copy.start()

    @pl.when(jnp.logical_and(new_expert, live))
    def _():
        slot = slot_ref[0]
        nxt = table_ref[2, e]

        @pl.when(nxt >= 0)
        def _():
            for copy in weight_copies(nxt, 1 - slot):
                copy.start()

        for copy in weight_copies(e, slot):
            copy.wait()
        slot_ref[0] = 1 - slot

        cb = DEINTERLEAVE_COLS
        src = lax.broadcasted_iota(jnp.int32, (cb, cb), 0)
        dst = lax.broadcasted_iota(jnp.int32, (cb, cb), 1)
        pick = jnp.where(dst < cb // 2, 2 * dst, 2 * (dst - cb // 2) + 1)
        sel = jnp.where(src == pick, 1.0, 0.0).astype(BF16)
        for c in range(2 * D_FF // cb):
            blk = jnp.dot(w_gu_f[slot, :, c * cb:(c + 1) * cb].astype(BF16), sel,
                          preferred_element_type=F32).astype(BF16)
            lo = c * (cb // 2)
            w_gu_b[:, lo:lo + cb // 2] = blk[:, :cb // 2]
            w_gu_b[:, D_FF + lo:D_FF + lo + cb // 2] = blk[:, cb // 2:]
        w_down_b[...] = w_down_f[slot].astype(BF16)

    def expert_rows(rows):
        xb = _load_token_rows(xs_ref, rows).astype(BF16)
        h = jnp.dot(xb, w_gu_b[...], preferred_element_type=F32) + b_gu_ref[e]
        gate = jnp.minimum(h[:, :D_FF], SWIGLU_LIMIT)
        up = jnp.clip(h[:, D_FF:], -SWIGLU_LIMIT, SWIGLU_LIMIT)
        act = (up + 1.0) * (gate * jax.nn.sigmoid(SWIGLU_ALPHA * gate))
        y = jnp.dot(act.astype(BF16), w_down_b[...], preferred_element_type=F32) + b_down_ref[e]
        _store_token_rows(ys_ref, y)

    sizes = list(EXPERT_TAIL_ROWS) + [bm]
    for lo, rows in zip([0] + sizes[:-1], sizes):
        @pl.when(jnp.logical_and(n_rows > lo, n_rows <= rows))
        def _(rows=rows):
            expert_rows(rows)


def _experts(table, xs_rows, layer, w_gu, b_gu, w_down, b_down):
    bm = BM_EXPERT
    n_blocks = xs_rows.shape[0] // (bm * ROW_WORDS_TILES)

    def row_map(b, tbl):
        return (jnp.minimum(b, tbl[3, 0] - 1), 0)

    def layer_biases(width):
        return pl.BlockSpec((N_EXPERTS, 1, width), lambda b, tbl: (layer, 0, 0))

    row_block = (bm * ROW_WORDS_TILES, V7X_LANES)
    grid_spec = pltpu.PrefetchScalarGridSpec(
        num_scalar_prefetch=1,
        grid=(n_blocks,),
        in_specs=[pl.BlockSpec(row_block, row_map),
                  pl.BlockSpec(memory_space=pl.ANY),
                  layer_biases(2 * D_FF),
                  pl.BlockSpec(memory_space=pl.ANY),
                  layer_biases(D_MODEL)],
        out_specs=pl.BlockSpec(row_block, row_map),
        scratch_shapes=[pltpu.VMEM((2, D_MODEL, 2 * D_FF), F32),
                        pltpu.VMEM((2, D_FF, D_MODEL), F32),
                        pltpu.VMEM((D_MODEL, 2 * D_FF), BF16),
                        pltpu.VMEM((D_FF, D_MODEL), BF16),
                        pltpu.SemaphoreType.DMA((2,)),
                        pltpu.SemaphoreType.DMA((2,)),
                        pltpu.SMEM((1,), jnp.int32)],
    )
    return pl.pallas_call(
        functools.partial(_expert_kernel, layer=layer),
        out_shape=jax.ShapeDtypeStruct(xs_rows.shape, xs_rows.dtype),
        grid_spec=grid_spec,
        compiler_params=pltpu.CompilerParams(dimension_semantics=("arbitrary",),
                                             vmem_limit_bytes=V7X_VMEM_LIMIT_BYTES),
        name="experts",
    )(table, xs_rows, w_gu, b_gu, w_down, b_down)


def _combine_kernel(x_ref, yk_ref, gate_ref, g_ref, b_ref, *out_refs):
    o_ref = out_refs[-1]
    o_ref[...] = _combine_rows(x_ref[...], yk_ref, gate_ref[...], g_ref[0], b_ref[0])


def _combine(x1, yk4, gates_pad, layer, ln_g, ln_b, *, out_rows, row0, out_so_far=None):
    t_tok = x1.shape[0]
    tm = TM_COMB
    assert row0 % tm == 0
    tile0 = row0 // tm
    in_specs = [pl.BlockSpec((tm, D_MODEL), lambda i: (i, 0)),
                pl.BlockSpec((TOP_K, tm * ROW_WORDS_TILES, V7X_LANES), lambda i: (0, i, 0)),
                pl.BlockSpec((tm, GATE_LANES), lambda i: (i, 0)),
                _layer_block(ln_g, layer), _layer_block(ln_b, layer)]
    args = [x1, yk4, gates_pad, ln_g, ln_b]
    aliases = {}
    if out_so_far is not None:
        in_specs.append(pl.BlockSpec(memory_space=pl.ANY))
        args.append(out_so_far)
        aliases = {len(args) - 1: 0}
    return pl.pallas_call(
        _combine_kernel,
        out_shape=jax.ShapeDtypeStruct((out_rows, D_MODEL), F32),
        grid=(t_tok // tm,),
        in_specs=in_specs,
        out_specs=pl.BlockSpec((tm, D_MODEL), lambda i: (i + tile0, 0)),
        input_output_aliases=aliases,
        compiler_params=pltpu.CompilerParams(dimension_semantics=("arbitrary",)),
        name="combine",
    )(*args)


def _moe_rows(x1_rows, logits_t, layer, br_t, w_gu, b_gu, w_down, b_down):
    t_tok = logits_t.shape[1]
    n_assign = t_tok * TOP_K
    bm = BM_EXPERT
    n_blocks = n_assign // bm + N_EXPERTS
    n_slots = n_blocks * bm

    idx_t, gates_pad, rank_t, counts_col, counts_row = _router(logits_t, layer, br_t)
    dest_km, table = _slots(idx_t, rank_t, counts_col, counts_row, n_blocks)

    tile = (ROW_WORDS_TILES, V7X_LANES)
    xs3 = _dispatch_rows(x1_rows.reshape(t_tok, *tile), dest_km, n_slots)
    ys_rows = _experts(table, xs3.reshape(n_slots * ROW_WORDS_TILES, V7X_LANES),
                       layer, w_gu, b_gu, w_down, b_down)
    yk3 = _gather_rows(ys_rows.reshape(n_slots, *tile), dest_km)
    return yk3.reshape(TOP_K, t_tok * ROW_WORDS_TILES, V7X_LANES), gates_pad


def kernel(x, w_in, b_in, conv_a, w_out_a, w_pool, scale_pool, conv_c, conv_c_b, ln_c_g, ln_c_b,
           w_out_c, b_out_c, w_o, ln1_g, ln1_b, w_router, b_router, w_gu, b_gu, w_down, b_down,
           ln2_g, ln2_b):
    bsz, seq_len, d = x.shape
    assert d == D_MODEL
    t_tok = bsz * seq_len
    depth = w_in.shape[0]

    def row(v):
        return v[:, None, :]

    wr_t = jnp.transpose(w_router, (0, 2, 1)).astype(BF16)
    mixer_consts = (w_in.astype(BF16), row(b_in), conv_a, w_out_a.astype(BF16),
                    w_pool.astype(BF16), row(scale_pool), conv_c, row(conv_c_b), row(ln_c_g),
                    row(ln_c_b), w_out_c.astype(BF16), row(b_out_c), w_o.astype(BF16),
                    row(ln1_g), row(ln1_b), wr_t)
    br_t = jnp.broadcast_to(b_router[:, :, None], (depth, N_EXPERTS, TR_ROUTE))
    n_le = depth * N_EXPERTS
    w_gu_f = w_gu.reshape(n_le, D_MODEL, 2 * D_FF)
    b_gu_d = jnp.concatenate([b_gu[..., 0::2], b_gu[..., 1::2]], axis=-1).reshape(n_le, 1, 2 * D_FF)
    w_down_f = w_down.reshape(n_le, D_FF, D_MODEL)
    b_down_d = b_down.reshape(n_le, 1, D_MODEL)
    ln2_g_r, ln2_b_r = row(ln2_g), row(ln2_b)

    assert bsz % TOKEN_GROUPS == 0
    per_group = t_tok // TOKEN_GROUPS
    x2 = x.reshape(t_tok, d)
    state = [None] * TOKEN_GROUPS
    for layer in range(depth):
        for g in range(TOKEN_GROUPS):
            if layer == 0:
                x1, x1_rows, logits_t = _mixer((x2,), layer, mixer_consts, seq_len=seq_len,
                                               n_tok=per_group, row0=g * per_group)
            else:
                x1, x1_rows, logits_t = _mixer(state[g], layer, mixer_consts, seq_len=seq_len,
                                               n_tok=per_group,
                                               combine_params=(ln2_g_r, ln2_b_r, layer - 1))
            yk4, gates = _moe_rows(x1_rows, logits_t, layer, br_t, w_gu_f, b_gu_d, w_down_f,
                                   b_down_d)
            state[g] = (x1, yk4, gates)
    out = None
    for g, (x1, yk4, gates) in enumerate(state):
        out = _combine(x1, yk4, gates, depth - 1, ln2_g_r, ln2_b_r, out_rows=t_tok,
                       row0=g * per_group, out_so_far=out)
    return out.reshape(bsz, seq_len, d)
```
